```python
import math
import jax, jax.numpy as jnp
from jax import lax
import numpy as np

D_MODEL = 1024
BATCH = 32
SEQ = 2048
DEPTH = 4

N_MIXERS = 4
LAYERS_PER_MIXER = tuple(len(range(m, DEPTH, N_MIXERS)) for m in range(N_MIXERS))
DN_ALPHA = (2.0 * DEPTH) ** 0.25
DN_BETA = (8.0 * DEPTH) ** -0.25
LN_EPS = 1e-5
NEG_INF = -1e30
Q_BLOCK = 128
GATHER_BLOCK = 32

MLA_HEADS = 16
MLA_Q_RANK = 384
MLA_KV_RANK = 256
MLA_NOPE = 64
MLA_ROPE = 32
MLA_V = 64
ROPE_THETA = 10000.0
MLA_IN = MLA_Q_RANK + MLA_KV_RANK + MLA_ROPE

NSA_HEADS = 16
NSA_KV_GROUPS = 4
NSA_HD = 64
NSA_CMP_LEN = 32
NSA_CMP_STRIDE = 16
NSA_SLC_LEN = 64
NSA_TOP_N = 8
NSA_WINDOW = 512
NSA_FORCE = 1e4
NSA_IN = NSA_HEADS * NSA_HD + 6 * NSA_KV_GROUPS * NSA_HD + 3 * NSA_HEADS

DIFF_HEADS = 8
DIFF_HD = 64
DIFF_IN = 3 * DIFF_HEADS * 2 * DIFF_HD

DIL_PATTERNS = ((128, 1), (512, 4), (2048, 16))
DIL_HEADS = 8
DIL_HD = 64
DIL_IN = len(DIL_PATTERNS) * 3 * DIL_HEADS * DIL_HD

MOE_GROUPS = 4
MOE_EXPERTS_PER_GROUP = 4
MOE_EXPERTS = MOE_GROUPS * MOE_EXPERTS_PER_GROUP
MOE_FF = 512
MOE_TOPK = 2

kernel_name = 'hybrid_mla_nsa_diff_dilated_hmoe_deepnorm'


def layer_norm(x, g, b):
    xf = x.astype(jnp.float32)
    mu = jnp.mean(xf, axis=-1, keepdims=True)
    var = jnp.mean(jnp.square(xf - mu), axis=-1, keepdims=True)
    return ((xf - mu) * lax.rsqrt(var + LN_EPS) * g + b).astype(x.dtype)


def rms_norm(x, g, eps=1e-6):
    xf = x.astype(jnp.float32)
    return (xf * lax.rsqrt(jnp.mean(xf * xf, axis=-1, keepdims=True) + eps) * g).astype(x.dtype)


def alibi_slopes(n):
    return jnp.asarray(2.0 ** (-8.0 * np.arange(1, n + 1) / n), jnp.float32)


def rope(x, pos):
    half = x.shape[-1] // 2
    freq = ROPE_THETA ** (-jnp.arange(half, dtype=jnp.float32) / half)
    ang = pos.astype(jnp.float32)[:, None] * freq
    cos, sin = jnp.cos(ang)[:, None, :], jnp.sin(ang)[:, None, :]
    x1, x2 = x[..., :half].astype(jnp.float32), x[..., half:].astype(jnp.float32)
    return jnp.concatenate([x1 * cos - x2 * sin, x1 * sin + x2 * cos], axis=-1).astype(x.dtype)


def split_blocks(a, blk):
    b, s = a.shape[:2]
    return jnp.moveaxis(a.reshape(b, s // blk, blk, *a.shape[2:]), 1, 0)


def merge_blocks(a):
    nb, b, blk = a.shape[:3]
    return jnp.moveaxis(a, 0, 1).reshape(b, nb * blk, *a.shape[3:])


def mla_mixer(x, w_in, q_norm, kv_norm, w_qb, w_kvb, w_o):
    b, s, _ = x.shape
    h, dq = MLA_HEADS, MLA_NOPE + MLA_ROPE
    c_q, c_kv, k_rope = jnp.split(x @ w_in, [MLA_Q_RANK, MLA_Q_RANK + MLA_KV_RANK], axis=-1)
    q = (rms_norm(c_q, q_norm) @ w_qb).reshape(b, s, h, dq)
    kv = (rms_norm(c_kv, kv_norm) @ w_kvb).reshape(b, s, h, MLA_NOPE + MLA_V)
    pos = jnp.arange(s)
    q = jnp.concatenate([q[..., :MLA_NOPE], rope(q[..., MLA_NOPE:], pos)], axis=-1)
    k_r = jnp.broadcast_to(rope(k_rope[:, :, None, :], pos), (b, s, h, MLA_ROPE))
    k = jnp.concatenate([kv[..., :MLA_NOPE], k_r], axis=-1)
    v = kv[..., MLA_NOPE:]
    scale = dq ** -0.5

    def attend(args):
        i, q_blk = args
        t = i * Q_BLOCK + jnp.arange(Q_BLOCK)
        sc = jnp.einsum('bqhd,bkhd->bhqk', q_blk, k).astype(jnp.float32) * scale
        sc = jnp.where(pos[None, :] <= t[:, None], sc, NEG_INF)
        p = jax.nn.softmax(sc, axis=-1).astype(v.dtype)
        return jnp.einsum('bhqk,bkhd->bqhd', p, v)

    o = merge_blocks(lax.map(attend, (jnp.arange(s // Q_BLOCK), split_blocks(q, Q_BLOCK))))
    return o.reshape(b, s, h * MLA_V) @ w_o


def nsa_mixer(x, w_in, w_phi_k, w_phi_v, cmp_pos, w_o):
    b, s, _ = x.shape
    h, g, d = NSA_HEADS, NSA_KV_GROUPS, NSA_HD
    hpg = h // g
    blk, W, Ls = GATHER_BLOCK, NSA_WINDOW, NSA_SLC_LEN
    cuts = [h * d + i * g * d for i in range(7)]
    q, kc, vc, ks, vs, kw, vw, gl = jnp.split(x @ w_in, cuts, axis=-1)
    q = q.reshape(b, s, g, hpg, d)
    kc, vc, ks, vs, kw, vw = [a.reshape(b, s, g, d) for a in (kc, vc, ks, vs, kw, vw)]
    gates = jax.nn.sigmoid(gl.astype(jnp.float32)).reshape(b, s, g, hpg, 3)
    slopes = alibi_slopes(h).reshape(g, hpg)
    scale = d ** -0.5
    n_cmp = (s - NSA_CMP_LEN) // NSA_CMP_STRIDE + 1
    cmp_start = np.arange(n_cmp) * NSA_CMP_STRIDE
    cmp_idx = cmp_start[:, None] + np.arange(NSA_CMP_LEN)[None, :]

    def compress(a, w_phi):
        blocks = a[:, cmp_idx] + cmp_pos[:, None, :]
        blocks = jnp.swapaxes(blocks, 2, 3).reshape(b, n_cmp, g, NSA_CMP_LEN * d)
        return blocks @ w_phi

    k_cmp, v_cmp = compress(kc, w_phi_k), compress(vc, w_phi_v)
    cmp_end = jnp.asarray(cmp_start + NSA_CMP_LEN - 1, jnp.float32)
    n_slc = s // Ls
    slc_start = np.arange(n_slc) * Ls
    overlap = (cmp_start[:, None] < slc_start[None, :] + Ls) & (cmp_start[:, None] + NSA_CMP_LEN > slc_start[None, :])
    cmp_to_slc = jnp.asarray(overlap, jnp.float32)
    top_n = min(NSA_TOP_N, n_slc)
    ks_blk = jnp.moveaxis(ks.reshape(b, n_slc, Ls, g, d), 3, 1)
    vs_blk = jnp.moveaxis(vs.reshape(b, n_slc, Ls, g, d), 3, 1)
    kw_pad = jnp.pad(kw, ((0, 0), (W, 0), (0, 0), (0, 0)))
    vw_pad = jnp.pad(vw, ((0, 0), (W, 0), (0, 0), (0, 0)))
    bi = jnp.arange(b)[:, None, None, None]
    gi = jnp.arange(g)[None, :, None, None]
    sidx = jnp.arange(n_slc)
    slc_start_j = jnp.asarray(slc_start)

    def attend(args):
        i, q_blk, g_blk = args
        t = i * blk + jnp.arange(blk)
        tf = t.astype(jnp.float32)
        dist_c = tf[:, None] - cmp_end[None, :]
        vis_c = dist_c >= 0
        sc = jnp.einsum('bqgjd,bngd->bgjqn', q_blk, k_cmp).astype(jnp.float32) * scale
        sc = jnp.where(vis_c, sc - slopes[:, :, None, None] * dist_c, NEG_INF)
        p_c = jax.nn.softmax(sc, axis=-1) * jnp.any(vis_c, axis=-1)[:, None]
        o_c = jnp.einsum('bgjqn,bngd->bqgjd', p_c.astype(v_cmp.dtype), v_cmp)
        imp = jnp.einsum('bgjqn,ns->bgqs', p_c, cmp_to_slc)
        forced = (sidx[None, :] == 0) | (sidx[None, :] == (t // Ls)[:, None])
        future = slc_start_j[None, :] > t[:, None]
        score = imp + jnp.where(forced, NSA_FORCE, 0.0) - jnp.where(future, 2.0 * NSA_FORCE, 0.0)
        _, top = lax.top_k(score, top_n)
        k_sel = ks_blk[bi, gi, top]
        v_sel = vs_blk[bi, gi, top]
        pos_sel = top[..., None] * Ls + jnp.arange(Ls)
        dist_s = (t[:, None, None] - pos_sel).astype(jnp.float32)[:, :, None]
        sc = jnp.einsum('bqgjd,bgqnld->bgjqnl', q_blk, k_sel).astype(jnp.float32) * scale
        sc = jnp.where(dist_s >= 0, sc - slopes[:, :, None, None, None] * dist_s, NEG_INF)
        p_s = jax.nn.softmax(sc.reshape(b, g, hpg, blk, top_n * Ls), axis=-1).reshape(sc.shape)
        o_s = jnp.einsum('bgjqnl,bgqnld->bqgjd', p_s.astype(v_sel.dtype), v_sel)
        k_w = lax.dynamic_slice_in_dim(kw_pad, i * blk, W + blk, axis=1)
        v_w = lax.dynamic_slice_in_dim(vw_pad, i * blk, W + blk, axis=1)
        pos_w = i * blk - W + jnp.arange(W + blk)
        dist_w = t[:, None] - pos_w[None, :]
        vis_w = (dist_w >= 0) & (dist_w < W) & (pos_w[None, :] >= 0)
        sc = jnp.einsum('bqgjd,bkgd->bgjqk', q_blk, k_w).astype(jnp.float32) * scale
        sc = jnp.where(vis_w, sc - slopes[:, :, None, None] * dist_w.astype(jnp.float32), NEG_INF)
        p_w = jax.nn.softmax(sc, axis=-1).astype(v_w.dtype)
        o_w = jnp.einsum('bgjqk,bkgd->bqgjd', p_w, v_w)
        o = g_blk[..., 0:1] * o_c + g_blk[..., 1:2] * o_s + g_blk[..., 2:3] * o_w
        return o.astype(x.dtype)

    o = merge_blocks(lax.map(attend, (jnp.arange(s // blk), split_blocks(q, blk), split_blocks(gates, blk))))
    return o.reshape(b, s, h * d) @ w_o


def diff_mixer(x, w_in, lam_q1, lam_k1, lam_q2, lam_k2, subln, w_o, layer_idx):
    b, s, _ = x.shape
    h, d = DIFF_HEADS, DIFF_HD
    q, k, v = jnp.split(x @ w_in, 3, axis=-1)
    q = q.reshape(b, s, h, 2, d)
    k = k.reshape(b, s, h, 2, d)
    v = v.reshape(b, s, h, 2 * d)
    lam_init = 0.8 - 0.6 * math.exp(-0.3 * layer_idx)
    f32 = jnp.float32
    lam = (jnp.exp(jnp.sum(lam_q1.astype(f32) * lam_k1.astype(f32)))
           - jnp.exp(jnp.sum(lam_q2.astype(f32) * lam_k2.astype(f32))) + lam_init)
    slopes = alibi_slopes(h)
    pos = jnp.arange(s)
    scale = d ** -0.5

    def attend(args):
        i, q_blk = args
        t = i * Q_BLOCK + jnp.arange(Q_BLOCK)
        dist = (t[:, None] - pos[None, :]).astype(f32)
        sc = jnp.einsum('bqhmd,bkhmd->bmhqk', q_blk, k).astype(f32) * scale
        sc = jnp.where(dist >= 0, sc - slopes[:, None, None] * dist, NEG_INF)
        p = jax.nn.softmax(sc, axis=-1)
        a = (p[:, 0] - lam * p[:, 1]).astype(v.dtype)
        return jnp.einsum('bhqk,bkhd->bqhd', a, v)

    o = merge_blocks(lax.map(attend, (jnp.arange(s // Q_BLOCK), split_blocks(q, Q_BLOCK))))
    o = rms_norm(o, subln, eps=1e-5) * (1.0 - lam_init)
    return o.reshape(b, s, h * 2 * d) @ w_o


def dilated_mixer(x, w_in, w_o):
    b, s, _ = x.shape
    h, d, blk = DIL_HEADS, DIL_HD, GATHER_BLOCK
    n_pat = len(DIL_PATTERNS)
    qkv = (x @ w_in).reshape(b, s, n_pat, 3, h, d)
    q = qkv[:, :, :, 0]
    k_pats = [qkv[:, :, p, 1] for p in range(n_pat)]
    v_pats = [qkv[:, :, p, 2] for p in range(n_pat)]
    slopes = alibi_slopes(h)
    scale = d ** -0.5

    def attend(args):
        i, q_blk = args
        t = i * blk + jnp.arange(blk)
        outs, maxes, denoms = [], [], []
        for p, (win, dil) in enumerate(DIL_PATTERNS):
            dist = np.arange(win // dil + 1) * dil
            idx = t[:, None] - dist[None, :]
            valid = idx >= 0
            idx = jnp.maximum(idx, 0)
            k_g = k_pats[p][:, idx]
            v_g = v_pats[p][:, idx]
            sc = jnp.einsum('bqhd,bqkhd->bhqk', q_blk[:, :, p], k_g).astype(jnp.float32) * scale
            sc = jnp.where(valid, sc - slopes[:, None, None] * jnp.asarray(dist, jnp.float32), NEG_INF)
            mx = jnp.max(sc, axis=-1)
            e = jnp.exp(sc - mx[..., None])
            den = jnp.sum(e, axis=-1)
            outs.append(jnp.einsum('bhqk,bqkhd->bqhd', (e / den[..., None]).astype(v_g.dtype), v_g))
            maxes.append(mx)
            denoms.append(den)
        m_all = jnp.max(jnp.stack(maxes), axis=0)
        wts = jnp.stack([den * jnp.exp(mx - m_all) for mx, den in zip(maxes, denoms)])
        wts = wts / jnp.sum(wts, axis=0)
        o = sum(jnp.swapaxes(wt, 1, 2)[..., None] * o_p for wt, o_p in zip(wts, outs))
        return o.astype(x.dtype)

    o = merge_blocks(lax.map(attend, (jnp.arange(s // blk), split_blocks(q, blk))))
    return o.reshape(b, s, h * d) @ w_o


def hier_moe(x, wg, bg, we, be, w1, w3, w2):
    b, s, dm = x.shape
    xt = x.reshape(b * s, dm)
    n = xt.shape[0]
    pg = jax.nn.softmax((xt @ wg + bg).astype(jnp.float32), axis=-1)
    g_top = jnp.argmax(pg, axis=-1)
    pg_top = jnp.take_along_axis(pg, g_top[:, None], axis=-1)[:, 0]
    le = jnp.einsum('nd,gde->nge', xt, we) + be
    le = jnp.take_along_axis(le, g_top[:, None, None], axis=1)[:, 0]
    pe = jax.nn.softmax(le.astype(jnp.float32), axis=-1)
    top_v, top_i = lax.top_k(pe, MOE_TOPK)
    top_v = top_v / jnp.sum(top_v, axis=-1, keepdims=True)
    expert_id = g_top[:, None] * MOE_EXPERTS_PER_GROUP + top_i
    gate = jnp.sum(jax.nn.one_hot(expert_id, MOE_EXPERTS, dtype=jnp.float32)
                   * (top_v * pg_top[:, None])[..., None], axis=1).astype(x.dtype)
    y = jnp.zeros_like(xt)
    for e in range(MOE_EXPERTS):
        hid = jax.nn.silu(xt @ w1[e]) * (xt @ w3[e])
        y = y + gate[:, e:e + 1] * (hid @ w2[e])
    return y.reshape(b, s, dm)


def setup_inputs(seed: int = 0) -> dict:
    key = jax.random.key(seed)
    keys = jax.random.split(key, 40)
    counter = iter(range(40))
    f32 = jnp.float32
    D = D_MODEL
    nA, nB, nC, nD = LAYERS_PER_MIXER

    def nrm(shape, std):
        return jax.random.normal(keys[next(counter)], shape, f32) * std

    def gain(shape):
        return 1.0 + nrm(shape, 0.02)

    return {
        'x': nrm((BATCH, SEQ, D), 1.0),
        'mla_w_in': nrm((nA, D, MLA_IN), D ** -0.5),
        'mla_q_norm': gain((nA, MLA_Q_RANK)),
        'mla_kv_norm': gain((nA, MLA_KV_RANK)),
        'mla_w_qb': nrm((nA, MLA_Q_RANK, MLA_HEADS * (MLA_NOPE + MLA_ROPE)), MLA_Q_RANK ** -0.5),
        'mla_w_kvb': nrm((nA, MLA_KV_RANK, MLA_HEADS * (MLA_NOPE + MLA_V)), MLA_KV_RANK ** -0.5),
        'mla_w_o': nrm((nA, MLA_HEADS * MLA_V, D), DN_BETA * (MLA_HEADS * MLA_V) ** -0.5),
        'nsa_w_in': nrm((nB, D, NSA_IN), D ** -0.5),
        'nsa_w_phi_k': nrm((nB, NSA_CMP_LEN * NSA_HD, NSA_HD), (NSA_CMP_LEN * NSA_HD) ** -0.5),
        'nsa_w_phi_v': nrm((nB, NSA_CMP_LEN * NSA_HD, NSA_HD), (NSA_CMP_LEN * NSA_HD) ** -0.5),
        'nsa_cmp_pos': nrm((nB, NSA_CMP_LEN, NSA_HD), 0.1),
        'nsa_w_o': nrm((nB, NSA_HEADS * NSA_HD, D), DN_BETA * (NSA_HEADS * NSA_HD) ** -0.5),
        'diff_w_in': nrm((nC, D, DIFF_IN), D ** -0.5),
        'diff_lam_q1': nrm((nC, DIFF_HD), 0.1),
        'diff_lam_k1': nrm((nC, DIFF_HD), 0.1),
        'diff_lam_q2': nrm((nC, DIFF_HD), 0.1),
        'diff_lam_k2': nrm((nC, DIFF_HD), 0.1),
        'diff_subln': gain((nC, 2 * DIFF_HD)),
        'diff_w_o': nrm((nC, DIFF_HEADS * 2 * DIFF_HD, D), DN_BETA * (DIFF_HEADS * 2 * DIFF_HD) ** -0.5),
        'dil_w_in': nrm((nD, D, DIL_IN), D ** -0.5),
        'dil_w_o': nrm((nD, DIL_HEADS * DIL_HD, D), DN_BETA * (DIL_HEADS * DIL_HD) ** -0.5),
        'ln1_g': gain((DEPTH, D)),
        'ln1_b': nrm((DEPTH, D), 0.02),
        'ln2_g': gain((DEPTH, D)),
        'ln2_b': nrm((DEPTH, D), 0.02),
        'moe_wg': nrm((DEPTH, D, MOE_GROUPS), D ** -0.5),
        'moe_bg': nrm((DEPTH, MOE_GROUPS), 0.01),
        'moe_we': nrm((DEPTH, MOE_GROUPS, D, MOE_EXPERTS_PER_GROUP), D ** -0.5),
        'moe_be': nrm((DEPTH, MOE_GROUPS, MOE_EXPERTS_PER_GROUP), 0.01),
        'moe_w1': nrm((DEPTH, MOE_EXPERTS, D, MOE_FF), D ** -0.5),
        'moe_w3': nrm((DEPTH, MOE_EXPERTS, D, MOE_FF), D ** -0.5),
        'moe_w2': nrm((DEPTH, MOE_EXPERTS, MOE_FF, D), DN_BETA * MOE_FF ** -0.5),
    }


def reference(x, mla_w_in, mla_q_norm, mla_kv_norm, mla_w_qb, mla_w_kvb, mla_w_o,
              nsa_w_in, nsa_w_phi_k, nsa_w_phi_v, nsa_cmp_pos, nsa_w_o,
              diff_w_in, diff_lam_q1, diff_lam_k1, diff_lam_q2, diff_lam_k2, diff_subln, diff_w_o,
              dil_w_in, dil_w_o,
              ln1_g, ln1_b, ln2_g, ln2_b,
              moe_wg, moe_bg, moe_we, moe_be, moe_w1, moe_w3, moe_w2):
    for i in range(DEPTH):
        m, j = i % N_MIXERS, i // N_MIXERS
        if m == 0:
            y = mla_mixer(x, mla_w_in[j], mla_q_norm[j], mla_kv_norm[j], mla_w_qb[j], mla_w_kvb[j], mla_w_o[j])
        elif m == 1:
            y = nsa_mixer(x, nsa_w_in[j], nsa_w_phi_k[j], nsa_w_phi_v[j], nsa_cmp_pos[j], nsa_w_o[j])
        elif m == 2:
            y = diff_mixer(x, diff_w_in[j], diff_lam_q1[j], diff_lam_k1[j], diff_lam_q2[j], diff_lam_k2[j],
                           diff_subln[j], diff_w_o[j], i)
        else:
            y = dilated_mixer(x, dil_w_in[j], dil_w_o[j])
        x = layer_norm(DN_ALPHA * x + y, ln1_g[i], ln1_b[i])
        y = hier_moe(x, moe_wg[i], moe_bg[i], moe_we[i], moe_be[i], moe_w1[i], moe_w3[i], moe_w2[i])
        x = layer_norm(DN_ALPHA * x + y, ln2_g[i], ln2_b[i])
    return x
```

```python
import functools
import math

import numpy as np
import jax
import jax.numpy as jnp
from jax import lax
from jax.experimental import pallas as pl
from jax.experimental.pallas import tpu as pltpu

F32 = jnp.float32
BF16 = jnp.bfloat16

DEPTH = 4
DN_ALPHA = (2.0 * DEPTH) ** 0.25
LN_EPS = 1e-5
NEG_INF = -1e30
LANE = 128
VMEM_LIMIT = 56 * 1024 * 1024

MLA_HEADS, MLA_Q_RANK, MLA_KV_RANK, MLA_NOPE, MLA_ROPE, MLA_V = 16, 384, 256, 64, 32, 64
ROPE_THETA = 10000.0
NSA_HEADS, NSA_GROUPS, NSA_HD = 16, 4, 64
NSA_CMP_LEN, NSA_CMP_STRIDE, NSA_SLC_LEN, NSA_TOP_N, NSA_WINDOW, NSA_FORCE = 32, 16, 64, 8, 512, 1e4
DIFF_HEADS, DIFF_HD = 8, 64
DIL_PATTERNS = ((128, 1), (512, 4), (2048, 16))
DIL_HEADS, DIL_HD = 8, 64
MOE_GROUPS, MOE_EPG, MOE_EXPERTS, MOE_FF = 4, 4, 16, 512


def _cparams(*sem):
    return pltpu.CompilerParams(dimension_semantics=sem, vmem_limit_bytes=VMEM_LIMIT)


def _alibi(n):
    return np.asarray(2.0 ** (-8.0 * np.arange(1, n + 1) / n), np.float32)


def _linear_kernel(x_ref, w_ref, *out_refs, splits, chunk):
    xb = x_ref[...].astype(BF16)
    col = 0
    for o_ref, n in zip(out_refs, splits):
        for c0 in range(0, n, chunk):
            cw = min(chunk, n - c0)
            o_ref[:, c0:c0 + cw] = jnp.dot(
                xb, w_ref[:, col + c0:col + c0 + cw], preferred_element_type=F32).astype(o_ref.dtype)
        col += n


def _linear(x, w, splits, dtypes, tm=512, name="linear"):
    M, K = x.shape
    ntot = sum(splits)
    assert w.shape == (K, ntot) and M % tm == 0 and all(n % LANE == 0 for n in splits)
    outs = pl.pallas_call(
        functools.partial(_linear_kernel, splits=tuple(splits), chunk=512),
        out_shape=[jax.ShapeDtypeStruct((M, n), d) for n, d in zip(splits, dtypes)],
        grid=(M // tm,),
        in_specs=[pl.BlockSpec((tm, K), lambda i: (i, 0)),
                  pl.BlockSpec((K, ntot), lambda i: (0, 0))],
        out_specs=[pl.BlockSpec((tm, n), lambda i: (i, 0)) for n in splits],
        compiler_params=_cparams("parallel"),
        name=name,
    )(x, w)
    return outs


def _layer_norm_rows(z, g, b):
    mu = jnp.mean(z, axis=-1, keepdims=True)
    zc = z - mu
    var = jnp.mean(zc * zc, axis=-1, keepdims=True)
    return zc * lax.rsqrt(var + LN_EPS) * g + b


def _outproj_ln_kernel(o_ref, w_ref, x_ref, g_ref, b_ref, out_ref):
    y = jnp.dot(o_ref[...].astype(BF16), w_ref[...], preferred_element_type=F32)
    out_ref[...] = _layer_norm_rows(DN_ALPHA * x_ref[...] + y, g_ref[...], b_ref[...])


def _outproj_ln(o, w_o, x, g, b, tm=512, name="outproj_ln"):
    M, K = o.shape
    D = x.shape[1]
    return pl.pallas_call(
        _outproj_ln_kernel,
        out_shape=jax.ShapeDtypeStruct((M, D), F32),
        grid=(M // tm,),
        in_specs=[pl.BlockSpec((tm, K), lambda i: (i, 0)),
                  pl.BlockSpec((K, D), lambda i: (0, 0)),
                  pl.BlockSpec((tm, D), lambda i: (i, 0)),
                  pl.BlockSpec((1, D), lambda i: (0, 0)),
                  pl.BlockSpec((1, D), lambda i: (0, 0))],
        out_specs=pl.BlockSpec((tm, D), lambda i: (i, 0)),
        compiler_params=_cparams("parallel"),
        name=name,
    )(o, w_o.astype(BF16), x, g.reshape(1, D), b.reshape(1, D))


def _flash_loop(q, k_at, v_at, fix_scores, lo, hi, dv):
    M = q.shape[0]

    def body(c, carry):
        m, l, acc = carry
        s = lax.dot_general(q, k_at(c), (((1,), (1,)), ((), ())), preferred_element_type=F32)
        s = fix_scores(c, s)
        m_new = jnp.maximum(m, jnp.max(s, axis=-1, keepdims=True))
        alpha = jnp.exp(m - m_new)
        p = jnp.exp(s - m_new)
        l = alpha * l + jnp.sum(p, axis=-1, keepdims=True)
        acc = alpha * acc + jnp.dot(p.astype(BF16), v_at(c), preferred_element_type=F32)
        return m_new, l, acc

    init = (jnp.full((M, 1), NEG_INF, F32), jnp.zeros((M, 1), F32), jnp.zeros((M, dv), F32))
    return lax.fori_loop(lo, hi, body, init)


def _rel_pos(rows, cols):
    return (lax.broadcasted_iota(jnp.int32, (rows, cols), 1)
            - lax.broadcasted_iota(jnp.int32, (rows, cols), 0))


def _rms_rows(c, g, eps):
    return c * lax.rsqrt(jnp.mean(c * c, axis=-1, keepdims=True) + eps) * g


def _mla_proj_kernel(x_ref, win_ref, qn_ref, kvn_ref, wq_ref, wqs_ref, wk_ref, wv_ref, cos_ref, sin_ref,
                     q_out, k_out, v_out, *, scale):
    xb = x_ref[...].astype(BF16)
    c = jnp.dot(xb, win_ref[...], preferred_element_type=F32)
    r0, r1 = MLA_Q_RANK, MLA_Q_RANK + MLA_KV_RANK
    cq = _rms_rows(c[:, :r0], qn_ref[...], 1e-6).astype(BF16)
    ckv = _rms_rows(c[:, r0:r1], kvn_ref[...], 1e-6).astype(BF16)
    cos, sin = cos_ref[...], sin_ref[...]
    kr = c[:, r1:r1 + LANE] * cos + c[:, r1 + LANE:r1 + 2 * LANE] * sin
    v_out[...] = jnp.dot(ckv, wv_ref[...], preferred_element_type=F32).astype(v_out.dtype)
    for h in range(MLA_HEADS):
        sl = slice(h * LANE, (h + 1) * LANE)
        qh = jnp.dot(cq, wq_ref[:, sl], preferred_element_type=F32)
        qhs = jnp.dot(cq, wqs_ref[:, sl], preferred_element_type=F32)
        q_out[:, sl] = ((qh * cos + qhs * sin) * scale).astype(q_out.dtype)
        kh = jnp.dot(ckv, wk_ref[:, sl], preferred_element_type=F32)
        k_out[:, sl] = (kh + kr).astype(k_out.dtype)


def _mla_attn_kernel(q_ref, k_ref, v_ref, o_ref, *, tq, tk):
    i = pl.program_id(2)
    q0 = i * tq
    rel = _rel_pos(tq, tk)
    hi = (q0 + tq + tk - 1) // tk
    lane = lax.broadcasted_iota(jnp.int32, (tq, LANE), 1)
    out = None
    for hh in range(2):
        q = q_ref[0, :, hh * LANE:(hh + 1) * LANE]

        def k_at(c, hh=hh):
            return k_ref[0, pl.ds(pl.multiple_of(c * tk, tk), tk), hh * LANE:(hh + 1) * LANE]

        def v_at(c):
            return v_ref[0, pl.ds(pl.multiple_of(c * tk, tk), tk), :]

        def fix(c, s):
            return jnp.where(rel + (c * tk - q0) <= 0, s, NEG_INF)

        _, l, acc = _flash_loop(q, k_at, v_at, fix, 0, hi, LANE)
        o = acc / l
        out = o if out is None else jnp.where(lane < MLA_V, out, o)
    o_ref[0] = out.astype(o_ref.dtype)


def _mla_mixer(x, w_in, q_norm, kv_norm, w_qb, w_kvb, w_o, ln_g, ln_b, B, S):
    N, D = x.shape
    H, dq = MLA_HEADS, MLA_NOPE + MLA_ROPE
    half = MLA_ROPE // 2
    r0, r1 = MLA_Q_RANK, MLA_Q_RANK + MLA_KV_RANK
    z = lambda *s: jnp.zeros(s, F32)
    swap = lambda a: jnp.concatenate([a[..., half:], a[..., :half]], axis=-1)
    kr_w = w_in[:, r1:]
    win = jnp.concatenate([w_in[:, :r1],
                           z(D, MLA_NOPE), kr_w, z(D, LANE - dq),
                           z(D, MLA_NOPE), swap(kr_w), z(D, LANE - dq)], axis=1).astype(BF16)
    wq3 = w_qb.reshape(r0, H, dq)
    wq = jnp.concatenate([wq3, z(r0, H, LANE - dq)], axis=-1).reshape(r0, H * LANE).astype(BF16)
    wqs = jnp.concatenate([z(r0, H, MLA_NOPE), swap(wq3[..., MLA_NOPE:]), z(r0, H, LANE - dq)],
                          axis=-1).reshape(r0, H * LANE).astype(BF16)
    wkv3 = w_kvb.reshape(MLA_KV_RANK, H, MLA_NOPE + MLA_V)
    wk = jnp.concatenate([wkv3[..., :MLA_NOPE], z(MLA_KV_RANK, H, LANE - MLA_NOPE)],
                         axis=-1).reshape(MLA_KV_RANK, H * LANE).astype(BF16)
    wv = wkv3[..., MLA_NOPE:].reshape(MLA_KV_RANK, H * MLA_V).astype(BF16)
    freq = ROPE_THETA ** (-jnp.arange(half, dtype=F32) / half)
    ang = jnp.arange(S, dtype=F32)[:, None] * freq
    cos, sin = jnp.cos(ang), jnp.sin(ang)
    ones, zer = jnp.ones((S, MLA_NOPE), F32), jnp.zeros((S, LANE - dq), F32)
    cos_t = jnp.concatenate([ones, cos, cos, zer], axis=1)
    sin_t = jnp.concatenate([0 * ones, -sin, sin, zer], axis=1)

    tm = 256
    nwin = win.shape[1]
    q, k, v = pl.pallas_call(
        functools.partial(_mla_proj_kernel, scale=dq ** -0.5),
        out_shape=[jax.ShapeDtypeStruct((N, H * LANE), BF16),
                   jax.ShapeDtypeStruct((N, H * LANE), BF16),
                   jax.ShapeDtypeStruct((N, H * MLA_V), BF16)],
        grid=(N // tm,),
        in_specs=[pl.BlockSpec((tm, D), lambda i: (i, 0)),
                  pl.BlockSpec((D, nwin), lambda i: (0, 0)),
                  pl.BlockSpec((1, r0), lambda i: (0, 0)),
                  pl.BlockSpec((1, MLA_KV_RANK), lambda i: (0, 0)),
                  pl.BlockSpec((r0, H * LANE), lambda i: (0, 0)),
                  pl.BlockSpec((r0, H * LANE), lambda i: (0, 0)),
                  pl.BlockSpec((MLA_KV_RANK, H * LANE), lambda i: (0, 0)),
                  pl.BlockSpec((MLA_KV_RANK, H * MLA_V), lambda i: (0, 0)),
                  pl.BlockSpec((tm, LANE), lambda i: (i % (S // tm), 0)),
                  pl.BlockSpec((tm, LANE), lambda i: (i % (S // tm), 0))],
        out_specs=[pl.BlockSpec((tm, H * LANE), lambda i: (i, 0)),
                   pl.BlockSpec((tm, H * LANE), lambda i: (i, 0)),
                   pl.BlockSpec((tm, H * MLA_V), lambda i: (i, 0))],
        compiler_params=_cparams("parallel"),
        name="mla_proj",
    )(x, win, q_norm.reshape(1, r0), kv_norm.reshape(1, MLA_KV_RANK), wq, wqs, wk, wv, cos_t, sin_t)

    tq, tk = 256, 512
    o = pl.pallas_call(
        functools.partial(_mla_attn_kernel, tq=tq, tk=tk),
        out_shape=jax.ShapeDtypeStruct((B, S, H * MLA_V), BF16),
        grid=(B, H // 2, S // tq),
        in_specs=[pl.BlockSpec((1, tq, 2 * LANE), lambda b, h, i: (b, i, h)),
                  pl.BlockSpec((1, S, 2 * LANE), lambda b, h, i: (b, 0, h)),
                  pl.BlockSpec((1, S, LANE), lambda b, h, i: (b, 0, h))],
        out_specs=pl.BlockSpec((1, tq, LANE), lambda b, h, i: (b, i, h)),
        compiler_params=_cparams("parallel", "parallel", "arbitrary"),
        name="mla_attn",
    )(q.reshape(B, S, H * LANE), k.reshape(B, S, H * LANE), v.reshape(B, S, H * MLA_V))
    return _outproj_ln(o.reshape(N, H * MLA_V), w_o, x, ln_g, ln_b, name="mla_out_ln")


def _diff_attn_kernel(slope_ref, lam_ref, sub_ref, q_ref, k_ref, v_ref, o_ref, *, tq, tk, lam_init):
    h = pl.program_id(1)
    i = pl.program_id(2)
    q0 = i * tq
    slope = slope_ref[h]
    lv = lam_ref[...]
    lam = (jnp.exp(jnp.sum(lv[0:1] * lv[1:2], axis=-1, keepdims=True))
           - jnp.exp(jnp.sum(lv[2:3] * lv[3:4], axis=-1, keepdims=True)) + lam_init)
    lane = lax.broadcasted_iota(jnp.int32, (tq, LANE), 1)
    qf = q_ref[0].astype(F32) * (DIFF_HD ** -0.5)
    q2 = jnp.concatenate([jnp.where(lane < DIFF_HD, qf, 0.0), jnp.where(lane < DIFF_HD, 0.0, qf)],
                         axis=0).astype(BF16)
    rel = _rel_pos(tq, tk)
    rel2 = jnp.concatenate([rel, rel], axis=0)
    hi = (q0 + tq + tk - 1) // tk

    def k_at(c):
        return k_ref[0, pl.ds(pl.multiple_of(c * tk, tk), tk), :]

    def v_at(c):
        return v_ref[0, pl.ds(pl.multiple_of(c * tk, tk), tk), :]

    def fix(c, s):
        d = rel2 + (c * tk - q0)
        return jnp.where(d <= 0, s + slope * d.astype(F32), NEG_INF)

    _, l, acc = _flash_loop(q2, k_at, v_at, fix, 0, hi, LANE)
    o = acc / l
    a = o[:tq] - lam * o[tq:]
    a = _rms_rows(a, sub_ref[...], 1e-5) * (1.0 - lam_init)
    o_ref[0] = a.astype(o_ref.dtype)


def _diff_mixer(x, w_in, lam_q1, lam_k1, lam_q2, lam_k2, subln, w_o, layer_idx, ln_g, ln_b, B, S):
    N, D = x.shape
    H, d = DIFF_HEADS, DIFF_HD
    nq = H * 2 * d
    (qkv,) = _linear(x, w_in.astype(BF16), [3 * nq], [BF16], name="diff_proj")
    qkv = qkv.reshape(B, S, 3 * nq)
    lam_init = 0.8 - 0.6 * math.exp(-0.3 * layer_idx)
    lamv = jnp.zeros((8, LANE), F32).at[:4, :d].set(jnp.stack([lam_q1, lam_k1, lam_q2, lam_k2]).astype(F32))
    tq, tk = 256, 512
    o = pl.pallas_call(
        functools.partial(_diff_attn_kernel, tq=tq, tk=tk, lam_init=lam_init),
        out_shape=jax.ShapeDtypeStruct((B, S, nq), BF16),
        grid=(B, H, S // tq),
        in_specs=[pl.BlockSpec(memory_space=pltpu.SMEM),
                  pl.BlockSpec((8, LANE), lambda b, h, i: (0, 0)),
                  pl.BlockSpec((1, 2 * d), lambda b, h, i: (0, 0)),
                  pl.BlockSpec((1, tq, LANE), lambda b, h, i: (b, i, h)),
                  pl.BlockSpec((1, S, LANE), lambda b, h, i: (b, 0, H + h)),
                  pl.BlockSpec((1, S, LANE), lambda b, h, i: (b, 0, 2 * H + h))],
        out_specs=pl.BlockSpec((1, tq, LANE), lambda b, h, i: (b, i, h)),
        compiler_params=_cparams("parallel", "parallel", "arbitrary"),
        name="diff_attn",
    )(jnp.asarray(_alibi(H)), lamv, subln.reshape(1, 2 * d).astype(F32), qkv, qkv, qkv)
    return _outproj_ln(o.reshape(N, nq), w_o, x, ln_g, ln_b, name="diff_out_ln")


SEL_LANE0 = NSA_HD
SEL_OFF = 1e30


def _nsa_compress_kernel(ak_ref, av_ref, plo_ref, phi_ref, wklo_ref, wkhi_ref, wvlo_ref, wvhi_ref, kc_out, vc_out):
    def one(a_ref, wlo_ref, whi_ref, out):
        a = a_ref[0]
        lo = jnp.dot((a + plo_ref[...]).astype(BF16), wlo_ref[...], preferred_element_type=F32)
        hi = jnp.dot((a + phi_ref[...]).astype(BF16), whi_ref[...], preferred_element_type=F32)
        out[0] = (lo + pltpu.roll(hi, hi.shape[0] - 1, 0)).astype(out.dtype)

    one(ak_ref, wklo_ref, wkhi_ref, kc_out)
    one(av_ref, wvlo_ref, wvhi_ref, vc_out)


def _pack_heads(o, tq):
    lane = lax.broadcasted_iota(jnp.int32, (tq, LANE), 1)
    p01 = jnp.where(lane < NSA_HD, o[0:tq], o[tq:2 * tq])
    p23 = jnp.where(lane < NSA_HD, o[2 * tq:3 * tq], o[3 * tq:4 * tq])
    return jnp.concatenate([p01, p23], axis=1)


def _nsa_attn_kernel(slope_ref, q_ref, gl_ref, kc_ref, vc_ref, ks_ref, vs_ref, kw_ref, vw_ref, oh_ref, c2s_ref,
                     rep_ref, o_ref, *, tq, tks, tkw, n_slc):
    g = pl.program_id(1)
    i = pl.program_id(2)
    t0 = i * tq
    hpg = NSA_HEADS // NSA_GROUPS
    M = hpg * tq
    q = jnp.concatenate([q_ref[0, :, j * LANE:(j + 1) * LANE] for j in range(hpg)], axis=0)
    slope = jnp.concatenate([jnp.full((tq, 1), slope_ref[g * hpg + j], F32) for j in range(hpg)], axis=0)
    trow1 = t0 + lax.broadcasted_iota(jnp.int32, (tq, 1), 0)
    trow = jnp.concatenate([trow1] * hpg, axis=0)
    lane_m = lax.broadcasted_iota(jnp.int32, (M, LANE), 1)

    sc = lax.dot_general(q, kc_ref[0], (((1,), (1,)), ((), ())), preferred_element_type=F32)
    dist_c = trow - (lane_m * NSA_CMP_STRIDE + (NSA_CMP_LEN - 1))
    sc = jnp.where(dist_c >= 0, sc - slope * dist_c.astype(F32), NEG_INF)
    e = jnp.exp(sc - jnp.max(sc, axis=-1, keepdims=True))
    p_c = jnp.where(trow >= NSA_CMP_LEN - 1, e / jnp.sum(e, axis=-1, keepdims=True), 0.0)
    o_c = jnp.dot(p_c.astype(BF16), vc_ref[0], preferred_element_type=F32)
    psum = p_c[0:tq]
    for j in range(1, hpg):
        psum = psum + p_c[j * tq:(j + 1) * tq]
    ph, plw = _split_bf16(psum)
    imp = (jnp.dot(ph, c2s_ref[...], preferred_element_type=F32)
           + jnp.dot(plw, c2s_ref[...], preferred_element_type=F32))

    lane = lax.broadcasted_iota(jnp.int32, (tq, LANE), 1)
    sidx = lane - SEL_LANE0
    valid = (sidx >= 0) & (sidx < n_slc)
    forced = (sidx == 0) | (sidx == trow1 // NSA_SLC_LEN)
    future = sidx * NSA_SLC_LEN > trow1
    score = imp + jnp.where(forced, NSA_FORCE, 0.0) - jnp.where(future, 2.0 * NSA_FORCE, 0.0)
    score = jnp.where(valid, score, -jnp.inf)
    sel = jnp.zeros((tq, LANE), jnp.bool_)
    for _ in range(min(NSA_TOP_N, n_slc)):
        mx = jnp.max(score, axis=-1, keepdims=True)
        pick = lane == jnp.min(jnp.where(score == mx, lane, LANE), axis=-1, keepdims=True)
        sel = sel | pick
        score = jnp.where(pick, -jnp.inf, score)
    selbias = jnp.where(valid & jnp.logical_not(sel), -SEL_OFF, 0.0).astype(BF16)

    q_s = q + jnp.concatenate([selbias] * hpg, axis=0)
    rel_s = jnp.concatenate([_rel_pos(tq, tks)] * hpg, axis=0)

    def ks_at(c):
        sl = pl.ds(pl.multiple_of(c * tks, tks), tks)
        return ks_ref[0, sl, :] + oh_ref[sl, :]

    def vs_at(c):
        return vs_ref[0, pl.ds(pl.multiple_of(c * tks, tks), tks), :]

    def fix_s(c, s):
        d = rel_s + (c * tks - t0)
        return jnp.where(d <= 0, s + slope * d.astype(F32), NEG_INF)

    _, l_s, acc_s = _flash_loop(q_s, ks_at, vs_at, fix_s, 0, (t0 + tq + tks - 1) // tks, LANE)
    o_s = acc_s / l_s

    rel_w = jnp.concatenate([_rel_pos(tq, tkw)] * hpg, axis=0)

    def kw_at(c):
        return kw_ref[0, pl.ds(pl.multiple_of(c * tkw, tkw), tkw), :]

    def vw_at(c):
        return vw_ref[0, pl.ds(pl.multiple_of(c * tkw, tkw), tkw), :]

    def fix_w(c, s):
        d = rel_w + (c * tkw - t0)
        return jnp.where((d <= 0) & (d > -NSA_WINDOW), s + slope * d.astype(F32), NEG_INF)

    lo_w = jnp.maximum(t0 - (NSA_WINDOW - 1), 0) // tkw
    _, l_w, acc_w = _flash_loop(q, kw_at, vw_at, fix_w, lo_w, (t0 + tq + tkw - 1) // tkw, LANE)
    o_w = acc_w / l_w

    gh, glw = _split_bf16(jax.nn.sigmoid(gl_ref[0]))
    gr = jnp.dot(gh, rep_ref[0], preferred_element_type=F32) + jnp.dot(glw, rep_ref[0], preferred_element_type=F32)
    w = hpg * NSA_HD
    o = (gr[:, 0:w] * _pack_heads(o_c, tq) + gr[:, w:2 * w] * _pack_heads(o_s, tq)
         + gr[:, 2 * w:3 * w] * _pack_heads(o_w, tq))
    o_ref[0] = o.astype(o_ref.dtype)


def _nsa_mixer(x, w_in, w_phi_k, w_phi_v, cmp_pos, w_o, ln_g, ln_b, B, S):
    N, D = x.shape
    H, G, d = NSA_HEADS, NSA_GROUPS, NSA_HD
    hpg = H // G
    L, st = NSA_CMP_LEN, NSA_CMP_STRIDE
    n_slc = S // NSA_SLC_LEN
    assert S % 256 == 0 and n_slc <= LANE - SEL_LANE0 and L == 2 * st
    cuts = [H * d + i * G * d for i in range(7)]
    wq, wkc, wvc, wks, wvs, wkw, wvw, wgl = jnp.split(w_in, cuts, axis=1)
    z = lambda *s: jnp.zeros(s, F32)
    pad_heads = lambda w, n: jnp.concatenate([w.reshape(D, n, d), z(D, n, LANE - d)], axis=-1).reshape(D, n * LANE)
    dup_heads = lambda w, n: jnp.concatenate([w.reshape(D, n, d)] * 2, axis=-1).reshape(D, n * LANE)
    wgl_p = jnp.concatenate([wgl, z(D, LANE - wgl.shape[1])], axis=1)
    wcat = jnp.concatenate([pad_heads(wq * d ** -0.5, H), wkc, wvc, pad_heads(wks, G), dup_heads(wvs, G),
                            pad_heads(wkw, G), dup_heads(wvw, G), wgl_p], axis=1).astype(BF16)
    q, kc, vc, ks, vs, kw, vw, gl = _linear(
        x, wcat, [H * LANE, G * d, G * d, G * LANE, G * LANE, G * LANE, G * LANE, LANE],
        [BF16, F32, F32, BF16, BF16, BF16, BF16, F32], name="nsa_proj")

    nrow = S // st
    eye = jnp.eye(G, dtype=F32)

    def phi_w(w_phi, half, dup):
        w = w_phi.reshape(L, d, d)[half * st:(half + 1) * st]
        wd = jnp.concatenate([w, w if dup else jnp.zeros_like(w)], axis=-1)
        return jnp.einsum('ldc,gh->lgdhc', wd, eye).reshape(st * G * d, G * LANE).astype(BF16)

    pos = lambda half: jnp.broadcast_to(cmp_pos[half * st:(half + 1) * st, None, :], (st, G, d)).reshape(1, st * G * d)
    wide = st * G * d
    cspec = pl.BlockSpec((1, nrow, wide), lambda b: (b, 0, 0))
    wspec = pl.BlockSpec((wide, G * LANE), lambda b: (0, 0))
    pspec = pl.BlockSpec((1, wide), lambda b: (0, 0))
    ospec = pl.BlockSpec((1, nrow, G * LANE), lambda b: (b, 0, 0))
    kcmp, vcmp = pl.pallas_call(
        _nsa_compress_kernel,
        out_shape=[jax.ShapeDtypeStruct((B, nrow, G * LANE), BF16)] * 2,
        grid=(B,),
        in_specs=[cspec, cspec, pspec, pspec, wspec, wspec, wspec, wspec],
        out_specs=[ospec, ospec],
        compiler_params=_cparams("parallel"),
        name="nsa_compress",
    )(kc.reshape(B, nrow, wide), vc.reshape(B, nrow, wide), pos(0), pos(1),
      phi_w(w_phi_k, 0, False), phi_w(w_phi_k, 1, False), phi_w(w_phi_v, 0, True), phi_w(w_phi_v, 1, True))

    oh = np.zeros((S, LANE), np.float32)
    oh[np.arange(S), SEL_LANE0 + np.arange(S) // NSA_SLC_LEN] = 1.0
    cmp_start = np.arange(nrow) * st
    slc_start = np.arange(n_slc) * NSA_SLC_LEN
    ov = (cmp_start[:, None] < slc_start[None, :] + NSA_SLC_LEN) & (cmp_start[:, None] + L > slc_start[None, :])
    ov[(S - L) // st + 1:] = False
    c2s = np.zeros((nrow, LANE), np.float32)
    c2s[:, SEL_LANE0:SEL_LANE0 + n_slc] = ov
    rep = np.zeros((G, LANE, 3 * hpg * d), np.float32)
    for g in range(G):
        for j in range(hpg):
            for br in range(3):
                rep[g, (g * hpg + j) * 3 + br, br * hpg * d + j * d:br * hpg * d + (j + 1) * d] = 1.0

    tq, tks, tkw = 128, 512, 256
    kvspec = pl.BlockSpec((1, S, LANE), lambda b, g, i: (b, 0, g))
    cmpspec = pl.BlockSpec((1, nrow, LANE), lambda b, g, i: (b, 0, g))
    o = pl.pallas_call(
        functools.partial(_nsa_attn_kernel, tq=tq, tks=tks, tkw=tkw, n_slc=n_slc),
        out_shape=jax.ShapeDtypeStruct((B, S, H * d), BF16),
        grid=(B, G, S // tq),
        in_specs=[pl.BlockSpec(memory_space=pltpu.SMEM),
                  pl.BlockSpec((1, tq, hpg * LANE), lambda b, g, i: (b, i, g)),
                  pl.BlockSpec((1, tq, LANE), lambda b, g, i: (b, i, 0)),
                  cmpspec, cmpspec, kvspec, kvspec, kvspec, kvspec,
                  pl.BlockSpec((S, LANE), lambda b, g, i: (0, 0)),
                  pl.BlockSpec((nrow, LANE), lambda b, g, i: (0, 0)),
                  pl.BlockSpec((1, LANE, 3 * hpg * d), lambda b, g, i: (g, 0, 0))],
        out_specs=pl.BlockSpec((1, tq, hpg * d), lambda b, g, i: (b, i, g)),
        compiler_params=_cparams("parallel", "parallel", "arbitrary"),
        name="nsa_attn",
    )(jnp.asarray(_alibi(H)), q.reshape(B, S, H * LANE), gl.reshape(B, S, LANE), kcmp, vcmp,
      ks.reshape(B, S, G * LANE), vs.reshape(B, S, G * LANE), kw.reshape(B, S, G * LANE), vw.reshape(B, S, G * LANE),
      jnp.asarray(oh, BF16), jnp.asarray(c2s, BF16), jnp.asarray(rep, BF16))
    return _outproj_ln(o.reshape(N, H * d), w_o, x, ln_g, ln_b, name="nsa_out_ln")


def _dil_attn_kernel(slope_ref, q_ref, k_ref, v_ref, o_ref, lse_ref, *, tq, dil, wsub):
    i = pl.program_id(2)
    d = DIL_HD
    rel = _rel_pos(tq, tq)
    rel2 = jnp.concatenate([rel, rel], axis=0)
    lane = lax.broadcasted_iota(jnp.int32, (tq, LANE), 1)
    lse_t = jnp.zeros((tq, LANE), F32)
    for hp in range(DIL_HEADS // 2):
        sl = slice(hp * LANE, (hp + 1) * LANE)
        qf = q_ref[0, :, sl].astype(F32) * (d ** -0.5)
        q2 = jnp.concatenate([jnp.where(lane < d, qf, 0.0), jnp.where(lane < d, 0.0, qf)], axis=0).astype(BF16)
        slope = jnp.concatenate([jnp.full((tq, 1), slope_ref[2 * hp] * dil, F32),
                                 jnp.full((tq, 1), slope_ref[2 * hp + 1] * dil, F32)], axis=0)

        def k_at(c, sl=sl):
            return k_ref[0, pl.ds(pl.multiple_of(c * tq, tq), tq), sl]

        def v_at(c, sl=sl):
            return v_ref[0, pl.ds(pl.multiple_of(c * tq, tq), tq), sl]

        def fix(c, s, slope=slope):
            dd = rel2 + (c - i) * tq
            return jnp.where((dd <= 0) & (dd >= -wsub), s + slope * dd.astype(F32), NEG_INF)

        m, l, acc = _flash_loop(q2, k_at, v_at, fix, jnp.maximum(i - (wsub + tq - 1) // tq, 0), i + 1, LANE)
        o = acc / l
        o_ref[0, :, sl] = jnp.where(lane < d, o[:tq], o[tq:]).astype(o_ref.dtype)
        lse = m + jnp.log(l)
        lse_t = jnp.where(lane == 2 * hp, lse[:tq], lse_t)
        lse_t = jnp.where(lane == 2 * hp + 1, lse[tq:], lse_t)
    lse_ref[0] = lse_t


def _dil_merge_ln_kernel(o0_ref, o1_ref, o2_ref, l0_ref, l1_ref, l2_ref, rep_ref, w_ref, x_ref, g_ref, b_ref, out_ref):
    ls = [l0_ref[...], l1_ref[...], l2_ref[...]]
    mx = jnp.maximum(jnp.maximum(ls[0], ls[1]), ls[2])
    es = [jnp.exp(v - mx) for v in ls]
    tot = es[0] + es[1] + es[2]
    o = None
    for e, o_ref in zip(es, (o0_ref, o1_ref, o2_ref)):
        wh, wl = _split_bf16(e / tot)
        wrep = jnp.dot(wh, rep_ref[...], preferred_element_type=F32) + jnp.dot(wl, rep_ref[...], preferred_element_type=F32)
        term = wrep * o_ref[...].astype(F32)
        o = term if o is None else o + term
    y = jnp.dot(o.astype(BF16), w_ref[...], preferred_element_type=F32)
    out_ref[...] = _layer_norm_rows(DN_ALPHA * x_ref[...] + y, g_ref[...], b_ref[...])


def _dil_mixer(x, w_in, w_o, ln_g, ln_b, B, S):
    N, D = x.shape
    H, d = DIL_HEADS, DIL_HD
    n_pat = len(DIL_PATTERNS)
    hd = H * d
    ncol = n_pat * 3 * hd
    (qkv,) = _linear(x, w_in.astype(BF16), [ncol], [BF16], name="dil_proj")
    tq = 128
    slopes = jnp.asarray(_alibi(H))
    outs, lses = [], []
    for p, (win, dil) in enumerate(DIL_PATTERNS):
        ls = S // dil
        assert ls % tq == 0 and win % dil == 0
        nblk = ncol // hd
        view = qkv.reshape(B, ls, dil * ncol)
        o, lse = pl.pallas_call(
            functools.partial(_dil_attn_kernel, tq=tq, dil=dil, wsub=win // dil),
            out_shape=[jax.ShapeDtypeStruct((B, ls, dil * hd), BF16),
                       jax.ShapeDtypeStruct((B, ls, dil * LANE), F32)],
            grid=(B, dil, ls // tq),
            in_specs=[pl.BlockSpec(memory_space=pltpu.SMEM),
                      pl.BlockSpec((1, tq, hd), lambda b, r, i, p=p, nblk=nblk: (b, i, r * nblk + 3 * p)),
                      pl.BlockSpec((1, ls, hd), lambda b, r, i, p=p, nblk=nblk: (b, 0, r * nblk + 3 * p + 1)),
                      pl.BlockSpec((1, ls, hd), lambda b, r, i, p=p, nblk=nblk: (b, 0, r * nblk + 3 * p + 2))],
            out_specs=[pl.BlockSpec((1, tq, hd), lambda b, r, i: (b, i, r)),
                       pl.BlockSpec((1, tq, LANE), lambda b, r, i: (b, i, r))],
            compiler_params=_cparams("parallel", "parallel", "arbitrary"),
            name=f"dil_attn_{p}",
        )(slopes, view, view, view)
        outs.append(o.reshape(N, hd))
        lses.append(lse.reshape(N, LANE))

    rep = np.zeros((LANE, hd), np.float32)
    for h in range(H):
        rep[h, h * d:(h + 1) * d] = 1.0
    tm = 512
    row = lambda n: pl.BlockSpec((tm, n), lambda i: (i, 0))
    full = lambda a, b: pl.BlockSpec((a, b), lambda i: (0, 0))
    return pl.pallas_call(
        _dil_merge_ln_kernel,
        out_shape=jax.ShapeDtypeStruct((N, D), F32),
        grid=(N // tm,),
        in_specs=[row(hd), row(hd), row(hd), row(LANE), row(LANE), row(LANE), full(LANE, hd), full(hd, D),
                  row(D), full(1, D), full(1, D)],
        out_specs=row(D),
        compiler_params=_cparams("parallel"),
        name="dil_merge_out_ln",
    )(*outs, *lses, jnp.asarray(rep, BF16), w_o.astype(BF16), x, ln_g.reshape(1, D), ln_b.reshape(1, D))


def _split_bf16(a):
    hi = a.astype(BF16)
    return hi, (a - hi.astype(F32)).astype(BF16)


def _router_kernel(x_ref, wh_ref, wl_ref, b_ref, gate_ref):
    xh, xl = _split_bf16(x_ref[...])
    wh, wl = wh_ref[...], wl_ref[...]
    logits = (jnp.dot(xh, wh, preferred_element_type=F32) + jnp.dot(xl, wh, preferred_element_type=F32)
              + jnp.dot(xh, wl, preferred_element_type=F32)) + b_ref[...]
    tm = logits.shape[0]
    lane = lax.broadcasted_iota(jnp.int32, (tm, LANE), 1)
    big = jnp.int32(LANE)
    lg = jnp.where(lane < MOE_GROUPS, logits, -jnp.inf)
    mg = jnp.max(lg, axis=-1, keepdims=True)
    sg = jnp.sum(jnp.exp(lg - mg), axis=-1, keepdims=True)
    pg_top = 1.0 / sg
    g_top = jnp.min(jnp.where(lg == mg, lane, big), axis=-1, keepdims=True)
    e_lo = MOE_GROUPS + MOE_EPG * g_top
    in_grp = (lane >= e_lo) & (lane < e_lo + MOE_EPG)
    le = jnp.where(in_grp, logits, -jnp.inf)
    me = jnp.max(le, axis=-1, keepdims=True)
    ee = jnp.exp(le - me)
    se = jnp.sum(ee, axis=-1, keepdims=True)
    pe = jnp.where(in_grp, ee / se, -1.0)
    v1 = jnp.max(pe, axis=-1, keepdims=True)
    i1 = jnp.min(jnp.where(pe == v1, lane, big), axis=-1, keepdims=True)
    pe2 = jnp.where(lane == i1, -1.0, pe)
    v2 = jnp.max(pe2, axis=-1, keepdims=True)
    i2 = jnp.min(jnp.where(pe2 == v2, lane, big), axis=-1, keepdims=True)
    tot = v1 + v2
    gate_ref[...] = (jnp.where(lane == i1, (v1 / tot) * pg_top, 0.0)
                     + jnp.where(lane == i2, (v2 / tot) * pg_top, 0.0))


def _moe_dense_kernel(x_ref, gate_ref, w1_ref, w3_ref, w2_ref, g_ref, b_ref, out_ref, acc_ref):
    e = pl.program_id(1)

    @pl.when(e == 0)
    def _():
        acc_ref[...] = jnp.zeros_like(acc_ref)

    xb = x_ref[...].astype(BF16)
    h1 = jnp.dot(xb, w1_ref[0], preferred_element_type=F32)
    h3 = jnp.dot(xb, w3_ref[0], preferred_element_type=F32)
    hid = (h1 * jax.nn.sigmoid(h1) * h3).astype(BF16)
    y = jnp.dot(hid, w2_ref[0], preferred_element_type=F32)
    gates = gate_ref[...]
    lane = lax.broadcasted_iota(jnp.int32, gates.shape, 1)
    ge = jnp.sum(jnp.where(lane == MOE_GROUPS + e, gates, 0.0), axis=-1, keepdims=True)
    acc_ref[...] += ge * y

    @pl.when(e == MOE_EXPERTS - 1)
    def _():
        out_ref[...] = _layer_norm_rows(DN_ALPHA * x_ref[...] + acc_ref[...], g_ref[...], b_ref[...])


def _hier_moe_ln(x, wg, bg, we, be, w1, w3, w2, ln_g, ln_b):
    N, D = x.shape
    wr = jnp.concatenate([wg, jnp.moveaxis(we, 0, 1).reshape(D, MOE_EXPERTS),
                          jnp.zeros((D, LANE - MOE_GROUPS - MOE_EXPERTS), F32)], axis=1)
    br = jnp.concatenate([bg, be.reshape(-1), jnp.zeros((LANE - MOE_GROUPS - MOE_EXPERTS,), F32)]).reshape(1, LANE)
    wrh, wrl = _split_bf16(wr)
    tm = 512
    gates = pl.pallas_call(
        _router_kernel,
        out_shape=jax.ShapeDtypeStruct((N, LANE), F32),
        grid=(N // tm,),
        in_specs=[pl.BlockSpec((tm, D), lambda i: (i, 0)),
                  pl.BlockSpec((D, LANE), lambda i: (0, 0)),
                  pl.BlockSpec((D, LANE), lambda i: (0, 0)),
                  pl.BlockSpec((1, LANE), lambda i: (0, 0))],
        out_specs=pl.BlockSpec((tm, LANE), lambda i: (i, 0)),
        compiler_params=_cparams("parallel"),
        name="moe_router",
    )(x, wrh, wrl, br)

    tm = 1024
    return pl.pallas_call(
        _moe_dense_kernel,
        out_shape=jax.ShapeDtypeStruct((N, D), F32),
        grid=(N // tm, MOE_EXPERTS),
        in_specs=[pl.BlockSpec((tm, D), lambda i, e: (i, 0)),
                  pl.BlockSpec((tm, LANE), lambda i, e: (i, 0)),
                  pl.BlockSpec((1, D, MOE_FF), lambda i, e: (e, 0, 0)),
                  pl.BlockSpec((1, D, MOE_FF), lambda i, e: (e, 0, 0)),
                  pl.BlockSpec((1, MOE_FF, D), lambda i, e: (e, 0, 0)),
                  pl.BlockSpec((1, D), lambda i, e: (0, 0)),
                  pl.BlockSpec((1, D), lambda i, e: (0, 0))],
        out_specs=pl.BlockSpec((tm, D), lambda i, e: (i, 0)),
        scratch_shapes=[pltpu.VMEM((tm, D), F32)],
        compiler_params=_cparams("parallel", "arbitrary"),
        name="moe_experts_ln",
    )(x, gates, w1.astype(BF16), w3.astype(BF16), w2.astype(BF16), ln_g.reshape(1, D), ln_b.reshape(1, D))


def kernel(x, mla_w_in, mla_q_norm, mla_kv_norm, mla_w_qb, mla_w_kvb, mla_w_o, nsa_w_in, nsa_w_phi_k, nsa_w_phi_v, nsa_cmp_pos, nsa_w_o, diff_w_in, diff_lam_q1, diff_lam_k1, diff_lam_q2, diff_lam_k2, diff_subln, diff_w_o, dil_w_in, dil_w_o, ln1_g, ln1_b, ln2_g, ln2_b, moe_wg, moe_bg, moe_we, moe_be, moe_w1, moe_w3, moe_w2):
    B, S, D = x.shape
    h = x.reshape(B * S, D)
    for i in range(DEPTH):
        m, j = i % 4, i // 4
        if m == 0:
            h = _mla_mixer(h, mla_w_in[j], mla_q_norm[j], mla_kv_norm[j], mla_w_qb[j], mla_w_kvb[j], mla_w_o[j],
                           ln1_g[i], ln1_b[i], B, S)
        elif m == 1:
            h = _nsa_mixer(h, nsa_w_in[j], nsa_w_phi_k[j], nsa_w_phi_v[j], nsa_cmp_pos[j], nsa_w_o[j],
                           ln1_g[i], ln1_b[i], B, S)
        elif m == 2:
            h = _diff_mixer(h, diff_w_in[j], diff_lam_q1[j], diff_lam_k1[j], diff_lam_q2[j], diff_lam_k2[j],
                            diff_subln[j], diff_w_o[j], i, ln1_g[i], ln1_b[i], B, S)
        else:
            h = _dil_mixer(h, dil_w_in[j], dil_w_o[j], ln1_g[i], ln1_b[i], B, S)
        h = _hier_moe_ln(h, moe_wg[i], moe_bg[i], moe_we[i], moe_be[i], moe_w1[i], moe_w3[i], moe_w2[i],
                         ln2_g[i], ln2_b[i])
    return h.reshape(B, S, D)
```

```python
import functools
import math

import numpy as np
import jax
import jax.numpy as jnp
from jax import lax
from jax.experimental import pallas as pl
from jax.experimental.pallas import tpu as pltpu

F32 = jnp.float32
BF16 = jnp.bfloat16

DEPTH = 4
DN_ALPHA = (2.0 * DEPTH) ** 0.25
LN_EPS = 1e-5
NEG_INF = -1e30
LOG2E = math.log2(math.e)
LANE = 128
VMEM_LIMIT = 56 * 1024 * 1024
ATTN_TQ, ATTN_TK = 256, 512

MLA_HEADS, MLA_Q_RANK, MLA_KV_RANK, MLA_NOPE, MLA_ROPE, MLA_V = 16, 384, 256, 64, 32, 64
ROPE_THETA = 10000.0
NSA_HEADS, NSA_GROUPS, NSA_HD = 16, 4, 64
NSA_CMP_LEN, NSA_CMP_STRIDE, NSA_SLC_LEN, NSA_TOP_N, NSA_WINDOW, NSA_FORCE = 32, 16, 64, 8, 512, 1e4
DIFF_HEADS, DIFF_HD = 8, 64
DIL_PATTERNS = ((128, 1), (512, 4), (2048, 16))
DIL_HEADS, DIL_HD = 8, 64
MOE_GROUPS, MOE_EPG, MOE_EXPERTS, MOE_FF = 4, 4, 16, 512


def _cparams(*sem):
    return pltpu.CompilerParams(dimension_semantics=sem, vmem_limit_bytes=VMEM_LIMIT)


def _alibi(n):
    return np.asarray(2.0 ** (-8.0 * np.arange(1, n + 1) / n), np.float32)


def _linear_kernel(x_ref, w_ref, *out_refs, splits, chunk):
    xb = x_ref[...].astype(BF16)
    col = 0
    for o_ref, n in zip(out_refs, splits):
        for c0 in range(0, n, chunk):
            cw = min(chunk, n - c0)
            o_ref[:, c0:c0 + cw] = jnp.dot(
                xb, w_ref[:, col + c0:col + c0 + cw], preferred_element_type=F32).astype(o_ref.dtype)
        col += n


def _linear(x, w, splits, dtypes, tm=512, name="linear"):
    M, K = x.shape
    ntot = sum(splits)
    assert w.shape == (K, ntot) and M % tm == 0 and all(n % LANE == 0 for n in splits)
    outs = pl.pallas_call(
        functools.partial(_linear_kernel, splits=tuple(splits), chunk=512),
        out_shape=[jax.ShapeDtypeStruct((M, n), d) for n, d in zip(splits, dtypes)],
        grid=(M // tm,),
        in_specs=[pl.BlockSpec((tm, K), lambda i: (i, 0)),
                  pl.BlockSpec((K, ntot), lambda i: (0, 0))],
        out_specs=[pl.BlockSpec((tm, n), lambda i: (i, 0)) for n in splits],
        compiler_params=_cparams("parallel"),
        name=name,
    )(x, w)
    return outs


def _layer_norm_rows(z, g, b):
    mu = jnp.mean(z, axis=-1, keepdims=True)
    zc = z - mu
    var = jnp.mean(zc * zc, axis=-1, keepdims=True)
    return zc * lax.rsqrt(var + LN_EPS) * g + b


def _outproj_ln_kernel(o_ref, w_ref, x_ref, g_ref, b_ref, out_ref):
    y = jnp.dot(o_ref[...].astype(BF16), w_ref[...], preferred_element_type=F32)
    out_ref[...] = _layer_norm_rows(DN_ALPHA * x_ref[...] + y, g_ref[...], b_ref[...])


def _outproj_ln(o, w_o, x, g, b, tm=512, name="outproj_ln"):
    M, K = o.shape
    D = x.shape[1]
    return pl.pallas_call(
        _outproj_ln_kernel,
        out_shape=jax.ShapeDtypeStruct((M, D), F32),
        grid=(M // tm,),
        in_specs=[pl.BlockSpec((tm, K), lambda i: (i, 0)),
                  pl.BlockSpec((K, D), lambda i: (0, 0)),
                  pl.BlockSpec((tm, D), lambda i: (i, 0)),
                  pl.BlockSpec((1, D), lambda i: (0, 0)),
                  pl.BlockSpec((1, D), lambda i: (0, 0))],
        out_specs=pl.BlockSpec((tm, D), lambda i: (i, 0)),
        compiler_params=_cparams("parallel"),
        name=name,
    )(o, w_o.astype(BF16), x, g.reshape(1, D), b.reshape(1, D))


def _scores(q, k, bias):
    s = lax.dot_general(q, k, (((1,), (1,)), ((), ())), preferred_element_type=F32)
    return s if bias is None else s + bias


def _flash_scratch(n_streams, M, t):
    return [pltpu.VMEM((n_streams, 2, M, t), F32), pltpu.VMEM((n_streams, 2, M, t), BF16),
            pltpu.VMEM((n_streams, M, LANE), F32), pltpu.VMEM((n_streams, 3, M, LANE), F32)]


def _flash_causal(streams, n_int, diag_vis, scratch):
    s_ref, p_ref, acc_ref, st_ref = scratch
    reps = s_ref.shape[-1] // LANE

    def scores(n, c):
        q, k_at, _, bias_at = streams[n]
        return _scores(q, k_at(c), None if bias_at is None else bias_at(c))

    def lane_partial_sum(p):
        return functools.reduce(jnp.add, [p[:, r * LANE:(r + 1) * LANE] for r in range(reps)])

    def deferred_pv(n, c, slot):
        v = streams[n][2](jnp.maximum(c - 1, 0))
        acc_ref[n] = st_ref[n, 2] * acc_ref[n] + jnp.dot(p_ref[n, 1 - slot], v, preferred_element_type=F32)

    def softmax(n, s, slot, vis=None):
        if vis is not None:
            s = jnp.where(vis, s, NEG_INF)
        m_old = st_ref[n, 0]
        m_new = jnp.maximum(m_old, jnp.max(s, axis=-1, keepdims=True))
        alpha = jnp.exp2(m_old - m_new)
        p = jnp.exp2(s - jnp.concatenate([m_new] * reps, axis=1))
        st_ref[n, 0] = m_new
        st_ref[n, 1] = alpha * st_ref[n, 1] + lane_partial_sum(p)
        st_ref[n, 2] = alpha
        p_ref[n, slot] = p.astype(BF16)

    def stage(c, slot):
        for n in range(len(streams)):
            s_ref[n, 1 - slot] = scores(n, c + 1)
        for n in range(len(streams)):
            deferred_pv(n, c, slot)
        for n in range(len(streams)):
            softmax(n, s_ref[n, slot], slot)

    for n in range(len(streams)):
        M = s_ref.shape[2]
        st_ref[n, 0] = jnp.full((M, LANE), NEG_INF, F32)
        st_ref[n, 1] = jnp.zeros((M, LANE), F32)
        st_ref[n, 2] = jnp.ones((M, LANE), F32)
        acc_ref[n] = jnp.zeros((M, LANE), F32)
        p_ref[n, 1] = jnp.zeros(p_ref.shape[2:], BF16)
        s_ref[n, 0] = scores(n, 0)

    def pair(cc, carry):
        stage(2 * cc, 0)
        stage(2 * cc + 1, 1)
        return carry

    lax.fori_loop(0, n_int // 2, pair, 0)

    @pl.when(n_int % 2 == 1)
    def _():
        stage(n_int - 1, 0)

    slot = n_int % 2
    res = []
    for n in range(len(streams)):
        v_at = streams[n][2]
        acc = st_ref[n, 2] * acc_ref[n] + jnp.dot(p_ref[n, 1 - slot], v_at(jnp.maximum(n_int - 1, 0)),
                                                 preferred_element_type=F32)
        s = jnp.where(diag_vis, s_ref[n, slot], NEG_INF)
        m_old = st_ref[n, 0]
        m_new = jnp.maximum(m_old, jnp.max(s, axis=-1, keepdims=True))
        alpha = jnp.exp2(m_old - m_new)
        p = jnp.exp2(s - jnp.concatenate([m_new] * reps, axis=1))
        l = jnp.sum(alpha * st_ref[n, 1] + lane_partial_sum(p), axis=-1, keepdims=True)
        res.append((l, alpha * acc + jnp.dot(p.astype(BF16), v_at(n_int), preferred_element_type=F32)))
    return res


def _window_attention(q, ks, vs, viss, biases=None):
    ss = []
    for j, (k, vis) in enumerate(zip(ks, viss)):
        s = _scores(q, k, None if biases is None else biases[j])
        ss.append(s if vis is None else jnp.where(vis, s, NEG_INF))
    m = functools.reduce(jnp.maximum, [jnp.max(s, axis=-1, keepdims=True) for s in ss])
    ps = [jnp.exp2(s - m) for s in ss]
    l = functools.reduce(jnp.add, [jnp.sum(p, axis=-1, keepdims=True) for p in ps])
    acc = functools.reduce(jnp.add, [jnp.dot(p.astype(BF16), v, preferred_element_type=F32) for p, v in zip(ps, vs)])
    return m, l, acc


def _rel_pos(rows, cols):
    return (lax.broadcasted_iota(jnp.int32, (rows, cols), 1)
            - lax.broadcasted_iota(jnp.int32, (rows, cols), 0))


def _rms_rows(c, g, eps):
    return c * lax.rsqrt(jnp.mean(c * c, axis=-1, keepdims=True) + eps) * g


def _mla_proj_kernel(x_ref, win_ref, qn_ref, kvn_ref, wq_ref, wqs_ref, wk_ref, wv_ref, cos_ref, sin_ref,
                     q_out, k_out, v_out, *, scale):
    xb = x_ref[...].astype(BF16)
    c = jnp.dot(xb, win_ref[...], preferred_element_type=F32)
    r0, r1 = MLA_Q_RANK, MLA_Q_RANK + MLA_KV_RANK
    cq = _rms_rows(c[:, :r0], qn_ref[...], 1e-6).astype(BF16)
    ckv = _rms_rows(c[:, r0:r1], kvn_ref[...], 1e-6).astype(BF16)
    cos, sin = cos_ref[...], sin_ref[...]
    kr = c[:, r1:r1 + LANE] * cos + c[:, r1 + LANE:r1 + 2 * LANE] * sin
    v_out[...] = jnp.dot(ckv, wv_ref[...], preferred_element_type=F32).astype(v_out.dtype)
    for h in range(MLA_HEADS):
        sl = slice(h * LANE, (h + 1) * LANE)
        qh = jnp.dot(cq, wq_ref[:, sl], preferred_element_type=F32)
        qhs = jnp.dot(cq, wqs_ref[:, sl], preferred_element_type=F32)
        q_out[:, sl] = ((qh * cos + qhs * sin) * scale).astype(q_out.dtype)
        kh = jnp.dot(ckv, wk_ref[:, sl], preferred_element_type=F32)
        k_out[:, sl] = (kh + kr).astype(k_out.dtype)


def _mla_attn_kernel(q_ref, k_ref, v_ref, o_ref, *scratch, tq, tk):
    q0 = pl.program_id(2) * tq
    c_d = q0 // tk
    causal = _rel_pos(tq, tk) + (c_d * tk - q0) <= 0

    def chunk(c):
        return pl.ds(pl.multiple_of(c * tk, tk), tk)

    def v_at(c):
        return v_ref[0, chunk(c), :]

    streams = []
    for hh in range(2):
        sl = slice(hh * LANE, (hh + 1) * LANE)
        streams.append((q_ref[0, :, sl], lambda c, sl=sl: k_ref[0, chunk(c), sl], v_at, None))
    outs = [acc / l for l, acc in _flash_causal(streams, c_d, causal, scratch)]
    lane = lax.broadcasted_iota(jnp.int32, (tq, LANE), 1)
    o_ref[0] = jnp.where(lane < MLA_V, outs[0], outs[1]).astype(o_ref.dtype)


def _mla_mixer(x, w_in, q_norm, kv_norm, w_qb, w_kvb, w_o, ln_g, ln_b, B, S):
    N, D = x.shape
    H, dq = MLA_HEADS, MLA_NOPE + MLA_ROPE
    half = MLA_ROPE // 2
    r0, r1 = MLA_Q_RANK, MLA_Q_RANK + MLA_KV_RANK
    z = lambda *s: jnp.zeros(s, F32)
    swap = lambda a: jnp.concatenate([a[..., half:], a[..., :half]], axis=-1)
    kr_w = w_in[:, r1:]
    win = jnp.concatenate([w_in[:, :r1],
                           z(D, MLA_NOPE), kr_w, z(D, LANE - dq),
                           z(D, MLA_NOPE), swap(kr_w), z(D, LANE - dq)], axis=1).astype(BF16)
    wq3 = w_qb.reshape(r0, H, dq)
    wq = jnp.concatenate([wq3, z(r0, H, LANE - dq)], axis=-1).reshape(r0, H * LANE).astype(BF16)
    wqs = jnp.concatenate([z(r0, H, MLA_NOPE), swap(wq3[..., MLA_NOPE:]), z(r0, H, LANE - dq)],
                          axis=-1).reshape(r0, H * LANE).astype(BF16)
    wkv3 = w_kvb.reshape(MLA_KV_RANK, H, MLA_NOPE + MLA_V)
    wk = jnp.concatenate([wkv3[..., :MLA_NOPE], z(MLA_KV_RANK, H, LANE - MLA_NOPE)],
                         axis=-1).reshape(MLA_KV_RANK, H * LANE).astype(BF16)
    wv = wkv3[..., MLA_NOPE:].reshape(MLA_KV_RANK, H * MLA_V).astype(BF16)
    freq = ROPE_THETA ** (-jnp.arange(half, dtype=F32) / half)
    ang = jnp.arange(S, dtype=F32)[:, None] * freq
    cos, sin = jnp.cos(ang), jnp.sin(ang)
    ones, zer = jnp.ones((S, MLA_NOPE), F32), jnp.zeros((S, LANE - dq), F32)
    cos_t = jnp.concatenate([ones, cos, cos, zer], axis=1)
    sin_t = jnp.concatenate([0 * ones, -sin, sin, zer], axis=1)

    tm = 256
    nwin = win.shape[1]
    q, k, v = pl.pallas_call(
        functools.partial(_mla_proj_kernel, scale=dq ** -0.5 * LOG2E),
        out_shape=[jax.ShapeDtypeStruct((N, H * LANE), BF16),
                   jax.ShapeDtypeStruct((N, H * LANE), BF16),
                   jax.ShapeDtypeStruct((N, H * MLA_V), BF16)],
        grid=(N // tm,),
        in_specs=[pl.BlockSpec((tm, D), lambda i: (i, 0)),
                  pl.BlockSpec((D, nwin), lambda i: (0, 0)),
                  pl.BlockSpec((1, r0), lambda i: (0, 0)),
                  pl.BlockSpec((1, MLA_KV_RANK), lambda i: (0, 0)),
                  pl.BlockSpec((r0, H * LANE), lambda i: (0, 0)),
                  pl.BlockSpec((r0, H * LANE), lambda i: (0, 0)),
                  pl.BlockSpec((MLA_KV_RANK, H * LANE), lambda i: (0, 0)),
                  pl.BlockSpec((MLA_KV_RANK, H * MLA_V), lambda i: (0, 0)),
                  pl.BlockSpec((tm, LANE), lambda i: (i % (S // tm), 0)),
                  pl.BlockSpec((tm, LANE), lambda i: (i % (S // tm), 0))],
        out_specs=[pl.BlockSpec((tm, H * LANE), lambda i: (i, 0)),
                   pl.BlockSpec((tm, H * LANE), lambda i: (i, 0)),
                   pl.BlockSpec((tm, H * MLA_V), lambda i: (i, 0))],
        compiler_params=_cparams("parallel"),
        name="mla_proj",
    )(x, win, q_norm.reshape(1, r0), kv_norm.reshape(1, MLA_KV_RANK), wq, wqs, wk, wv, cos_t, sin_t)

    tq, tk = ATTN_TQ, ATTN_TK
    o = pl.pallas_call(
        functools.partial(_mla_attn_kernel, tq=tq, tk=tk),
        out_shape=jax.ShapeDtypeStruct((B, S, H * MLA_V), BF16),
        grid=(B, H // 2, S // tq),
        in_specs=[pl.BlockSpec((1, tq, 2 * LANE), lambda b, h, i: (b, i, h)),
                  pl.BlockSpec((1, S, 2 * LANE), lambda b, h, i: (b, 0, h)),
                  pl.BlockSpec((1, S, LANE), lambda b, h, i: (b, 0, h))],
        out_specs=pl.BlockSpec((1, tq, LANE), lambda b, h, i: (b, i, h)),
        scratch_shapes=_flash_scratch(2, tq, tk),
        compiler_params=_cparams("parallel", "parallel", "arbitrary"),
        name="mla_attn",
    )(q.reshape(B, S, H * LANE), k.reshape(B, S, H * LANE), v.reshape(B, S, H * MLA_V))
    return _outproj_ln(o.reshape(N, H * MLA_V), w_o, x, ln_g, ln_b, name="mla_out_ln")


def _diff_attn_kernel(slope_ref, lam_ref, sub_ref, q_ref, k_ref, v_ref, o_ref, *scratch, tq, tk, lam_init):
    h = pl.program_id(1)
    q0 = pl.program_id(2) * tq
    c_d = q0 // tk
    slope = slope_ref[h] * LOG2E
    lv = lam_ref[...]
    lam = (jnp.exp(jnp.sum(lv[0:1] * lv[1:2], axis=-1, keepdims=True))
           - jnp.exp(jnp.sum(lv[2:3] * lv[3:4], axis=-1, keepdims=True)) + lam_init)
    lane = lax.broadcasted_iota(jnp.int32, (tq, LANE), 1)
    qf = q_ref[0].astype(F32) * (DIFF_HD ** -0.5 * LOG2E)
    qs = [jnp.where(lane < DIFF_HD, qf, 0.0).astype(BF16), jnp.where(lane < DIFF_HD, 0.0, qf).astype(BF16)]
    col = lax.broadcasted_iota(jnp.int32, (1, tk), 1)

    def chunk(c):
        return pl.ds(pl.multiple_of(c * tk, tk), tk)

    def k_at(c):
        return k_ref[0, chunk(c), :]

    def v_at(c):
        return v_ref[0, chunk(c), :]

    def bias_at(c):
        return slope * (col + c * tk).astype(F32)

    streams = [(q, k_at, v_at, bias_at) for q in qs]
    causal = _rel_pos(tq, tk) + (c_d * tk - q0) <= 0
    outs = [acc / l for l, acc in _flash_causal(streams, c_d, causal, scratch)]
    a = outs[0] - lam * outs[1]
    a = _rms_rows(a, sub_ref[...], 1e-5) * (1.0 - lam_init)
    o_ref[0] = a.astype(o_ref.dtype)


def _diff_mixer(x, w_in, lam_q1, lam_k1, lam_q2, lam_k2, subln, w_o, layer_idx, ln_g, ln_b, B, S):
    N, D = x.shape
    H, d = DIFF_HEADS, DIFF_HD
    nq = H * 2 * d
    (qkv,) = _linear(x, w_in.astype(BF16), [3 * nq], [BF16], name="diff_proj")
    qkv = qkv.reshape(B, S, 3 * nq)
    lam_init = 0.8 - 0.6 * math.exp(-0.3 * layer_idx)
    lamv = jnp.zeros((8, LANE), F32).at[:4, :d].set(jnp.stack([lam_q1, lam_k1, lam_q2, lam_k2]).astype(F32))
    tq, tk = ATTN_TQ, ATTN_TK
    o = pl.pallas_call(
        functools.partial(_diff_attn_kernel, tq=tq, tk=tk, lam_init=lam_init),
        out_shape=jax.ShapeDtypeStruct((B, S, nq), BF16),
        grid=(B, H, S // tq),
        in_specs=[pl.BlockSpec(memory_space=pltpu.SMEM),
                  pl.BlockSpec((8, LANE), lambda b, h, i: (0, 0)),
                  pl.BlockSpec((1, 2 * d), lambda b, h, i: (0, 0)),
                  pl.BlockSpec((1, tq, LANE), lambda b, h, i: (b, i, h)),
                  pl.BlockSpec((1, S, LANE), lambda b, h, i: (b, 0, H + h)),
                  pl.BlockSpec((1, S, LANE), lambda b, h, i: (b, 0, 2 * H + h))],
        out_specs=pl.BlockSpec((1, tq, LANE), lambda b, h, i: (b, i, h)),
        scratch_shapes=_flash_scratch(2, tq, tk),
        compiler_params=_cparams("parallel", "parallel", "arbitrary"),
        name="diff_attn",
    )(jnp.asarray(_alibi(H)), lamv, subln.reshape(1, 2 * d).astype(F32), qkv, qkv, qkv)
    return _outproj_ln(o.reshape(N, nq), w_o, x, ln_g, ln_b, name="diff_out_ln")


SEL_LANE0 = NSA_HD
SEL_OFF = 1e30
ALIBI_LANE0 = 96
POS_SPLIT = 64


def _nsa_compress_kernel(ak_ref, av_ref, plo_ref, phi_ref, wklo_ref, wkhi_ref, wvlo_ref, wvhi_ref, kc_out, vc_out):
    def one(a_ref, wlo_ref, whi_ref, out):
        a = a_ref[0]
        lo = jnp.dot((a + plo_ref[...]).astype(BF16), wlo_ref[...], preferred_element_type=F32)
        hi = jnp.dot((a + phi_ref[...]).astype(BF16), whi_ref[...], preferred_element_type=F32)
        out[0] = (lo + pltpu.roll(hi, hi.shape[0] - 1, 0)).astype(out.dtype)

    one(ak_ref, wklo_ref, wkhi_ref, kc_out)
    one(av_ref, wvlo_ref, wvhi_ref, vc_out)


def _pack_heads(o, tq):
    lane = lax.broadcasted_iota(jnp.int32, (tq, LANE), 1)
    p01 = jnp.where(lane < NSA_HD, o[0:tq], o[tq:2 * tq])
    p23 = jnp.where(lane < NSA_HD, o[2 * tq:3 * tq], o[3 * tq:4 * tq])
    return jnp.concatenate([p01, p23], axis=1)


def _nsa_attn_kernel(q_ref, qc_ref, gl_ref, kc_ref, vc_ref, ks_ref, vs_ref, kw_ref, vw_ref, ktab_ref, ctab_ref,
                     c2s_ref, rep_ref, o_ref, *scratch, tq, tk, tks, n_slc):
    i = pl.program_id(2)
    t0 = i * tq
    hpg = NSA_HEADS // NSA_GROUPS
    M = hpg * tq
    q = jnp.concatenate([q_ref[0, :, j * LANE:(j + 1) * LANE]
                         + jnp.broadcast_to(qc_ref[0, j:j + 1, :], (tq, LANE)).astype(BF16)
                         for j in range(hpg)], axis=0)
    trow1 = t0 + lax.broadcasted_iota(jnp.int32, (tq, 1), 0)
    trow = jnp.concatenate([trow1] * hpg, axis=0)
    lane_m = lax.broadcasted_iota(jnp.int32, (M, LANE), 1)

    sc = lax.dot_general(q, kc_ref[0] + ctab_ref[...], (((1,), (1,)), ((), ())), preferred_element_type=F32)
    sc = jnp.where(trow >= lane_m * NSA_CMP_STRIDE + (NSA_CMP_LEN - 1), sc, NEG_INF)
    e = jnp.exp2(sc - jnp.max(sc, axis=-1, keepdims=True))
    p_c = jnp.where(trow >= NSA_CMP_LEN - 1, e / jnp.sum(e, axis=-1, keepdims=True), 0.0)
    o_c = jnp.dot(p_c.astype(BF16), vc_ref[0], preferred_element_type=F32)
    psum = p_c[0:tq]
    for j in range(1, hpg):
        psum = psum + p_c[j * tq:(j + 1) * tq]
    ph, plw = _split_bf16(psum)
    imp = (jnp.dot(ph, c2s_ref[...], preferred_element_type=F32)
           + jnp.dot(plw, c2s_ref[...], preferred_element_type=F32))

    lane = lax.broadcasted_iota(jnp.int32, (tq, LANE), 1)
    sidx = lane - SEL_LANE0
    valid = (sidx >= 0) & (sidx < n_slc)
    forced = (sidx == 0) | (sidx == trow1 // NSA_SLC_LEN)
    future = sidx * NSA_SLC_LEN > trow1
    score = imp + jnp.where(forced, NSA_FORCE, 0.0) - jnp.where(future, 2.0 * NSA_FORCE, 0.0)
    score = jnp.where(valid, score, -jnp.inf)
    sel = jnp.zeros((tq, LANE), jnp.bool_)
    for _ in range(min(NSA_TOP_N, n_slc)):
        mx = jnp.max(score, axis=-1, keepdims=True)
        pick = lane == jnp.min(jnp.where(score == mx, lane, LANE), axis=-1, keepdims=True)
        sel = sel | pick
        score = jnp.where(pick, -jnp.inf, score)
    selbias = jnp.where(valid & jnp.logical_not(sel), -SEL_OFF, 0.0).astype(BF16)

    q_s = q + jnp.concatenate([selbias] * hpg, axis=0)
    half = M // 2
    c_s = t0 // tks
    causal_s = jnp.concatenate([_rel_pos(tq, tks)] * (hpg // 2), axis=0) + (c_s * tks - t0) <= 0

    def chunk_s(c):
        return pl.ds(pl.multiple_of(c * tks, tks), tks)

    def ks_at(c):
        return ks_ref[0, chunk_s(c), :] + ktab_ref[chunk_s(c), :]

    def vs_at(c):
        return vs_ref[0, chunk_s(c), :]

    res = _flash_causal([(q_s[:half], ks_at, vs_at, None), (q_s[half:], ks_at, vs_at, None)], c_s, causal_s, scratch)
    o_s = jnp.concatenate([acc / l for l, acc in res], axis=0)

    c_d = t0 // tk
    d_diag = jnp.concatenate([_rel_pos(tq, tk)] * hpg, axis=0) + (c_d * tk - t0)

    def chunk(c):
        return pl.ds(pl.multiple_of(c * tk, tk), tk)

    cw = [jnp.maximum(c_d - 2, 0), jnp.maximum(c_d - 1, 0), c_d]
    vis_w = [(d_diag - 2 * tk > -NSA_WINDOW) & (c_d >= 2), jnp.broadcast_to(c_d >= 1, d_diag.shape), d_diag <= 0]
    _, l_w, acc_w = _window_attention(q, [kw_ref[0, chunk(c), :] + ktab_ref[chunk(c), :] for c in cw],
                                      [vw_ref[0, chunk(c), :] for c in cw], vis_w)
    o_w = acc_w / l_w

    gh, glw = _split_bf16(jax.nn.sigmoid(gl_ref[0]))
    gr = jnp.dot(gh, rep_ref[0], preferred_element_type=F32) + jnp.dot(glw, rep_ref[0], preferred_element_type=F32)
    w = hpg * NSA_HD
    o = (gr[:, 0:w] * _pack_heads(o_c, tq) + gr[:, w:2 * w] * _pack_heads(o_s, tq)
         + gr[:, 2 * w:3 * w] * _pack_heads(o_w, tq))
    o_ref[0] = o.astype(o_ref.dtype)


def _nsa_mixer(x, w_in, w_phi_k, w_phi_v, cmp_pos, w_o, ln_g, ln_b, B, S):
    N, D = x.shape
    H, G, d = NSA_HEADS, NSA_GROUPS, NSA_HD
    hpg = H // G
    L, st = NSA_CMP_LEN, NSA_CMP_STRIDE
    n_slc = S // NSA_SLC_LEN
    assert S % 256 == 0 and n_slc <= LANE - SEL_LANE0 and L == 2 * st
    cuts = [H * d + i * G * d for i in range(7)]
    wq, wkc, wvc, wks, wvs, wkw, wvw, wgl = jnp.split(w_in, cuts, axis=1)
    z = lambda *s: jnp.zeros(s, F32)
    pad_heads = lambda w, n: jnp.concatenate([w.reshape(D, n, d), z(D, n, LANE - d)], axis=-1).reshape(D, n * LANE)
    dup_heads = lambda w, n: jnp.concatenate([w.reshape(D, n, d)] * 2, axis=-1).reshape(D, n * LANE)
    wgl_p = jnp.concatenate([wgl, z(D, LANE - wgl.shape[1])], axis=1)
    wcat = jnp.concatenate([pad_heads(wq * (d ** -0.5 * LOG2E), H), wkc, wvc, pad_heads(wks, G), dup_heads(wvs, G),
                            pad_heads(wkw, G), dup_heads(wvw, G), wgl_p], axis=1).astype(BF16)
    q, kc, vc, ks, vs, kw, vw, gl = _linear(
        x, wcat, [H * LANE, G * d, G * d, G * LANE, G * LANE, G * LANE, G * LANE, LANE],
        [BF16, F32, F32, BF16, BF16, BF16, BF16, F32], name="nsa_proj")

    nrow = S // st
    eye = jnp.eye(G, dtype=F32)

    def phi_w(w_phi, half, dup):
        w = w_phi.reshape(L, d, d)[half * st:(half + 1) * st]
        wd = jnp.concatenate([w, w if dup else jnp.zeros_like(w)], axis=-1)
        return jnp.einsum('ldc,gh->lgdhc', wd, eye).reshape(st * G * d, G * LANE).astype(BF16)

    pos = lambda half: jnp.broadcast_to(cmp_pos[half * st:(half + 1) * st, None, :], (st, G, d)).reshape(1, st * G * d)
    wide = st * G * d
    cspec = pl.BlockSpec((1, nrow, wide), lambda b: (b, 0, 0))
    wspec = pl.BlockSpec((wide, G * LANE), lambda b: (0, 0))
    pspec = pl.BlockSpec((1, wide), lambda b: (0, 0))
    ospec = pl.BlockSpec((1, nrow, G * LANE), lambda b: (b, 0, 0))
    kcmp, vcmp = pl.pallas_call(
        _nsa_compress_kernel,
        out_shape=[jax.ShapeDtypeStruct((B, nrow, G * LANE), BF16)] * 2,
        grid=(B,),
        in_specs=[cspec, cspec, pspec, pspec, wspec, wspec, wspec, wspec],
        out_specs=[ospec, ospec],
        compiler_params=_cparams("parallel"),
        name="nsa_compress",
    )(kc.reshape(B, nrow, wide), vc.reshape(B, nrow, wide), pos(0), pos(1),
      phi_w(w_phi_k, 0, False), phi_w(w_phi_k, 1, False), phi_w(w_phi_v, 0, True), phi_w(w_phi_v, 1, True))

    def pos_lanes(tab, pos):
        tab[:, ALIBI_LANE0:ALIBI_LANE0 + 3] = (pos // POS_SPLIT)[:, None]
        tab[:, ALIBI_LANE0 + 3:ALIBI_LANE0 + 6] = (pos % POS_SPLIT)[:, None]
        return tab

    ktab = np.zeros((S, LANE), np.float32)
    ktab[np.arange(S), SEL_LANE0 + np.arange(S) // NSA_SLC_LEN] = 1.0
    ktab = pos_lanes(ktab, np.arange(S))
    ctab = pos_lanes(np.zeros((nrow, LANE), np.float32), np.arange(nrow) * st + (L - 1))
    s2 = jnp.asarray(_alibi(H) * LOG2E, F32)
    parts = []
    for _ in range(3):
        part = s2.astype(BF16).astype(F32)
        parts.append(part)
        s2 = s2 - part
    qcoef = jnp.zeros((H, LANE), F32).at[:, ALIBI_LANE0:ALIBI_LANE0 + 6].set(
        jnp.stack([POS_SPLIT * p for p in parts] + parts, axis=1)).reshape(G, hpg, LANE)
    cmp_start = np.arange(nrow) * st
    slc_start = np.arange(n_slc) * NSA_SLC_LEN
    ov = (cmp_start[:, None] < slc_start[None, :] + NSA_SLC_LEN) & (cmp_start[:, None] + L > slc_start[None, :])
    ov[(S - L) // st + 1:] = False
    c2s = np.zeros((nrow, LANE), np.float32)
    c2s[:, SEL_LANE0:SEL_LANE0 + n_slc] = ov
    rep = np.zeros((G, LANE, 3 * hpg * d), np.float32)
    for g in range(G):
        for j in range(hpg):
            for br in range(3):
                rep[g, (g * hpg + j) * 3 + br, br * hpg * d + j * d:br * hpg * d + (j + 1) * d] = 1.0

    tq, tk, tks = 128, 256, ATTN_TK
    assert NSA_WINDOW == 2 * tk and tk % tq == 0
    kvspec = pl.BlockSpec((1, S, LANE), lambda b, g, i: (b, 0, g))
    cmpspec = pl.BlockSpec((1, nrow, LANE), lambda b, g, i: (b, 0, g))
    o = pl.pallas_call(
        functools.partial(_nsa_attn_kernel, tq=tq, tk=tk, tks=tks, n_slc=n_slc),
        out_shape=jax.ShapeDtypeStruct((B, S, H * d), BF16),
        grid=(B, G, S // tq),
        in_specs=[pl.BlockSpec((1, tq, hpg * LANE), lambda b, g, i: (b, i, g)),
                  pl.BlockSpec((1, hpg, LANE), lambda b, g, i: (g, 0, 0)),
                  pl.BlockSpec((1, tq, LANE), lambda b, g, i: (b, i, 0)),
                  cmpspec, cmpspec, kvspec, kvspec, kvspec, kvspec,
                  pl.BlockSpec((S, LANE), lambda b, g, i: (0, 0)),
                  pl.BlockSpec((nrow, LANE), lambda b, g, i: (0, 0)),
                  pl.BlockSpec((nrow, LANE), lambda b, g, i: (0, 0)),
                  pl.BlockSpec((1, LANE, 3 * hpg * d), lambda b, g, i: (g, 0, 0))],
        out_specs=pl.BlockSpec((1, tq, hpg * d), lambda b, g, i: (b, i, g)),
        scratch_shapes=_flash_scratch(2, hpg * tq // 2, tks),
        compiler_params=_cparams("parallel", "parallel", "arbitrary"),
        name="nsa_attn",
    )(q.reshape(B, S, H * LANE), qcoef, gl.reshape(B, S, LANE), kcmp, vcmp,
      ks.reshape(B, S, G * LANE), vs.reshape(B, S, G * LANE), kw.reshape(B, S, G * LANE), vw.reshape(B, S, G * LANE),
      jnp.asarray(ktab, BF16), jnp.asarray(ctab, BF16), jnp.asarray(c2s, BF16), jnp.asarray(rep, BF16))
    return _outproj_ln(o.reshape(N, H * d), w_o, x, ln_g, ln_b, name="nsa_out_ln")


def _dil_attn_kernel(slope_ref, q_ref, k_ref, v_ref, o_ref, lse_ref, *, t, dil):
    i = pl.program_id(2)
    d = DIL_HD
    rel = _rel_pos(t, t)
    relf = rel.astype(F32)
    rel2 = jnp.concatenate([rel, rel], axis=0)
    lane = lax.broadcasted_iota(jnp.int32, (t, LANE), 1)
    scale = d ** -0.5 * LOG2E

    def chunk(c):
        return pl.ds(pl.multiple_of(c * t, t), t)

    cs = [jnp.maximum(i - 1, 0), i]
    viss = [(rel2 >= 0) & (i >= 1), rel2 <= 0]
    lse_t = jnp.zeros((t, LANE), F32)
    for hp in range(DIL_HEADS // 2):
        sl = slice(hp * LANE, (hp + 1) * LANE)
        qf = q_ref[0, :, sl].astype(F32) * scale
        q2 = jnp.concatenate([jnp.where(lane < d, qf, 0.0), jnp.where(lane < d, 0.0, qf)], axis=0).astype(BF16)
        sa, sb = slope_ref[2 * hp] * (dil * LOG2E), slope_ref[2 * hp + 1] * (dil * LOG2E)
        bias = jnp.concatenate([sa * relf, sb * relf], axis=0)
        shift = jnp.concatenate([jnp.full((t, 1), sa * t, F32), jnp.full((t, 1), sb * t, F32)], axis=0)
        m, l, acc = _window_attention(q2, [k_ref[0, chunk(c), sl] for c in cs], [v_ref[0, chunk(c), sl] for c in cs],
                                      viss, [bias - shift, bias])
        o = acc / l
        o_ref[0, :, hp * LANE:(hp + 1) * LANE] = jnp.where(lane < d, o[:t], o[t:]).astype(o_ref.dtype)
        lse = m + jnp.log2(l)
        lse_t = jnp.where(lane == 2 * hp, lse[:t], lse_t)
        lse_t = jnp.where(lane == 2 * hp + 1, lse[t:], lse_t)
    lse_ref[0] = lse_t


def _dil_merge_ln_kernel(o0_ref, o1_ref, o2_ref, l0_ref, l1_ref, l2_ref, rep_ref, w_ref, x_ref, g_ref, b_ref, out_ref):
    ls = [l0_ref[...], l1_ref[...], l2_ref[...]]
    mx = jnp.maximum(jnp.maximum(ls[0], ls[1]), ls[2])
    es = [jnp.exp2(v - mx) for v in ls]
    tot = es[0] + es[1] + es[2]
    o = None
    for e, o_ref in zip(es, (o0_ref, o1_ref, o2_ref)):
        wh, wl = _split_bf16(e / tot)
        wrep = jnp.dot(wh, rep_ref[...], preferred_element_type=F32) + jnp.dot(wl, rep_ref[...], preferred_element_type=F32)
        term = wrep * o_ref[...].astype(F32)
        o = term if o is None else o + term
    y = jnp.dot(o.astype(BF16), w_ref[...], preferred_element_type=F32)
    out_ref[...] = _layer_norm_rows(DN_ALPHA * x_ref[...] + y, g_ref[...], b_ref[...])


def _dil_mixer(x, w_in, w_o, ln_g, ln_b, B, S):
    N, D = x.shape
    H, d = DIL_HEADS, DIL_HD
    n_pat = len(DIL_PATTERNS)
    hd = H * d
    ncol = n_pat * 3 * hd
    (qkv,) = _linear(x, w_in.astype(BF16), [ncol], [BF16], name="dil_proj")
    tq = 128
    slopes = jnp.asarray(_alibi(H))
    outs, lses = [], []
    for p, (win, dil) in enumerate(DIL_PATTERNS):
        ls = S // dil
        assert ls % tq == 0 and win == dil * tq
        nblk = ncol // hd
        view = qkv.reshape(B, ls, dil * ncol)
        o, lse = pl.pallas_call(
            functools.partial(_dil_attn_kernel, t=tq, dil=dil),
            out_shape=[jax.ShapeDtypeStruct((B, ls, dil * hd), BF16),
                       jax.ShapeDtypeStruct((B, ls, dil * LANE), F32)],
            grid=(B, dil, ls // tq),
            in_specs=[pl.BlockSpec(memory_space=pltpu.SMEM),
                      pl.BlockSpec((1, tq, hd), lambda b, r, i, p=p, nblk=nblk: (b, i, r * nblk + 3 * p)),
                      pl.BlockSpec((1, ls, hd), lambda b, r, i, p=p, nblk=nblk: (b, 0, r * nblk + 3 * p + 1)),
                      pl.BlockSpec((1, ls, hd), lambda b, r, i, p=p, nblk=nblk: (b, 0, r * nblk + 3 * p + 2))],
            out_specs=[pl.BlockSpec((1, tq, hd), lambda b, r, i: (b, i, r)),
                       pl.BlockSpec((1, tq, LANE), lambda b, r, i: (b, i, r))],
            compiler_params=_cparams("parallel", "parallel", "arbitrary"),
            name=f"dil_attn_{p}",
        )(slopes, view, view, view)
        outs.append(o.reshape(N, hd))
        lses.append(lse.reshape(N, LANE))

    rep = np.zeros((LANE, hd), np.float32)
    for h in range(H):
        rep[h, h * d:(h + 1) * d] = 1.0
    tm = 512
    row = lambda n: pl.BlockSpec((tm, n), lambda i: (i, 0))
    full = lambda a, b: pl.BlockSpec((a, b), lambda i: (0, 0))
    return pl.pallas_call(
        _dil_merge_ln_kernel,
        out_shape=jax.ShapeDtypeStruct((N, D), F32),
        grid=(N // tm,),
        in_specs=[row(hd), row(hd), row(hd), row(LANE), row(LANE), row(LANE), full(LANE, hd), full(hd, D),
                  row(D), full(1, D), full(1, D)],
        out_specs=row(D),
        compiler_params=_cparams("parallel"),
        name="dil_merge_out_ln",
    )(*outs, *lses, jnp.asarray(rep, BF16), w_o.astype(BF16), x, ln_g.reshape(1, D), ln_b.reshape(1, D))


def _split_bf16(a):
    hi = a.astype(BF16)
    return hi, (a - hi.astype(F32)).astype(BF16)


def _router_kernel(x_ref, wh_ref, wl_ref, b_ref, gate_ref):
    xh, xl = _split_bf16(x_ref[...])
    wh, wl = wh_ref[...], wl_ref[...]
    logits = (jnp.dot(xh, wh, preferred_element_type=F32) + jnp.dot(xl, wh, preferred_element_type=F32)
              + jnp.dot(xh, wl, preferred_element_type=F32)) + b_ref[...]
    tm = logits.shape[0]
    lane = lax.broadcasted_iota(jnp.int32, (tm, LANE), 1)
    big = jnp.int32(LANE)
    lg = jnp.where(lane < MOE_GROUPS, logits, -jnp.inf)
    mg = jnp.max(lg, axis=-1, keepdims=True)
    sg = jnp.sum(jnp.exp(lg - mg), axis=-1, keepdims=True)
    pg_top = 1.0 / sg
    g_top = jnp.min(jnp.where(lg == mg, lane, big), axis=-1, keepdims=True)
    e_lo = MOE_GROUPS + MOE_EPG * g_top
    in_grp = (lane >= e_lo) & (lane < e_lo + MOE_EPG)
    le = jnp.where(in_grp, logits, -jnp.inf)
    me = jnp.max(le, axis=-1, keepdims=True)
    ee = jnp.exp(le - me)
    se = jnp.sum(ee, axis=-1, keepdims=True)
    pe = jnp.where(in_grp, ee / se, -1.0)
    v1 = jnp.max(pe, axis=-1, keepdims=True)
    i1 = jnp.min(jnp.where(pe == v1, lane, big), axis=-1, keepdims=True)
    pe2 = jnp.where(lane == i1, -1.0, pe)
    v2 = jnp.max(pe2, axis=-1, keepdims=True)
    i2 = jnp.min(jnp.where(pe2 == v2, lane, big), axis=-1, keepdims=True)
    tot = v1 + v2
    gate_ref[...] = (jnp.where(lane == i1, (v1 / tot) * pg_top, 0.0)
                     + jnp.where(lane == i2, (v2 / tot) * pg_top, 0.0))


def _moe_dense_kernel(x_ref, gate_ref, w1_ref, w3_ref, w2_ref, g_ref, b_ref, out_ref, acc_ref):
    e = pl.program_id(1)

    @pl.when(e == 0)
    def _():
        acc_ref[...] = jnp.zeros_like(acc_ref)

    xb = x_ref[...].astype(BF16)
    h1 = jnp.dot(xb, w1_ref[0], preferred_element_type=F32)
    h3 = jnp.dot(xb, w3_ref[0], preferred_element_type=F32)
    hid = (h1 * jax.nn.sigmoid(h1) * h3).astype(BF16)
    y = jnp.dot(hid, w2_ref[0], preferred_element_type=F32)
    gates = gate_ref[...]
    lane = lax.broadcasted_iota(jnp.int32, gates.shape, 1)
    ge = jnp.sum(jnp.where(lane == MOE_GROUPS + e, gates, 0.0), axis=-1, keepdims=True)
    acc_ref[...] += ge * y

    @pl.when(e == MOE_EXPERTS - 1)
    def _():
        out_ref[...] = _layer_norm_rows(DN_ALPHA * x_ref[...] + acc_ref[...], g_ref[...], b_ref[...])


def _hier_moe_ln(x, wg, bg, we, be, w1, w3, w2, ln_g, ln_b):
    N, D = x.shape
    wr = jnp.concatenate([wg, jnp.moveaxis(we, 0, 1).reshape(D, MOE_EXPERTS),
                          jnp.zeros((D, LANE - MOE_GROUPS - MOE_EXPERTS), F32)], axis=1)
    br = jnp.concatenate([bg, be.reshape(-1), jnp.zeros((LANE - MOE_GROUPS - MOE_EXPERTS,), F32)]).reshape(1, LANE)
    wrh, wrl = _split_bf16(wr)
    tm = 512
    gates = pl.pallas_call(
        _router_kernel,
        out_shape=jax.ShapeDtypeStruct((N, LANE), F32),
        grid=(N // tm,),
        in_specs=[pl.BlockSpec((tm, D), lambda i: (i, 0)),
                  pl.BlockSpec((D, LANE), lambda i: (0, 0)),
                  pl.BlockSpec((D, LANE), lambda i: (0, 0)),
                  pl.BlockSpec((1, LANE), lambda i: (0, 0))],
        out_specs=pl.BlockSpec((tm, LANE), lambda i: (i, 0)),
        compiler_params=_cparams("parallel"),
        name="moe_router",
    )(x, wrh, wrl, br)

    tm = 1024
    return pl.pallas_call(
        _moe_dense_kernel,
        out_shape=jax.ShapeDtypeStruct((N, D), F32),
        grid=(N // tm, MOE_EXPERTS),
        in_specs=[pl.BlockSpec((tm, D), lambda i, e: (i, 0)),
                  pl.BlockSpec((tm, LANE), lambda i, e: (i, 0)),
                  pl.BlockSpec((1, D, MOE_FF), lambda i, e: (e, 0, 0)),
                  pl.BlockSpec((1, D, MOE_FF), lambda i, e: (e, 0, 0)),
                  pl.BlockSpec((1, MOE_FF, D), lambda i, e: (e, 0, 0)),
                  pl.BlockSpec((1, D), lambda i, e: (0, 0)),
                  pl.BlockSpec((1, D), lambda i, e: (0, 0))],
        out_specs=pl.BlockSpec((tm, D), lambda i, e: (i, 0)),
        scratch_shapes=[pltpu.VMEM((tm, D), F32)],
        compiler_params=_cparams("parallel", "arbitrary"),
        name="moe_experts_ln",
    )(x, gates, w1.astype(BF16), w3.astype(BF16), w2.astype(BF16), ln_g.reshape(1, D), ln_b.reshape(1, D))


def kernel(x, mla_w_in, mla_q_norm, mla_kv_norm, mla_w_qb, mla_w_kvb, mla_w_o, nsa_w_in, nsa_w_phi_k, nsa_w_phi_v, nsa_cmp_pos, nsa_w_o, diff_w_in, diff_lam_q1, diff_lam_k1, diff_lam_q2, diff_lam_k2, diff_subln, diff_w_o, dil_w_in, dil_w_o, ln1_g, ln1_b, ln2_g, ln2_b, moe_wg, moe_bg, moe_we, moe_be, moe_w1, moe_w3, moe_w2):
    B, S, D = x.shape
    h = x.reshape(B * S, D)
    for i in range(DEPTH):
        m, j = i % 4, i // 4
        if m == 0:
            h = _mla_mixer(h, mla_w_in[j], mla_q_norm[j], mla_kv_norm[j], mla_w_qb[j], mla_w_kvb[j], mla_w_o[j],
                           ln1_g[i], ln1_b[i], B, S)
        elif m == 1:
            h = _nsa_mixer(h, nsa_w_in[j], nsa_w_phi_k[j], nsa_w_phi_v[j], nsa_cmp_pos[j], nsa_w_o[j],
                           ln1_g[i], ln1_b[i], B, S)
        elif m == 2:
            h = _diff_mixer(h, diff_w_in[j], diff_lam_q1[j], diff_lam_k1[j], diff_lam_q2[j], diff_lam_k2[j],
                            diff_subln[j], diff_w_o[j], i, ln1_g[i], ln1_b[i], B, S)
        else:
            h = _dil_mixer(h, dil_w_in[j], dil_w_o[j], ln1_g[i], ln1_b[i], B, S)
        h = _hier_moe_ln(h, moe_wg[i], moe_bg[i], moe_we[i], moe_be[i], moe_w1[i], moe_w3[i], moe_w2[i],
                         ln2_g[i], ln2_b[i])
    return h.reshape(B, S, D)
```

```python
import functools
import math

import numpy as np
import jax
import jax.numpy as jnp
from jax import lax
from jax.experimental import pallas as pl
from jax.experimental.pallas import tpu as pltpu

F32 = jnp.float32
BF16 = jnp.bfloat16

DEPTH = 4
DN_ALPHA = (2.0 * DEPTH) ** 0.25
LN_EPS = 1e-5
NEG_INF = -1e30
LOG2E = math.log2(math.e)
LANE = 128
VMEM_LIMIT = 56 * 1024 * 1024
ATTN_TQ, ATTN_TK = 256, 512
MOE_TM = 512

MLA_HEADS, MLA_Q_RANK, MLA_KV_RANK, MLA_NOPE, MLA_ROPE, MLA_V = 16, 384, 256, 64, 32, 64
ROPE_THETA = 10000.0
NSA_HEADS, NSA_GROUPS, NSA_HD = 16, 4, 64
NSA_CMP_LEN, NSA_CMP_STRIDE, NSA_SLC_LEN, NSA_TOP_N, NSA_WINDOW, NSA_FORCE = 32, 16, 64, 8, 512, 1e4
DIFF_HEADS, DIFF_HD = 8, 64
DIL_PATTERNS = ((128, 1), (512, 4), (2048, 16))
DIL_HEADS, DIL_HD = 8, 64
MOE_GROUPS, MOE_EPG, MOE_EXPERTS, MOE_FF = 4, 4, 16, 512


def _cparams(*sem):
    return pltpu.CompilerParams(dimension_semantics=sem, vmem_limit_bytes=VMEM_LIMIT)


def _alibi(n):
    return np.asarray(2.0 ** (-8.0 * np.arange(1, n + 1) / n), np.float32)


def _linear_kernel(x_ref, w_ref, *out_refs, splits, chunk):
    xb = x_ref[...].astype(BF16)
    col = 0
    for o_ref, n in zip(out_refs, splits):
        for c0 in range(0, n, chunk):
            cw = min(chunk, n - c0)
            o_ref[:, c0:c0 + cw] = jnp.dot(
                xb, w_ref[:, col + c0:col + c0 + cw], preferred_element_type=F32).astype(o_ref.dtype)
        col += n


def _linear(x, w, splits, dtypes, tm=512, name="linear"):
    M, K = x.shape
    ntot = sum(splits)
    assert w.shape == (K, ntot) and M % tm == 0 and all(n % LANE == 0 for n in splits)
    outs = pl.pallas_call(
        functools.partial(_linear_kernel, splits=tuple(splits), chunk=512),
        out_shape=[jax.ShapeDtypeStruct((M, n), d) for n, d in zip(splits, dtypes)],
        grid=(M // tm,),
        in_specs=[pl.BlockSpec((tm, K), lambda i: (i, 0)),
                  pl.BlockSpec((K, ntot), lambda i: (0, 0))],
        out_specs=[pl.BlockSpec((tm, n), lambda i: (i, 0)) for n in splits],
        compiler_params=_cparams("parallel"),
        name=name,
    )(x, w)
    return outs


def _layer_norm_rows(z, g, b):
    mu = jnp.mean(z, axis=-1, keepdims=True)
    zc = z - mu
    var = jnp.mean(zc * zc, axis=-1, keepdims=True)
    return zc * lax.rsqrt(var + LN_EPS) * g + b


def _outproj_ln_kernel(o_ref, w_ref, x_ref, g_ref, b_ref, out_ref):
    y = jnp.dot(o_ref[...].astype(BF16), w_ref[...], preferred_element_type=F32)
    out_ref[...] = _layer_norm_rows(DN_ALPHA * x_ref[...] + y, g_ref[...], b_ref[...])


def _outproj_ln(o, w_o, x, g, b, tm=512, name="outproj_ln"):
    M, K = o.shape
    D = x.shape[1]
    return pl.pallas_call(
        _outproj_ln_kernel,
        out_shape=jax.ShapeDtypeStruct((M, D), F32),
        grid=(M // tm,),
        in_specs=[pl.BlockSpec((tm, K), lambda i: (i, 0)),
                  pl.BlockSpec((K, D), lambda i: (0, 0)),
                  pl.BlockSpec((tm, D), lambda i: (i, 0)),
                  pl.BlockSpec((1, D), lambda i: (0, 0)),
                  pl.BlockSpec((1, D), lambda i: (0, 0))],
        out_specs=pl.BlockSpec((tm, D), lambda i: (i, 0)),
        compiler_params=_cparams("parallel"),
        name=name,
    )(o, w_o.astype(BF16), x, g.reshape(1, D), b.reshape(1, D))


def _scores(q, k, bias):
    s = lax.dot_general(q, k, (((1,), (1,)), ((), ())), preferred_element_type=F32)
    return s if bias is None else s + bias


def _flash_scratch(n_streams, M, t):
    return [pltpu.VMEM((n_streams, 2, M, t), F32), pltpu.VMEM((n_streams, 2, M, t), BF16),
            pltpu.VMEM((n_streams, M, LANE), F32), pltpu.VMEM((n_streams, 3, M, LANE), F32)]


def _flash_causal(streams, n_int, diag_vis, scratch):
    s_ref, p_ref, acc_ref, st_ref = scratch
    reps = s_ref.shape[-1] // LANE

    def scores(n, c):
        q, k_at, _, bias_at = streams[n]
        return _scores(q, k_at(c), None if bias_at is None else bias_at(c))

    def lane_partial_sum(p):
        return functools.reduce(jnp.add, [p[:, r * LANE:(r + 1) * LANE] for r in range(reps)])

    def deferred_pv(n, c, slot):
        v = streams[n][2](jnp.maximum(c - 1, 0))
        acc_ref[n] = st_ref[n, 2] * acc_ref[n] + jnp.dot(p_ref[n, 1 - slot], v, preferred_element_type=F32)

    def softmax(n, s, slot, vis=None):
        if vis is not None:
            s = jnp.where(vis, s, NEG_INF)
        m_old = st_ref[n, 0]
        m_new = jnp.maximum(m_old, jnp.max(s, axis=-1, keepdims=True))
        alpha = jnp.exp2(m_old - m_new)
        p = jnp.exp2(s - jnp.concatenate([m_new] * reps, axis=1))
        st_ref[n, 0] = m_new
        st_ref[n, 1] = alpha * st_ref[n, 1] + lane_partial_sum(p)
        st_ref[n, 2] = alpha
        p_ref[n, slot] = p.astype(BF16)

    def stage(c, slot):
        for n in range(len(streams)):
            s_ref[n, 1 - slot] = scores(n, c + 1)
        for n in range(len(streams)):
            deferred_pv(n, c, slot)
        for n in range(len(streams)):
            softmax(n, s_ref[n, slot], slot)

    for n in range(len(streams)):
        M = s_ref.shape[2]
        st_ref[n, 0] = jnp.full((M, LANE), NEG_INF, F32)
        st_ref[n, 1] = jnp.zeros((M, LANE), F32)
        st_ref[n, 2] = jnp.ones((M, LANE), F32)
        acc_ref[n] = jnp.zeros((M, LANE), F32)
        p_ref[n, 1] = jnp.zeros(p_ref.shape[2:], BF16)
        s_ref[n, 0] = scores(n, 0)

    def pair(cc, carry):
        stage(2 * cc, 0)
        stage(2 * cc + 1, 1)
        return carry

    lax.fori_loop(0, n_int // 2, pair, 0)

    @pl.when(n_int % 2 == 1)
    def _():
        stage(n_int - 1, 0)

    slot = n_int % 2
    res = []
    for n in range(len(streams)):
        v_at = streams[n][2]
        acc = st_ref[n, 2] * acc_ref[n] + jnp.dot(p_ref[n, 1 - slot], v_at(jnp.maximum(n_int - 1, 0)),
                                                 preferred_element_type=F32)
        s = jnp.where(diag_vis, s_ref[n, slot], NEG_INF)
        m_old = st_ref[n, 0]
        m_new = jnp.maximum(m_old, jnp.max(s, axis=-1, keepdims=True))
        alpha = jnp.exp2(m_old - m_new)
        p = jnp.exp2(s - jnp.concatenate([m_new] * reps, axis=1))
        l = jnp.sum(alpha * st_ref[n, 1] + lane_partial_sum(p), axis=-1, keepdims=True)
        res.append((l, alpha * acc + jnp.dot(p.astype(BF16), v_at(n_int), preferred_element_type=F32)))
    return res


def _window_attention(q, ks, vs, viss, biases=None):
    ss = []
    for j, (k, vis) in enumerate(zip(ks, viss)):
        s = _scores(q, k, None if biases is None else biases[j])
        ss.append(s if vis is None else jnp.where(vis, s, NEG_INF))
    m = functools.reduce(jnp.maximum, [jnp.max(s, axis=-1, keepdims=True) for s in ss])
    ps = [jnp.exp2(s - m) for s in ss]
    l = functools.reduce(jnp.add, [jnp.sum(p, axis=-1, keepdims=True) for p in ps])
    acc = functools.reduce(jnp.add, [jnp.dot(p.astype(BF16), v, preferred_element_type=F32) for p, v in zip(ps, vs)])
    return m, l, acc


def _rel_pos(rows, cols):
    return (lax.broadcasted_iota(jnp.int32, (rows, cols), 1)
            - lax.broadcasted_iota(jnp.int32, (rows, cols), 0))


def _rms_rows(c, g, eps):
    return c * lax.rsqrt(jnp.mean(c * c, axis=-1, keepdims=True) + eps) * g


def _mla_proj_kernel(x_ref, win_ref, qn_ref, kvn_ref, wq_ref, wqs_ref, wk_ref, wv_ref, cos_ref, sin_ref,
                     q_out, k_out, v_out, *, scale):
    xb = x_ref[...].astype(BF16)
    c = jnp.dot(xb, win_ref[...], preferred_element_type=F32)
    r0, r1 = MLA_Q_RANK, MLA_Q_RANK + MLA_KV_RANK
    cq = _rms_rows(c[:, :r0], qn_ref[...], 1e-6).astype(BF16)
    ckv = _rms_rows(c[:, r0:r1], kvn_ref[...], 1e-6).astype(BF16)
    cos, sin = cos_ref[...], sin_ref[...]
    kr = c[:, r1:r1 + LANE] * cos + c[:, r1 + LANE:r1 + 2 * LANE] * sin
    v_out[...] = jnp.dot(ckv, wv_ref[...], preferred_element_type=F32).astype(v_out.dtype)
    for h in range(MLA_HEADS):
        sl = slice(h * LANE, (h + 1) * LANE)
        qh = jnp.dot(cq, wq_ref[:, sl], preferred_element_type=F32)
        qhs = jnp.dot(cq, wqs_ref[:, sl], preferred_element_type=F32)
        q_out[:, sl] = ((qh * cos + qhs * sin) * scale).astype(q_out.dtype)
        kh = jnp.dot(ckv, wk_ref[:, sl], preferred_element_type=F32)
        k_out[:, sl] = (kh + kr).astype(k_out.dtype)


def _mla_attn_kernel(q_ref, k_ref, v_ref, o_ref, *scratch, tq, tk):
    q0 = pl.program_id(2) * tq
    c_d = q0 // tk
    causal = _rel_pos(tq, tk) + (c_d * tk - q0) <= 0

    def chunk(c):
        return pl.ds(pl.multiple_of(c * tk, tk), tk)

    def v_at(c):
        return v_ref[0, chunk(c), :]

    streams = []
    for hh in range(2):
        sl = slice(hh * LANE, (hh + 1) * LANE)
        streams.append((q_ref[0, :, sl], lambda c, sl=sl: k_ref[0, chunk(c), sl], v_at, None))
    outs = [acc / l for l, acc in _flash_causal(streams, c_d, causal, scratch)]
    lane = lax.broadcasted_iota(jnp.int32, (tq, LANE), 1)
    o_ref[0] = jnp.where(lane < MLA_V, outs[0], outs[1]).astype(o_ref.dtype)


def _mla_mixer(x, w_in, q_norm, kv_norm, w_qb, w_kvb, w_o, ln_g, ln_b, B, S):
    N, D = x.shape
    H, dq = MLA_HEADS, MLA_NOPE + MLA_ROPE
    half = MLA_ROPE // 2
    r0, r1 = MLA_Q_RANK, MLA_Q_RANK + MLA_KV_RANK
    z = lambda *s: jnp.zeros(s, F32)
    swap = lambda a: jnp.concatenate([a[..., half:], a[..., :half]], axis=-1)
    kr_w = w_in[:, r1:]
    win = jnp.concatenate([w_in[:, :r1],
                           z(D, MLA_NOPE), kr_w, z(D, LANE - dq),
                           z(D, MLA_NOPE), swap(kr_w), z(D, LANE - dq)], axis=1).astype(BF16)
    wq3 = w_qb.reshape(r0, H, dq)
    wq = jnp.concatenate([wq3, z(r0, H, LANE - dq)], axis=-1).reshape(r0, H * LANE).astype(BF16)
    wqs = jnp.concatenate([z(r0, H, MLA_NOPE), swap(wq3[..., MLA_NOPE:]), z(r0, H, LANE - dq)],
                          axis=-1).reshape(r0, H * LANE).astype(BF16)
    wkv3 = w_kvb.reshape(MLA_KV_RANK, H, MLA_NOPE + MLA_V)
    wk = jnp.concatenate([wkv3[..., :MLA_NOPE], z(MLA_KV_RANK, H, LANE - MLA_NOPE)],
                         axis=-1).reshape(MLA_KV_RANK, H * LANE).astype(BF16)
    wv = wkv3[..., MLA_NOPE:].reshape(MLA_KV_RANK, H * MLA_V).astype(BF16)
    freq = ROPE_THETA ** (-jnp.arange(half, dtype=F32) / half)
    ang = jnp.arange(S, dtype=F32)[:, None] * freq
    cos, sin = jnp.cos(ang), jnp.sin(ang)
    ones, zer = jnp.ones((S, MLA_NOPE), F32), jnp.zeros((S, LANE - dq), F32)
    cos_t = jnp.concatenate([ones, cos, cos, zer], axis=1)
    sin_t = jnp.concatenate([0 * ones, -sin, sin, zer], axis=1)

    tm = 256
    nwin = win.shape[1]
    q, k, v = pl.pallas_call(
        functools.partial(_mla_proj_kernel, scale=dq ** -0.5 * LOG2E),
        out_shape=[jax.ShapeDtypeStruct((N, H * LANE), BF16),
                   jax.ShapeDtypeStruct((N, H * LANE), BF16),
                   jax.ShapeDtypeStruct((N, H * MLA_V), BF16)],
        grid=(N // tm,),
        in_specs=[pl.BlockSpec((tm, D), lambda i: (i, 0)),
                  pl.BlockSpec((D, nwin), lambda i: (0, 0)),
                  pl.BlockSpec((1, r0), lambda i: (0, 0)),
                  pl.BlockSpec((1, MLA_KV_RANK), lambda i: (0, 0)),
                  pl.BlockSpec((r0, H * LANE), lambda i: (0, 0)),
                  pl.BlockSpec((r0, H * LANE), lambda i: (0, 0)),
                  pl.BlockSpec((MLA_KV_RANK, H * LANE), lambda i: (0, 0)),
                  pl.BlockSpec((MLA_KV_RANK, H * MLA_V), lambda i: (0, 0)),
                  pl.BlockSpec((tm, LANE), lambda i: (i % (S // tm), 0)),
                  pl.BlockSpec((tm, LANE), lambda i: (i % (S // tm), 0))],
        out_specs=[pl.BlockSpec((tm, H * LANE), lambda i: (i, 0)),
                   pl.BlockSpec((tm, H * LANE), lambda i: (i, 0)),
                   pl.BlockSpec((tm, H * MLA_V), lambda i: (i, 0))],
        compiler_params=_cparams("parallel"),
        name="mla_proj",
    )(x, win, q_norm.reshape(1, r0), kv_norm.reshape(1, MLA_KV_RANK), wq, wqs, wk, wv, cos_t, sin_t)

    tq, tk = ATTN_TQ, ATTN_TK
    o = pl.pallas_call(
        functools.partial(_mla_attn_kernel, tq=tq, tk=tk),
        out_shape=jax.ShapeDtypeStruct((B, S, H * MLA_V), BF16),
        grid=(B, H // 2, S // tq),
        in_specs=[pl.BlockSpec((1, tq, 2 * LANE), lambda b, h, i: (b, i, h)),
                  pl.BlockSpec((1, S, 2 * LANE), lambda b, h, i: (b, 0, h)),
                  pl.BlockSpec((1, S, LANE), lambda b, h, i: (b, 0, h))],
        out_specs=pl.BlockSpec((1, tq, LANE), lambda b, h, i: (b, i, h)),
        scratch_shapes=_flash_scratch(2, tq, tk),
        compiler_params=_cparams("parallel", "parallel", "arbitrary"),
        name="mla_attn",
    )(q.reshape(B, S, H * LANE), k.reshape(B, S, H * LANE), v.reshape(B, S, H * MLA_V))
    return _outproj_ln(o.reshape(N, H * MLA_V), w_o, x, ln_g, ln_b, name="mla_out_ln")


def _diff_attn_kernel(slope_ref, lam_ref, sub_ref, q_ref, k_ref, v_ref, o_ref, *scratch, tq, tk, lam_init):
    h = pl.program_id(1)
    q0 = pl.program_id(2) * tq
    c_d = q0 // tk
    slope = slope_ref[h] * LOG2E
    lv = lam_ref[...]
    lam = (jnp.exp(jnp.sum(lv[0:1] * lv[1:2], axis=-1, keepdims=True))
           - jnp.exp(jnp.sum(lv[2:3] * lv[3:4], axis=-1, keepdims=True)) + lam_init)
    lane = lax.broadcasted_iota(jnp.int32, (tq, LANE), 1)
    qf = q_ref[0].astype(F32) * (DIFF_HD ** -0.5 * LOG2E)
    qs = [jnp.where(lane < DIFF_HD, qf, 0.0).astype(BF16), jnp.where(lane < DIFF_HD, 0.0, qf).astype(BF16)]
    col = lax.broadcasted_iota(jnp.int32, (1, tk), 1)

    def chunk(c):
        return pl.ds(pl.multiple_of(c * tk, tk), tk)

    def k_at(c):
        return k_ref[0, chunk(c), :]

    def v_at(c):
        return v_ref[0, chunk(c), :]

    def bias_at(c):
        return slope * (col + c * tk).astype(F32)

    streams = [(q, k_at, v_at, bias_at) for q in qs]
    causal = _rel_pos(tq, tk) + (c_d * tk - q0) <= 0
    outs = [acc / l for l, acc in _flash_causal(streams, c_d, causal, scratch)]
    a = outs[0] - lam * outs[1]
    a = _rms_rows(a, sub_ref[...], 1e-5) * (1.0 - lam_init)
    o_ref[0] = a.astype(o_ref.dtype)


def _diff_mixer(x, w_in, lam_q1, lam_k1, lam_q2, lam_k2, subln, w_o, layer_idx, ln_g, ln_b, B, S):
    N, D = x.shape
    H, d = DIFF_HEADS, DIFF_HD
    nq = H * 2 * d
    (qkv,) = _linear(x, w_in.astype(BF16), [3 * nq], [BF16], name="diff_proj")
    qkv = qkv.reshape(B, S, 3 * nq)
    lam_init = 0.8 - 0.6 * math.exp(-0.3 * layer_idx)
    lamv = jnp.zeros((8, LANE), F32).at[:4, :d].set(jnp.stack([lam_q1, lam_k1, lam_q2, lam_k2]).astype(F32))
    tq, tk = ATTN_TQ, ATTN_TK
    o = pl.pallas_call(
        functools.partial(_diff_attn_kernel, tq=tq, tk=tk, lam_init=lam_init),
        out_shape=jax.ShapeDtypeStruct((B, S, nq), BF16),
        grid=(B, H, S // tq),
        in_specs=[pl.BlockSpec(memory_space=pltpu.SMEM),
                  pl.BlockSpec((8, LANE), lambda b, h, i: (0, 0)),
                  pl.BlockSpec((1, 2 * d), lambda b, h, i: (0, 0)),
                  pl.BlockSpec((1, tq, LANE), lambda b, h, i: (b, i, h)),
                  pl.BlockSpec((1, S, LANE), lambda b, h, i: (b, 0, H + h)),
                  pl.BlockSpec((1, S, LANE), lambda b, h, i: (b, 0, 2 * H + h))],
        out_specs=pl.BlockSpec((1, tq, LANE), lambda b, h, i: (b, i, h)),
        scratch_shapes=_flash_scratch(2, tq, tk),
        compiler_params=_cparams("parallel", "parallel", "arbitrary"),
        name="diff_attn",
    )(jnp.asarray(_alibi(H)), lamv, subln.reshape(1, 2 * d).astype(F32), qkv, qkv, qkv)
    return _outproj_ln(o.reshape(N, nq), w_o, x, ln_g, ln_b, name="diff_out_ln")


SEL_LANE0 = NSA_HD
SEL_OFF = 1e30
ALIBI_LANE0 = 96
POS_SPLIT = 64


def _nsa_compress_kernel(ak_ref, av_ref, plo_ref, phi_ref, wklo_ref, wkhi_ref, wvlo_ref, wvhi_ref, kc_out, vc_out):
    def one(a_ref, wlo_ref, whi_ref, out):
        a = a_ref[0]
        lo = jnp.dot((a + plo_ref[...]).astype(BF16), wlo_ref[...], preferred_element_type=F32)
        hi = jnp.dot((a + phi_ref[...]).astype(BF16), whi_ref[...], preferred_element_type=F32)
        out[0] = (lo + pltpu.roll(hi, hi.shape[0] - 1, 0)).astype(out.dtype)

    one(ak_ref, wklo_ref, wkhi_ref, kc_out)
    one(av_ref, wvlo_ref, wvhi_ref, vc_out)


def _pack_heads(o, tq):
    lane = lax.broadcasted_iota(jnp.int32, (tq, LANE), 1)
    p01 = jnp.where(lane < NSA_HD, o[0:tq], o[tq:2 * tq])
    p23 = jnp.where(lane < NSA_HD, o[2 * tq:3 * tq], o[3 * tq:4 * tq])
    return jnp.concatenate([p01, p23], axis=1)


def _nsa_attn_kernel(q_ref, qc_ref, gl_ref, kc_ref, vc_ref, ks_ref, vs_ref, kw_ref, vw_ref, ktab_ref, ctab_ref,
                     c2s_ref, rep_ref, o_ref, *scratch, tq, tk, tks, n_slc):
    i = pl.program_id(2)
    t0 = i * tq
    hpg = NSA_HEADS // NSA_GROUPS
    M = hpg * tq
    q = jnp.concatenate([q_ref[0, :, j * LANE:(j + 1) * LANE]
                         + jnp.broadcast_to(qc_ref[0, j:j + 1, :], (tq, LANE)).astype(BF16)
                         for j in range(hpg)], axis=0)
    trow1 = t0 + lax.broadcasted_iota(jnp.int32, (tq, 1), 0)
    trow = jnp.concatenate([trow1] * hpg, axis=0)
    lane_m = lax.broadcasted_iota(jnp.int32, (M, LANE), 1)

    sc = lax.dot_general(q, kc_ref[0] + ctab_ref[...], (((1,), (1,)), ((), ())), preferred_element_type=F32)
    sc = jnp.where(trow >= lane_m * NSA_CMP_STRIDE + (NSA_CMP_LEN - 1), sc, NEG_INF)
    e = jnp.exp2(sc - jnp.max(sc, axis=-1, keepdims=True))
    p_c = jnp.where(trow >= NSA_CMP_LEN - 1, e / jnp.sum(e, axis=-1, keepdims=True), 0.0)
    o_c = jnp.dot(p_c.astype(BF16), vc_ref[0], preferred_element_type=F32)
    psum = p_c[0:tq]
    for j in range(1, hpg):
        psum = psum + p_c[j * tq:(j + 1) * tq]
    ph, plw = _split_bf16(psum)
    imp = (jnp.dot(ph, c2s_ref[...], preferred_element_type=F32)
           + jnp.dot(plw, c2s_ref[...], preferred_element_type=F32))

    lane = lax.broadcasted_iota(jnp.int32, (tq, LANE), 1)
    sidx = lane - SEL_LANE0
    valid = (sidx >= 0) & (sidx < n_slc)
    forced = (sidx == 0) | (sidx == trow1 // NSA_SLC_LEN)
    future = sidx * NSA_SLC_LEN > trow1
    score = imp + jnp.where(forced, NSA_FORCE, 0.0) - jnp.where(future, 2.0 * NSA_FORCE, 0.0)
    score = jnp.where(valid, score, -jnp.inf)
    n_rows = ALIBI_LANE0 - SEL_LANE0
    st = score.T[SEL_LANE0:ALIBI_LANE0]
    blk = lax.broadcasted_iota(jnp.int32, (n_rows, tq), 0)
    sel = jnp.zeros((n_rows, tq), jnp.bool_)
    for _ in range(min(NSA_TOP_N, n_slc)):
        mx = jnp.max(st, axis=0, keepdims=True)
        pick = blk == jnp.min(jnp.where(st == mx, blk, n_rows), axis=0, keepdims=True)
        sel = sel | pick
        st = jnp.where(pick, -jnp.inf, st)
    off = jnp.where(sel | (blk >= n_slc), 0.0, -SEL_OFF)
    selbias = jnp.concatenate([jnp.zeros((SEL_LANE0, tq), F32), off, jnp.zeros((LANE - ALIBI_LANE0, tq), F32)],
                              axis=0).T.astype(BF16)

    q_s = q + jnp.concatenate([selbias] * hpg, axis=0)
    half = M // 2
    c_s = t0 // tks
    causal_s = jnp.concatenate([_rel_pos(tq, tks)] * (hpg // 2), axis=0) + (c_s * tks - t0) <= 0

    def chunk_s(c):
        return pl.ds(pl.multiple_of(c * tks, tks), tks)

    def ks_at(c):
        return ks_ref[0, chunk_s(c), :] + ktab_ref[chunk_s(c), :]

    def vs_at(c):
        return vs_ref[0, chunk_s(c), :]

    res = _flash_causal([(q_s[:half], ks_at, vs_at, None), (q_s[half:], ks_at, vs_at, None)], c_s, causal_s, scratch)
    o_s = jnp.concatenate([acc / l for l, acc in res], axis=0)

    c_d = t0 // tk
    d_diag = jnp.concatenate([_rel_pos(tq, tk)] * hpg, axis=0) + (c_d * tk - t0)

    def chunk(c):
        return pl.ds(pl.multiple_of(c * tk, tk), tk)

    cw = [jnp.maximum(c_d - 2, 0), jnp.maximum(c_d - 1, 0), c_d]
    vis_w = [(d_diag - 2 * tk > -NSA_WINDOW) & (c_d >= 2), jnp.broadcast_to(c_d >= 1, d_diag.shape), d_diag <= 0]
    _, l_w, acc_w = _window_attention(q, [kw_ref[0, chunk(c), :] + ktab_ref[chunk(c), :] for c in cw],
                                      [vw_ref[0, chunk(c), :] for c in cw], vis_w)
    o_w = acc_w / l_w

    gh, glw = _split_bf16(jax.nn.sigmoid(gl_ref[0]))
    gr = jnp.dot(gh, rep_ref[0], preferred_element_type=F32) + jnp.dot(glw, rep_ref[0], preferred_element_type=F32)
    w = hpg * NSA_HD
    o = (gr[:, 0:w] * _pack_heads(o_c, tq) + gr[:, w:2 * w] * _pack_heads(o_s, tq)
         + gr[:, 2 * w:3 * w] * _pack_heads(o_w, tq))
    o_ref[0] = o.astype(o_ref.dtype)


def _nsa_mixer(x, w_in, w_phi_k, w_phi_v, cmp_pos, w_o, ln_g, ln_b, B, S):
    N, D = x.shape
    H, G, d = NSA_HEADS, NSA_GROUPS, NSA_HD
    hpg = H // G
    L, st = NSA_CMP_LEN, NSA_CMP_STRIDE
    n_slc = S // NSA_SLC_LEN
    assert S % 256 == 0 and n_slc <= ALIBI_LANE0 - SEL_LANE0 and L == 2 * st
    cuts = [H * d + i * G * d for i in range(7)]
    wq, wkc, wvc, wks, wvs, wkw, wvw, wgl = jnp.split(w_in, cuts, axis=1)
    z = lambda *s: jnp.zeros(s, F32)
    pad_heads = lambda w, n: jnp.concatenate([w.reshape(D, n, d), z(D, n, LANE - d)], axis=-1).reshape(D, n * LANE)
    dup_heads = lambda w, n: jnp.concatenate([w.reshape(D, n, d)] * 2, axis=-1).reshape(D, n * LANE)
    wgl_p = jnp.concatenate([wgl, z(D, LANE - wgl.shape[1])], axis=1)
    wcat = jnp.concatenate([pad_heads(wq * (d ** -0.5 * LOG2E), H), wkc, wvc, pad_heads(wks, G), dup_heads(wvs, G),
                            pad_heads(wkw, G), dup_heads(wvw, G), wgl_p], axis=1).astype(BF16)
    q, kc, vc, ks, vs, kw, vw, gl = _linear(
        x, wcat, [H * LANE, G * d, G * d, G * LANE, G * LANE, G * LANE, G * LANE, LANE],
        [BF16, F32, F32, BF16, BF16, BF16, BF16, F32], name="nsa_proj")

    nrow = S // st
    eye = jnp.eye(G, dtype=F32)

    def phi_w(w_phi, half, dup):
        w = w_phi.reshape(L, d, d)[half * st:(half + 1) * st]
        wd = jnp.concatenate([w, w if dup else jnp.zeros_like(w)], axis=-1)
        return jnp.einsum('ldc,gh->lgdhc', wd, eye).reshape(st * G * d, G * LANE).astype(BF16)

    pos = lambda half: jnp.broadcast_to(cmp_pos[half * st:(half + 1) * st, None, :], (st, G, d)).reshape(1, st * G * d)
    wide = st * G * d
    cspec = pl.BlockSpec((1, nrow, wide), lambda b: (b, 0, 0))
    wspec = pl.BlockSpec((wide, G * LANE), lambda b: (0, 0))
    pspec = pl.BlockSpec((1, wide), lambda b: (0, 0))
    ospec = pl.BlockSpec((1, nrow, G * LANE), lambda b: (b, 0, 0))
    kcmp, vcmp = pl.pallas_call(
        _nsa_compress_kernel,
        out_shape=[jax.ShapeDtypeStruct((B, nrow, G * LANE), BF16)] * 2,
        grid=(B,),
        in_specs=[cspec, cspec, pspec, pspec, wspec, wspec, wspec, wspec],
        out_specs=[ospec, ospec],
        compiler_params=_cparams("parallel"),
        name="nsa_compress",
    )(kc.reshape(B, nrow, wide), vc.reshape(B, nrow, wide), pos(0), pos(1),
      phi_w(w_phi_k, 0, False), phi_w(w_phi_k, 1, False), phi_w(w_phi_v, 0, True), phi_w(w_phi_v, 1, True))

    def pos_lanes(tab, pos):
        tab[:, ALIBI_LANE0:ALIBI_LANE0 + 3] = (pos // POS_SPLIT)[:, None]
        tab[:, ALIBI_LANE0 + 3:ALIBI_LANE0 + 6] = (pos % POS_SPLIT)[:, None]
        return tab

    ktab = np.zeros((S, LANE), np.float32)
    ktab[np.arange(S), SEL_LANE0 + np.arange(S) // NSA_SLC_LEN] = 1.0
    ktab = pos_lanes(ktab, np.arange(S))
    ctab = pos_lanes(np.zeros((nrow, LANE), np.float32), np.arange(nrow) * st + (L - 1))
    s2 = jnp.asarray(_alibi(H) * LOG2E, F32)
    parts = []
    for _ in range(3):
        part = s2.astype(BF16).astype(F32)
        parts.append(part)
        s2 = s2 - part
    qcoef = jnp.zeros((H, LANE), F32).at[:, ALIBI_LANE0:ALIBI_LANE0 + 6].set(
        jnp.stack([POS_SPLIT * p for p in parts] + parts, axis=1)).reshape(G, hpg, LANE)
    cmp_start = np.arange(nrow) * st
    slc_start = np.arange(n_slc) * NSA_SLC_LEN
    ov = (cmp_start[:, None] < slc_start[None, :] + NSA_SLC_LEN) & (cmp_start[:, None] + L > slc_start[None, :])
    ov[(S - L) // st + 1:] = False
    c2s = np.zeros((nrow, LANE), np.float32)
    c2s[:, SEL_LANE0:SEL_LANE0 + n_slc] = ov
    rep = np.zeros((G, LANE, 3 * hpg * d), np.float32)
    for g in range(G):
        for j in range(hpg):
            for br in range(3):
                rep[g, (g * hpg + j) * 3 + br, br * hpg * d + j * d:br * hpg * d + (j + 1) * d] = 1.0

    tq, tk, tks = 128, 256, ATTN_TK
    assert NSA_WINDOW == 2 * tk and tk % tq == 0
    kvspec = pl.BlockSpec((1, S, LANE), lambda b, g, i: (b, 0, g))
    cmpspec = pl.BlockSpec((1, nrow, LANE), lambda b, g, i: (b, 0, g))
    o = pl.pallas_call(
        functools.partial(_nsa_attn_kernel, tq=tq, tk=tk, tks=tks, n_slc=n_slc),
        out_shape=jax.ShapeDtypeStruct((B, S, H * d), BF16),
        grid=(B, G, S // tq),
        in_specs=[pl.BlockSpec((1, tq, hpg * LANE), lambda b, g, i: (b, i, g)),
                  pl.BlockSpec((1, hpg, LANE), lambda b, g, i: (g, 0, 0)),
                  pl.BlockSpec((1, tq, LANE), lambda b, g, i: (b, i, 0)),
                  cmpspec, cmpspec, kvspec, kvspec, kvspec, kvspec,
                  pl.BlockSpec((S, LANE), lambda b, g, i: (0, 0)),
                  pl.BlockSpec((nrow, LANE), lambda b, g, i: (0, 0)),
                  pl.BlockSpec((nrow, LANE), lambda b, g, i: (0, 0)),
                  pl.BlockSpec((1, LANE, 3 * hpg * d), lambda b, g, i: (g, 0, 0))],
        out_specs=pl.BlockSpec((1, tq, hpg * d), lambda b, g, i: (b, i, g)),
        scratch_shapes=_flash_scratch(2, hpg * tq // 2, tks),
        compiler_params=_cparams("parallel", "parallel", "arbitrary"),
        name="nsa_attn",
    )(q.reshape(B, S, H * LANE), qcoef, gl.reshape(B, S, LANE), kcmp, vcmp,
      ks.reshape(B, S, G * LANE), vs.reshape(B, S, G * LANE), kw.reshape(B, S, G * LANE), vw.reshape(B, S, G * LANE),
      jnp.asarray(ktab, BF16), jnp.asarray(ctab, BF16), jnp.asarray(c2s, BF16), jnp.asarray(rep, BF16))
    return _outproj_ln(o.reshape(N, H * d), w_o, x, ln_g, ln_b, name="nsa_out_ln")


def _dil_attn_kernel(slope_ref, q_ref, k_ref, v_ref, o_ref, lse_ref, *, t, dil):
    i = pl.program_id(2)
    d = DIL_HD
    rel = _rel_pos(t, t)
    relf = rel.astype(F32)
    rel2 = jnp.concatenate([rel, rel], axis=0)
    lane = lax.broadcasted_iota(jnp.int32, (t, LANE), 1)
    scale = d ** -0.5 * LOG2E

    def chunk(c):
        return pl.ds(pl.multiple_of(c * t, t), t)

    cs = [jnp.maximum(i - 1, 0), i]
    viss = [(rel2 >= 0) & (i >= 1), rel2 <= 0]
    lse_t = jnp.zeros((t, LANE), F32)
    for hp in range(DIL_HEADS // 2):
        sl = slice(hp * LANE, (hp + 1) * LANE)
        qf = q_ref[0, :, sl].astype(F32) * scale
        q2 = jnp.concatenate([jnp.where(lane < d, qf, 0.0), jnp.where(lane < d, 0.0, qf)], axis=0).astype(BF16)
        sa, sb = slope_ref[2 * hp] * (dil * LOG2E), slope_ref[2 * hp + 1] * (dil * LOG2E)
        bias = jnp.concatenate([sa * relf, sb * relf], axis=0)
        shift = jnp.concatenate([jnp.full((t, 1), sa * t, F32), jnp.full((t, 1), sb * t, F32)], axis=0)
        m, l, acc = _window_attention(q2, [k_ref[0, chunk(c), sl] for c in cs], [v_ref[0, chunk(c), sl] for c in cs],
                                      viss, [bias - shift, bias])
        o = acc / l
        o_ref[0, :, hp * LANE:(hp + 1) * LANE] = jnp.where(lane < d, o[:t], o[t:]).astype(o_ref.dtype)
        lse = m + jnp.log2(l)
        lse_t = jnp.where(lane == 2 * hp, lse[:t], lse_t)
        lse_t = jnp.where(lane == 2 * hp + 1, lse[t:], lse_t)
    lse_ref[0] = lse_t


def _dil_merge_ln_kernel(o0_ref, o1_ref, o2_ref, l0_ref, l1_ref, l2_ref, rep_ref, w_ref, x_ref, g_ref, b_ref, out_ref):
    ls = [l0_ref[...], l1_ref[...], l2_ref[...]]
    mx = jnp.maximum(jnp.maximum(ls[0], ls[1]), ls[2])
    es = [jnp.exp2(v - mx) for v in ls]
    tot = es[0] + es[1] + es[2]
    o = None
    for e, o_ref in zip(es, (o0_ref, o1_ref, o2_ref)):
        wh, wl = _split_bf16(e / tot)
        wrep = jnp.dot(wh, rep_ref[...], preferred_element_type=F32) + jnp.dot(wl, rep_ref[...], preferred_element_type=F32)
        term = wrep * o_ref[...].astype(F32)
        o = term if o is None else o + term
    y = jnp.dot(o.astype(BF16), w_ref[...], preferred_element_type=F32)
    out_ref[...] = _layer_norm_rows(DN_ALPHA * x_ref[...] + y, g_ref[...], b_ref[...])


def _dil_mixer(x, w_in, w_o, ln_g, ln_b, B, S):
    N, D = x.shape
    H, d = DIL_HEADS, DIL_HD
    n_pat = len(DIL_PATTERNS)
    hd = H * d
    ncol = n_pat * 3 * hd
    (qkv,) = _linear(x, w_in.astype(BF16), [ncol], [BF16], name="dil_proj")
    tq = 128
    slopes = jnp.asarray(_alibi(H))
    outs, lses = [], []
    for p, (win, dil) in enumerate(DIL_PATTERNS):
        ls = S // dil
        assert ls % tq == 0 and win == dil * tq
        nblk = ncol // hd
        view = qkv.reshape(B, ls, dil * ncol)
        o, lse = pl.pallas_call(
            functools.partial(_dil_attn_kernel, t=tq, dil=dil),
            out_shape=[jax.ShapeDtypeStruct((B, ls, dil * hd), BF16),
                       jax.ShapeDtypeStruct((B, ls, dil * LANE), F32)],
            grid=(B, dil, ls // tq),
            in_specs=[pl.BlockSpec(memory_space=pltpu.SMEM),
                      pl.BlockSpec((1, tq, hd), lambda b, r, i, p=p, nblk=nblk: (b, i, r * nblk + 3 * p)),
                      pl.BlockSpec((1, ls, hd), lambda b, r, i, p=p, nblk=nblk: (b, 0, r * nblk + 3 * p + 1)),
                      pl.BlockSpec((1, ls, hd), lambda b, r, i, p=p, nblk=nblk: (b, 0, r * nblk + 3 * p + 2))],
            out_specs=[pl.BlockSpec((1, tq, hd), lambda b, r, i: (b, i, r)),
                       pl.BlockSpec((1, tq, LANE), lambda b, r, i: (b, i, r))],
            compiler_params=_cparams("parallel", "parallel", "arbitrary"),
            name=f"dil_attn_{p}",
        )(slopes, view, view, view)
        outs.append(o.reshape(N, hd))
        lses.append(lse.reshape(N, LANE))

    rep = np.zeros((LANE, hd), np.float32)
    for h in range(H):
        rep[h, h * d:(h + 1) * d] = 1.0
    tm = 512
    row = lambda n: pl.BlockSpec((tm, n), lambda i: (i, 0))
    full = lambda a, b: pl.BlockSpec((a, b), lambda i: (0, 0))
    return pl.pallas_call(
        _dil_merge_ln_kernel,
        out_shape=jax.ShapeDtypeStruct((N, D), F32),
        grid=(N // tm,),
        in_specs=[row(hd), row(hd), row(hd), row(LANE), row(LANE), row(LANE), full(LANE, hd), full(hd, D),
                  row(D), full(1, D), full(1, D)],
        out_specs=row(D),
        compiler_params=_cparams("parallel"),
        name="dil_merge_out_ln",
    )(*outs, *lses, jnp.asarray(rep, BF16), w_o.astype(BF16), x, ln_g.reshape(1, D), ln_b.reshape(1, D))


def _split_bf16(a):
    hi = a.astype(BF16)
    return hi, (a - hi.astype(F32)).astype(BF16)


META_GRP, META_RANK = MOE_EPG, MOE_EPG + 1


def _router_kernel(x_ref, wh_ref, wl_ref, b_ref, tri_ref, meta_ref, cnt_ref, run_ref):
    @pl.when(pl.program_id(0) == 0)
    def _():
        run_ref[...] = jnp.zeros_like(run_ref)

    xh, xl = _split_bf16(x_ref[...])
    wh, wl = wh_ref[...], wl_ref[...]
    logits = (jnp.dot(xh, wh, preferred_element_type=F32) + jnp.dot(xl, wh, preferred_element_type=F32)
              + jnp.dot(xh, wl, preferred_element_type=F32)) + b_ref[...]
    tm = logits.shape[0]
    lane = lax.broadcasted_iota(jnp.int32, (tm, LANE), 1)
    big = jnp.int32(LANE)
    lg = jnp.where(lane < MOE_GROUPS, logits, -jnp.inf)
    mg = jnp.max(lg, axis=-1, keepdims=True)
    sg = jnp.sum(jnp.exp(lg - mg), axis=-1, keepdims=True)
    pg_top = 1.0 / sg
    g_top = jnp.min(jnp.where(lg == mg, lane, big), axis=-1, keepdims=True)
    e_lo = MOE_GROUPS + MOE_EPG * g_top
    in_grp = (lane >= e_lo) & (lane < e_lo + MOE_EPG)
    le = jnp.where(in_grp, logits, -jnp.inf)
    me = jnp.max(le, axis=-1, keepdims=True)
    ee = jnp.exp(le - me)
    se = jnp.sum(ee, axis=-1, keepdims=True)
    pe = jnp.where(in_grp, ee / se, -1.0)
    v1 = jnp.max(pe, axis=-1, keepdims=True)
    i1 = jnp.min(jnp.where(pe == v1, lane, big), axis=-1, keepdims=True)
    pe2 = jnp.where(lane == i1, -1.0, pe)
    v2 = jnp.max(pe2, axis=-1, keepdims=True)
    i2 = jnp.min(jnp.where(pe2 == v2, lane, big), axis=-1, keepdims=True)
    tot = v1 + v2
    gates = (jnp.where(lane == i1 - e_lo, (v1 / tot) * pg_top, 0.0)
             + jnp.where(lane == i2 - e_lo, (v2 / tot) * pg_top, 0.0))
    onehot = lane == g_top
    prefix = jnp.dot(tri_ref[...], jnp.where(onehot, 1.0, 0.0).astype(BF16), preferred_element_type=F32)
    rank = jnp.sum(jnp.where(onehot, prefix + run_ref[...] - 1.0, 0.0), axis=-1, keepdims=True)
    meta_ref[...] = (gates + jnp.where(lane == META_GRP, g_top.astype(F32), 0.0)
                     + jnp.where(lane == META_RANK, rank, 0.0))
    run_ref[...] += prefix[tm - 1:tm, :]
    cnt_ref[...] = run_ref[...]


def _moe_dispatch_kernel(pos_ref, x_ref, meta_ref, zeros_hbm, xs_hbm, buf, sem):
    del zeros_hbm
    tm, d = x_ref.shape
    buf[:, :d] = x_ref[...]
    buf[:, d:] = meta_ref[...]

    def issue(r, carry):
        pltpu.make_async_copy(buf.at[pl.ds(r, 1)], xs_hbm.at[pl.ds(pos_ref[0, 0, r], 1)], sem).start()
        return carry

    lax.fori_loop(0, tm, issue, 0)
    pltpu.make_async_copy(buf, xs_hbm.at[pl.ds(0, tm)], sem).wait()


def _moe_group_kernel(tile_grp_ref, xs_ref, w1_ref, w3_ref, w2_ref, g_ref, b_ref, ys_ref):
    del tile_grp_ref
    d = ys_ref.shape[1]
    x = xs_ref[:, :d]
    gates = xs_ref[:, d:]
    lane = lax.broadcasted_iota(jnp.int32, gates.shape, 1)
    xb = x.astype(BF16)
    y = None
    for e in range(MOE_EPG):
        h1 = jnp.dot(xb, w1_ref[0, e], preferred_element_type=F32)
        h3 = jnp.dot(xb, w3_ref[0, e], preferred_element_type=F32)
        hid = (h1 * jax.nn.sigmoid(h1) * h3).astype(BF16)
        ge = jnp.sum(jnp.where(lane == e, gates, 0.0), axis=-1, keepdims=True)
        term = ge * jnp.dot(hid, w2_ref[0, e], preferred_element_type=F32)
        y = term if y is None else y + term
    ys_ref[...] = _layer_norm_rows(DN_ALPHA * x + y, g_ref[...], b_ref[...])


def _moe_collect_kernel(pos_ref, ys_hbm, out_ref, sem):
    tm = out_ref.shape[0]

    def issue(r, carry):
        pltpu.make_async_copy(ys_hbm.at[pl.ds(pos_ref[0, 0, r], 1)], out_ref.at[pl.ds(r, 1)], sem).start()
        return carry

    lax.fori_loop(0, tm, issue, 0)
    pltpu.make_async_copy(ys_hbm.at[pl.ds(0, tm)], out_ref, sem).wait()


def _hier_moe_ln(x, wg, bg, we, be, w1, w3, w2, ln_g, ln_b):
    N, D = x.shape
    G, E, FF = MOE_GROUPS, MOE_EPG, MOE_FF
    wr = jnp.concatenate([wg, jnp.moveaxis(we, 0, 1).reshape(D, MOE_EXPERTS),
                          jnp.zeros((D, LANE - G - MOE_EXPERTS), F32)], axis=1)
    br = jnp.concatenate([bg, be.reshape(-1), jnp.zeros((LANE - G - MOE_EXPERTS,), F32)]).reshape(1, LANE)
    wrh, wrl = _split_bf16(wr)
    tm = MOE_TM
    tri = jnp.asarray(np.tril(np.ones((tm, tm), np.float32)), BF16)
    meta, cnt = pl.pallas_call(
        _router_kernel,
        out_shape=[jax.ShapeDtypeStruct((N, LANE), F32), jax.ShapeDtypeStruct((1, LANE), F32)],
        grid=(N // tm,),
        in_specs=[pl.BlockSpec((tm, D), lambda i: (i, 0)),
                  pl.BlockSpec((D, LANE), lambda i: (0, 0)),
                  pl.BlockSpec((D, LANE), lambda i: (0, 0)),
                  pl.BlockSpec((1, LANE), lambda i: (0, 0)),
                  pl.BlockSpec((tm, tm), lambda i: (0, 0))],
        out_specs=[pl.BlockSpec((tm, LANE), lambda i: (i, 0)), pl.BlockSpec((1, LANE), lambda i: (0, 0))],
        scratch_shapes=[pltpu.VMEM((1, LANE), F32)],
        compiler_params=_cparams("arbitrary"),
        name="moe_router",
    )(x, wrh, wrl, br, tri)

    counts = cnt[0, :G].astype(jnp.int32)
    padded = (counts + tm - 1) // tm * tm
    ends = jnp.cumsum(padded)
    starts = ends - padded
    pos = (starts[meta[:, META_GRP].astype(jnp.int32)] + meta[:, META_RANK].astype(jnp.int32)).reshape(N // tm, 1, tm)
    n_pad = N + G * tm
    n_tiles = n_pad // tm
    tile_grp = jnp.minimum(jnp.searchsorted(ends, jnp.arange(n_tiles, dtype=jnp.int32) * tm, side="right"),
                           G - 1).astype(jnp.int32)

    pos_spec = pl.BlockSpec((1, 1, tm), lambda i: (i, 0, 0), memory_space=pltpu.SMEM)
    xs = pl.pallas_call(
        _moe_dispatch_kernel,
        out_shape=jax.ShapeDtypeStruct((n_pad, D + LANE), F32),
        grid=(N // tm,),
        in_specs=[pos_spec,
                  pl.BlockSpec((tm, D), lambda i: (i, 0)),
                  pl.BlockSpec((tm, LANE), lambda i: (i, 0)),
                  pl.BlockSpec(memory_space=pl.ANY)],
        out_specs=pl.BlockSpec(memory_space=pl.ANY),
        scratch_shapes=[pltpu.VMEM((tm, D + LANE), F32), pltpu.SemaphoreType.DMA],
        input_output_aliases={3: 0},
        compiler_params=_cparams("arbitrary"),
        name="moe_dispatch",
    )(pos, x, meta, jnp.zeros((n_pad, D + LANE), F32))

    wspec = lambda a, b: pl.BlockSpec((1, E, a, b), lambda t, tg: (tg[t], 0, 0, 0))
    ys = pl.pallas_call(
        _moe_group_kernel,
        out_shape=jax.ShapeDtypeStruct((n_pad, D), F32),
        grid_spec=pltpu.PrefetchScalarGridSpec(
            num_scalar_prefetch=1,
            grid=(n_tiles,),
            in_specs=[pl.BlockSpec((tm, D + LANE), lambda t, tg: (t, 0)),
                      wspec(D, FF), wspec(D, FF), wspec(FF, D),
                      pl.BlockSpec((1, D), lambda t, tg: (0, 0)),
                      pl.BlockSpec((1, D), lambda t, tg: (0, 0))],
            out_specs=pl.BlockSpec((tm, D), lambda t, tg: (t, 0))),
        compiler_params=_cparams("arbitrary"),
        name="moe_experts_ln",
    )(tile_grp, xs, w1.astype(BF16).reshape(G, E, D, FF), w3.astype(BF16).reshape(G, E, D, FF),
      w2.astype(BF16).reshape(G, E, FF, D), ln_g.reshape(1, D), ln_b.reshape(1, D))

    return pl.pallas_call(
        _moe_collect_kernel,
        out_shape=jax.ShapeDtypeStruct((N, D), F32),
        grid=(N // tm,),
        in_specs=[pos_spec, pl.BlockSpec(memory_space=pl.ANY)],
        out_specs=pl.BlockSpec((tm, D), lambda i: (i, 0)),
        scratch_shapes=[pltpu.SemaphoreType.DMA],
        compiler_params=_cparams("arbitrary"),
        name="moe_collect",
    )(pos, ys)


def kernel(x, mla_w_in, mla_q_norm, mla_kv_norm, mla_w_qb, mla_w_kvb, mla_w_o, nsa_w_in, nsa_w_phi_k, nsa_w_phi_v, nsa_cmp_pos, nsa_w_o, diff_w_in, diff_lam_q1, diff_lam_k1, diff_lam_q2, diff_lam_k2, diff_subln, diff_w_o, dil_w_in, dil_w_o, ln1_g, ln1_b, ln2_g, ln2_b, moe_wg, moe_bg, moe_we, moe_be, moe_w1, moe_w3, moe_w2):
    B, S, D = x.shape
    h = x.reshape(B * S, D)
    for i in range(DEPTH):
        m, j = i % 4, i // 4
        if m == 0:
            h = _mla_mixer(h, mla_w_in[j], mla_q_norm[j], mla_kv_norm[j], mla_w_qb[j], mla_w_kvb[j], mla_w_o[j],
                           ln1_g[i], ln1_b[i], B, S)
        elif m == 1:
            h = _nsa_mixer(h, nsa_w_in[j], nsa_w_phi_k[j], nsa_w_phi_v[j], nsa_cmp_pos[j], nsa_w_o[j],
                           ln1_g[i], ln1_b[i], B, S)
        elif m == 2:
            h = _diff_mixer(h, diff_w_in[j], diff_lam_q1[j], diff_lam_k1[j], diff_lam_q2[j], diff_lam_k2[j],
                            diff_subln[j], diff_w_o[j], i, ln1_g[i], ln1_b[i], B, S)
        else:
            h = _dil_mixer(h, dil_w_in[j], dil_w_o[j], ln1_g[i], ln1_b[i], B, S)
        h = _hier_moe_ln(h, moe_wg[i], moe_bg[i], moe_we[i], moe_be[i], moe_w1[i], moe_w3[i], moe_w2[i],
                         ln2_g[i], ln2_b[i])
    return h.reshape(B, S, D)
```

```python
import functools
import math

import numpy as np
import jax
import jax.numpy as jnp
from jax import lax
from jax.experimental import pallas as pl
from jax.experimental.pallas import tpu as pltpu

F32 = jnp.float32
BF16 = jnp.bfloat16

DEPTH = 4
DN_ALPHA = (2.0 * DEPTH) ** 0.25
LN_EPS = 1e-5
NEG_INF = -1e30
LOG2E = math.log2(math.e)
LANE = 128
VMEM_LIMIT = 56 * 1024 * 1024
ATTN_TQ, ATTN_TK = 256, 512
MOE_TM = 512

MLA_HEADS, MLA_Q_RANK, MLA_KV_RANK, MLA_NOPE, MLA_ROPE, MLA_V = 16, 384, 256, 64, 32, 64
ROPE_THETA = 10000.0
NSA_HEADS, NSA_GROUPS, NSA_HD = 16, 4, 64
NSA_CMP_LEN, NSA_CMP_STRIDE, NSA_SLC_LEN, NSA_TOP_N, NSA_WINDOW, NSA_FORCE = 32, 16, 64, 8, 512, 1e4
DIFF_HEADS, DIFF_HD = 8, 64
DIL_PATTERNS = ((128, 1), (512, 4), (2048, 16))
DIL_HEADS, DIL_HD = 8, 64
MOE_GROUPS, MOE_EPG, MOE_EXPERTS, MOE_FF = 4, 4, 16, 512


def _cparams(*sem):
    return pltpu.CompilerParams(dimension_semantics=sem, vmem_limit_bytes=VMEM_LIMIT)


def _alibi(n):
    return np.asarray(2.0 ** (-8.0 * np.arange(1, n + 1) / n), np.float32)


def _linear_kernel(x_ref, w_ref, *out_refs, splits, chunk):
    xb = x_ref[...].astype(BF16)
    col = 0
    for o_ref, n in zip(out_refs, splits):
        for c0 in range(0, n, chunk):
            cw = min(chunk, n - c0)
            o_ref[:, c0:c0 + cw] = jnp.dot(
                xb, w_ref[:, col + c0:col + c0 + cw], preferred_element_type=F32).astype(o_ref.dtype)
        col += n


def _linear(x, w, splits, dtypes, tm=512, name="linear"):
    M, K = x.shape
    ntot = sum(splits)
    assert w.shape == (K, ntot) and M % tm == 0 and all(n % LANE == 0 for n in splits)
    outs = pl.pallas_call(
        functools.partial(_linear_kernel, splits=tuple(splits), chunk=512),
        out_shape=[jax.ShapeDtypeStruct((M, n), d) for n, d in zip(splits, dtypes)],
        grid=(M // tm,),
        in_specs=[pl.BlockSpec((tm, K), lambda i: (i, 0)),
                  pl.BlockSpec((K, ntot), lambda i: (0, 0))],
        out_specs=[pl.BlockSpec((tm, n), lambda i: (i, 0)) for n in splits],
        compiler_params=_cparams("parallel"),
        name=name,
    )(x, w)
    return outs


def _layer_norm_rows(z, g, b):
    mu = jnp.mean(z, axis=-1, keepdims=True)
    zc = z - mu
    var = jnp.mean(zc * zc, axis=-1, keepdims=True)
    return zc * lax.rsqrt(var + LN_EPS) * g + b


def _outproj_ln_kernel(o_ref, w_ref, x_ref, g_ref, b_ref, out_ref):
    y = jnp.dot(o_ref[...].astype(BF16), w_ref[...], preferred_element_type=F32)
    out_ref[...] = _layer_norm_rows(DN_ALPHA * x_ref[...] + y, g_ref[...], b_ref[...])


def _outproj_ln(o, w_o, x, g, b, tm=512, name="outproj_ln"):
    M, K = o.shape
    D = x.shape[1]
    return pl.pallas_call(
        _outproj_ln_kernel,
        out_shape=jax.ShapeDtypeStruct((M, D), F32),
        grid=(M // tm,),
        in_specs=[pl.BlockSpec((tm, K), lambda i: (i, 0)),
                  pl.BlockSpec((K, D), lambda i: (0, 0)),
                  pl.BlockSpec((tm, D), lambda i: (i, 0)),
                  pl.BlockSpec((1, D), lambda i: (0, 0)),
                  pl.BlockSpec((1, D), lambda i: (0, 0))],
        out_specs=pl.BlockSpec((tm, D), lambda i: (i, 0)),
        compiler_params=_cparams("parallel"),
        name=name,
    )(o, w_o.astype(BF16), x, g.reshape(1, D), b.reshape(1, D))


def _scores(q, k, bias):
    s = lax.dot_general(q, k, (((1,), (1,)), ((), ())), preferred_element_type=F32)
    return s if bias is None else s + bias


def _causal_item_list(n_q, tq, tk):
    items = []
    for i in range(n_q):
        c_d = (i * tq) // tk
        items += [(i, c, c == 0, c == c_d) for c in range(c_d + 1)]
    return tuple(items)


def _flat_scratch(n_streams, M, t):
    return [pltpu.VMEM((n_streams, 2, M, t), F32), pltpu.VMEM((n_streams, 2, M, t), BF16),
            pltpu.VMEM((n_streams, M, LANE), F32), pltpu.VMEM((n_streams, 2, 3, M, LANE), F32)]


def _flash_static(items, streams, finalize, scratch, tq, tk):
    s_ref, p_ref, acc_ref, st_ref = scratch
    ns = len(streams)
    reps = tk // LANE
    n_rep = s_ref.shape[2] // tq

    def scores(n, item):
        q_at, k_at, _, bias_at = streams[n]
        return _scores(q_at(item[0]), k_at(item[1]), None if bias_at is None else bias_at(item[1]))

    def lane_partial_sum(p):
        return functools.reduce(jnp.add, [p[:, r * LANE:(r + 1) * LANE] for r in range(reps)])

    def value_update(n, prev, slot_prev):
        pv = jnp.dot(p_ref[n, slot_prev], streams[n][2](prev[1]), preferred_element_type=F32)
        acc_ref[n] = pv if prev[2] else st_ref[n, slot_prev, 2] * acc_ref[n] + pv

    def finish(prev, slot_prev):
        finalize(prev[0], [(jnp.sum(st_ref[n, slot_prev, 1], axis=-1, keepdims=True), acc_ref[n]) for n in range(ns)])

    for n in range(ns):
        s_ref[n, 0] = scores(n, items[0])
    for j, item in enumerate(items):
        slot = j % 2
        i, c, first, last = item
        if j + 1 < len(items):
            for n in range(ns):
                s_ref[n, 1 - slot] = scores(n, items[j + 1])
        if j > 0:
            for n in range(ns):
                value_update(n, items[j - 1], 1 - slot)
        vis = None
        if (c + 1) * tk > i * tq + 1:
            vis = _rel_pos(tq, tk) + (c * tk - i * tq) <= 0
            if n_rep > 1:
                vis = jnp.concatenate([vis] * n_rep, axis=0)
        for n in range(ns):
            s = s_ref[n, slot]
            if vis is not None:
                s = jnp.where(vis, s, NEG_INF)
            m_cur = jnp.max(s, axis=-1, keepdims=True)
            if first:
                m_new = jnp.broadcast_to(m_cur, (s.shape[0], LANE))
            else:
                m_old = st_ref[n, 1 - slot, 0]
                m_new = jnp.maximum(m_old, m_cur)
                alpha = jnp.exp2(m_old - m_new)
                st_ref[n, slot, 2] = alpha
            p = jnp.exp2(s - jnp.concatenate([m_new] * reps, axis=1))
            st_ref[n, slot, 0] = m_new
            st_ref[n, slot, 1] = lane_partial_sum(p) if first else alpha * st_ref[n, 1 - slot, 1] + lane_partial_sum(p)
            p_ref[n, slot] = p.astype(BF16)
        if j > 0 and items[j - 1][3]:
            finish(items[j - 1], 1 - slot)
    last_slot = (len(items) - 1) % 2
    for n in range(ns):
        value_update(n, items[-1], last_slot)
    finish(items[-1], last_slot)


def _window_attention(q, ks, vs, viss, biases=None):
    ss = []
    for j, (k, vis) in enumerate(zip(ks, viss)):
        s = _scores(q, k, None if biases is None else biases[j])
        ss.append(s if vis is None else jnp.where(vis, s, NEG_INF))
    m = functools.reduce(jnp.maximum, [jnp.max(s, axis=-1, keepdims=True) for s in ss])
    ps = [jnp.exp2(s - m) for s in ss]
    l = functools.reduce(jnp.add, [jnp.sum(p, axis=-1, keepdims=True) for p in ps])
    acc = functools.reduce(jnp.add, [jnp.dot(p.astype(BF16), v, preferred_element_type=F32) for p, v in zip(ps, vs)])
    return m, l, acc


def _rel_pos(rows, cols):
    return (lax.broadcasted_iota(jnp.int32, (rows, cols), 1)
            - lax.broadcasted_iota(jnp.int32, (rows, cols), 0))


def _rms_rows(c, g, eps):
    return c * lax.rsqrt(jnp.mean(c * c, axis=-1, keepdims=True) + eps) * g


def _mla_proj_kernel(x_ref, win_ref, qn_ref, kvn_ref, wq_ref, wqs_ref, wk_ref, wv_ref, cos_ref, sin_ref,
                     q_out, k_out, v_out, *, scale):
    xb = x_ref[...].astype(BF16)
    c = jnp.dot(xb, win_ref[...], preferred_element_type=F32)
    r0, r1 = MLA_Q_RANK, MLA_Q_RANK + MLA_KV_RANK
    cq = _rms_rows(c[:, :r0], qn_ref[...], 1e-6).astype(BF16)
    ckv = _rms_rows(c[:, r0:r1], kvn_ref[...], 1e-6).astype(BF16)
    cos, sin = cos_ref[...], sin_ref[...]
    kr = c[:, r1:r1 + LANE] * cos + c[:, r1 + LANE:r1 + 2 * LANE] * sin
    v_out[...] = jnp.dot(ckv, wv_ref[...], preferred_element_type=F32).astype(v_out.dtype)
    for h in range(MLA_HEADS):
        sl = slice(h * LANE, (h + 1) * LANE)
        qh = jnp.dot(cq, wq_ref[:, sl], preferred_element_type=F32)
        qhs = jnp.dot(cq, wqs_ref[:, sl], preferred_element_type=F32)
        q_out[:, sl] = ((qh * cos + qhs * sin) * scale).astype(q_out.dtype)
        kh = jnp.dot(ckv, wk_ref[:, sl], preferred_element_type=F32)
        k_out[:, sl] = (kh + kr).astype(k_out.dtype)


def _mla_attn_kernel(q_ref, k_ref, v_ref, o_ref, *scratch, tq, tk, items):
    def rows(i, t):
        return pl.ds(i * t, t)

    def v_at(c):
        return v_ref[0, rows(c, tk), :]

    streams = []
    for hh in range(2):
        sl = slice(hh * LANE, (hh + 1) * LANE)
        streams.append((lambda i, sl=sl: q_ref[0, rows(i, tq), sl], lambda c, sl=sl: k_ref[0, rows(c, tk), sl],
                        v_at, None))
    lane = lax.broadcasted_iota(jnp.int32, (tq, LANE), 1)

    def finalize(i, res):
        (la, acca), (lb, accb) = res
        o_ref[0, rows(i, tq), :] = jnp.where(lane < MLA_V, acca / la, accb / lb).astype(o_ref.dtype)

    _flash_static(items, streams, finalize, scratch, tq, tk)


def _mla_mixer(x, w_in, q_norm, kv_norm, w_qb, w_kvb, w_o, ln_g, ln_b, B, S):
    N, D = x.shape
    H, dq = MLA_HEADS, MLA_NOPE + MLA_ROPE
    half = MLA_ROPE // 2
    r0, r1 = MLA_Q_RANK, MLA_Q_RANK + MLA_KV_RANK
    z = lambda *s: jnp.zeros(s, F32)
    swap = lambda a: jnp.concatenate([a[..., half:], a[..., :half]], axis=-1)
    kr_w = w_in[:, r1:]
    win = jnp.concatenate([w_in[:, :r1],
                           z(D, MLA_NOPE), kr_w, z(D, LANE - dq),
                           z(D, MLA_NOPE), swap(kr_w), z(D, LANE - dq)], axis=1).astype(BF16)
    wq3 = w_qb.reshape(r0, H, dq)
    wq = jnp.concatenate([wq3, z(r0, H, LANE - dq)], axis=-1).reshape(r0, H * LANE).astype(BF16)
    wqs = jnp.concatenate([z(r0, H, MLA_NOPE), swap(wq3[..., MLA_NOPE:]), z(r0, H, LANE - dq)],
                          axis=-1).reshape(r0, H * LANE).astype(BF16)
    wkv3 = w_kvb.reshape(MLA_KV_RANK, H, MLA_NOPE + MLA_V)
    wk = jnp.concatenate([wkv3[..., :MLA_NOPE], z(MLA_KV_RANK, H, LANE - MLA_NOPE)],
                         axis=-1).reshape(MLA_KV_RANK, H * LANE).astype(BF16)
    wv = wkv3[..., MLA_NOPE:].reshape(MLA_KV_RANK, H * MLA_V).astype(BF16)
    freq = ROPE_THETA ** (-jnp.arange(half, dtype=F32) / half)
    ang = jnp.arange(S, dtype=F32)[:, None] * freq
    cos, sin = jnp.cos(ang), jnp.sin(ang)
    ones, zer = jnp.ones((S, MLA_NOPE), F32), jnp.zeros((S, LANE - dq), F32)
    cos_t = jnp.concatenate([ones, cos, cos, zer], axis=1)
    sin_t = jnp.concatenate([0 * ones, -sin, sin, zer], axis=1)

    tm = 256
    nwin = win.shape[1]
    q, k, v = pl.pallas_call(
        functools.partial(_mla_proj_kernel, scale=dq ** -0.5 * LOG2E),
        out_shape=[jax.ShapeDtypeStruct((N, H * LANE), BF16),
                   jax.ShapeDtypeStruct((N, H * LANE), BF16),
                   jax.ShapeDtypeStruct((N, H * MLA_V), BF16)],
        grid=(N // tm,),
        in_specs=[pl.BlockSpec((tm, D), lambda i: (i, 0)),
                  pl.BlockSpec((D, nwin), lambda i: (0, 0)),
                  pl.BlockSpec((1, r0), lambda i: (0, 0)),
                  pl.BlockSpec((1, MLA_KV_RANK), lambda i: (0, 0)),
                  pl.BlockSpec((r0, H * LANE), lambda i: (0, 0)),
                  pl.BlockSpec((r0, H * LANE), lambda i: (0, 0)),
                  pl.BlockSpec((MLA_KV_RANK, H * LANE), lambda i: (0, 0)),
                  pl.BlockSpec((MLA_KV_RANK, H * MLA_V), lambda i: (0, 0)),
                  pl.BlockSpec((tm, LANE), lambda i: (i % (S // tm), 0)),
                  pl.BlockSpec((tm, LANE), lambda i: (i % (S // tm), 0))],
        out_specs=[pl.BlockSpec((tm, H * LANE), lambda i: (i, 0)),
                   pl.BlockSpec((tm, H * LANE), lambda i: (i, 0)),
                   pl.BlockSpec((tm, H * MLA_V), lambda i: (i, 0))],
        compiler_params=_cparams("parallel"),
        name="mla_proj",
    )(x, win, q_norm.reshape(1, r0), kv_norm.reshape(1, MLA_KV_RANK), wq, wqs, wk, wv, cos_t, sin_t)

    tq, tk = ATTN_TQ, ATTN_TK
    o = pl.pallas_call(
        functools.partial(_mla_attn_kernel, tq=tq, tk=tk, items=_causal_item_list(S // tq, tq, tk)),
        out_shape=jax.ShapeDtypeStruct((B, S, H * MLA_V), BF16),
        grid=(B, H // 2),
        in_specs=[pl.BlockSpec((1, S, 2 * LANE), lambda b, h: (b, 0, h)),
                  pl.BlockSpec((1, S, 2 * LANE), lambda b, h: (b, 0, h)),
                  pl.BlockSpec((1, S, LANE), lambda b, h: (b, 0, h))],
        out_specs=pl.BlockSpec((1, S, LANE), lambda b, h: (b, 0, h)),
        scratch_shapes=_flat_scratch(2, tq, tk),
        compiler_params=_cparams("parallel", "parallel"),
        name="mla_attn",
    )(q.reshape(B, S, H * LANE), k.reshape(B, S, H * LANE), v.reshape(B, S, H * MLA_V))
    return _outproj_ln(o.reshape(N, H * MLA_V), w_o, x, ln_g, ln_b, name="mla_out_ln")


def _diff_attn_kernel(slope_ref, lam_ref, sub_ref, q_ref, k_ref, v_ref, o_ref, *scratch, tq, tk, items, lam_init):
    slope = slope_ref[pl.program_id(1)] * LOG2E
    lv = lam_ref[...]
    lam = (jnp.exp(jnp.sum(lv[0:1] * lv[1:2], axis=-1, keepdims=True))
           - jnp.exp(jnp.sum(lv[2:3] * lv[3:4], axis=-1, keepdims=True)) + lam_init)
    lane = lax.broadcasted_iota(jnp.int32, (tq, LANE), 1)
    colf = lax.broadcasted_iota(jnp.int32, (1, tk), 1).astype(F32)

    def rows(i, t):
        return pl.ds(i * t, t)

    def q_at(i, first_map):
        qf = q_ref[0, rows(i, tq), :].astype(F32) * (DIFF_HD ** -0.5 * LOG2E)
        return jnp.where((lane < DIFF_HD) == first_map, qf, 0.0).astype(BF16)

    def k_at(c):
        return k_ref[0, rows(c, tk), :]

    def v_at(c):
        return v_ref[0, rows(c, tk), :]

    def bias_at(c):
        return slope * (colf + float(c * tk))

    def finalize(i, res):
        (l0, acc0), (l1, acc1) = res
        a = _rms_rows(acc0 / l0 - lam * (acc1 / l1), sub_ref[...], 1e-5) * (1.0 - lam_init)
        o_ref[0, rows(i, tq), :] = a.astype(o_ref.dtype)

    streams = [(functools.partial(q_at, first_map=fm), k_at, v_at, bias_at) for fm in (True, False)]
    _flash_static(items, streams, finalize, scratch, tq, tk)


def _diff_mixer(x, w_in, lam_q1, lam_k1, lam_q2, lam_k2, subln, w_o, layer_idx, ln_g, ln_b, B, S):
    N, D = x.shape
    H, d = DIFF_HEADS, DIFF_HD
    nq = H * 2 * d
    (qkv,) = _linear(x, w_in.astype(BF16), [3 * nq], [BF16], name="diff_proj")
    qkv = qkv.reshape(B, S, 3 * nq)
    lam_init = 0.8 - 0.6 * math.exp(-0.3 * layer_idx)
    lamv = jnp.zeros((8, LANE), F32).at[:4, :d].set(jnp.stack([lam_q1, lam_k1, lam_q2, lam_k2]).astype(F32))
    tq, tk = ATTN_TQ, ATTN_TK
    o = pl.pallas_call(
        functools.partial(_diff_attn_kernel, tq=tq, tk=tk, items=_causal_item_list(S // tq, tq, tk), lam_init=lam_init),
        out_shape=jax.ShapeDtypeStruct((B, S, nq), BF16),
        grid=(B, H),
        in_specs=[pl.BlockSpec(memory_space=pltpu.SMEM),
                  pl.BlockSpec((8, LANE), lambda b, h: (0, 0)),
                  pl.BlockSpec((1, 2 * d), lambda b, h: (0, 0)),
                  pl.BlockSpec((1, S, LANE), lambda b, h: (b, 0, h)),
                  pl.BlockSpec((1, S, LANE), lambda b, h: (b, 0, H + h)),
                  pl.BlockSpec((1, S, LANE), lambda b, h: (b, 0, 2 * H + h))],
        out_specs=pl.BlockSpec((1, S, LANE), lambda b, h: (b, 0, h)),
        scratch_shapes=_flat_scratch(2, tq, tk),
        compiler_params=_cparams("parallel", "parallel"),
        name="diff_attn",
    )(jnp.asarray(_alibi(H)), lamv, subln.reshape(1, 2 * d).astype(F32), qkv, qkv, qkv)
    return _outproj_ln(o.reshape(N, nq), w_o, x, ln_g, ln_b, name="diff_out_ln")


SEL_LANE0 = NSA_HD
SEL_OFF = 1e30
ALIBI_LANE0 = 96
POS_SPLIT = 64


def _nsa_compress_kernel(ak_ref, av_ref, plo_ref, phi_ref, wklo_ref, wkhi_ref, wvlo_ref, wvhi_ref, kc_out, vc_out):
    def one(a_ref, wlo_ref, whi_ref, out):
        a = a_ref[0]
        lo = jnp.dot((a + plo_ref[...]).astype(BF16), wlo_ref[...], preferred_element_type=F32)
        hi = jnp.dot((a + phi_ref[...]).astype(BF16), whi_ref[...], preferred_element_type=F32)
        out[0] = (lo + pltpu.roll(hi, hi.shape[0] - 1, 0)).astype(out.dtype)

    one(ak_ref, wklo_ref, wkhi_ref, kc_out)
    one(av_ref, wvlo_ref, wvhi_ref, vc_out)


def _pack_heads(o, tq):
    lane = lax.broadcasted_iota(jnp.int32, (tq, LANE), 1)
    p01 = jnp.where(lane < NSA_HD, o[0:tq], o[tq:2 * tq])
    p23 = jnp.where(lane < NSA_HD, o[2 * tq:3 * tq], o[3 * tq:4 * tq])
    return jnp.concatenate([p01, p23], axis=1)


def _nsa_attn_kernel(q_ref, qc_ref, gl_ref, kc_ref, vc_ref, ks_ref, vs_ref, kw_ref, vw_ref, ktab_ref, ctab_ref,
                     c2s_ref, rep_ref, o_ref, selb_ref, ocw_ref, gs_ref, *scratch, tq, tk, tqs, tks, n_slc, items):
    hpg = NSA_HEADS // NSA_GROUPS

    def head_q(rows, j):
        return q_ref[0, rows, j * LANE:(j + 1) * LANE] + jnp.broadcast_to(qc_ref[0, j:j + 1, :],
                                                                         (rows.size, LANE)).astype(BF16)

    def tile(i, carry):
        _nsa_tile(i, head_q, gl_ref, kc_ref, vc_ref, kw_ref, vw_ref, ktab_ref, ctab_ref, c2s_ref, rep_ref,
                  selb_ref, ocw_ref, gs_ref, tq=tq, tk=tk, n_slc=n_slc)
        return carry

    lax.fori_loop(0, q_ref.shape[1] // tq, tile, 0, unroll=2)

    def rows_s(i, t):
        return pl.ds(i * t, t)

    def q_at(i, pair):
        r = rows_s(i, tqs)
        return jnp.concatenate([head_q(r, j) + selb_ref[r, :] for j in (2 * pair, 2 * pair + 1)], axis=0)

    def ks_at(c):
        return ks_ref[0, rows_s(c, tks), :] + ktab_ref[rows_s(c, tks), :]

    def vs_at(c):
        return vs_ref[0, rows_s(c, tks), :]

    def finalize(i, res):
        r = rows_s(i, tqs)
        o_s = jnp.concatenate([acc / l for l, acc in res], axis=0)
        o_ref[0, r, :] = (ocw_ref[r, :] + gs_ref[r, :] * _pack_heads(o_s, tqs)).astype(o_ref.dtype)

    streams = [(functools.partial(q_at, pair=pr), ks_at, vs_at, None) for pr in range(hpg // 2)]
    _flash_static(items, streams, finalize, scratch, tqs, tks)


def _nsa_tile(i, head_q, gl_ref, kc_ref, vc_ref, kw_ref, vw_ref, ktab_ref, ctab_ref, c2s_ref, rep_ref,
              selb_ref, ocw_ref, gs_ref, *, tq, tk, n_slc):
    t0 = i * tq
    rows = pl.ds(pl.multiple_of(t0, tq), tq)
    hpg = NSA_HEADS // NSA_GROUPS
    M = hpg * tq
    q = jnp.concatenate([head_q(rows, j) for j in range(hpg)], axis=0)
    trow1 = t0 + lax.broadcasted_iota(jnp.int32, (tq, 1), 0)
    trow = jnp.concatenate([trow1] * hpg, axis=0)
    lane_m = lax.broadcasted_iota(jnp.int32, (M, LANE), 1)

    sc = lax.dot_general(q, kc_ref[0] + ctab_ref[...], (((1,), (1,)), ((), ())), preferred_element_type=F32)
    sc = jnp.where(trow >= lane_m * NSA_CMP_STRIDE + (NSA_CMP_LEN - 1), sc, NEG_INF)
    e = jnp.exp2(sc - jnp.max(sc, axis=-1, keepdims=True))
    p_c = jnp.where(trow >= NSA_CMP_LEN - 1, e / jnp.sum(e, axis=-1, keepdims=True), 0.0)
    o_c = jnp.dot(p_c.astype(BF16), vc_ref[0], preferred_element_type=F32)
    psum = p_c[0:tq]
    for j in range(1, hpg):
        psum = psum + p_c[j * tq:(j + 1) * tq]
    ph, plw = _split_bf16(psum)
    imp = (jnp.dot(ph, c2s_ref[...], preferred_element_type=F32)
           + jnp.dot(plw, c2s_ref[...], preferred_element_type=F32))

    lane = lax.broadcasted_iota(jnp.int32, (tq, LANE), 1)
    sidx = lane - SEL_LANE0
    valid = (sidx >= 0) & (sidx < n_slc)
    forced = (sidx == 0) | (sidx == trow1 // NSA_SLC_LEN)
    future = sidx * NSA_SLC_LEN > trow1
    score = imp + jnp.where(forced, NSA_FORCE, 0.0) - jnp.where(future, 2.0 * NSA_FORCE, 0.0)
    score = jnp.where(valid, score, -jnp.inf)
    n_rows = ALIBI_LANE0 - SEL_LANE0
    st = score.T[SEL_LANE0:ALIBI_LANE0]
    blk = lax.broadcasted_iota(jnp.int32, (n_rows, tq), 0)
    sel = jnp.zeros((n_rows, tq), jnp.bool_)
    for _ in range(min(NSA_TOP_N, n_slc)):
        mx = jnp.max(st, axis=0, keepdims=True)
        pick = blk == jnp.min(jnp.where(st == mx, blk, n_rows), axis=0, keepdims=True)
        sel = sel | pick
        st = jnp.where(pick, -jnp.inf, st)
    off = jnp.where(sel | (blk >= n_slc), 0.0, -SEL_OFF)
    selbias = jnp.concatenate([jnp.zeros((SEL_LANE0, tq), F32), off, jnp.zeros((LANE - ALIBI_LANE0, tq), F32)],
                              axis=0).T.astype(BF16)

    selb_ref[rows, :] = selbias

    c_d = t0 // tk
    d_diag = jnp.concatenate([_rel_pos(tq, tk)] * hpg, axis=0) + (c_d * tk - t0)

    def chunk(c):
        return pl.ds(pl.multiple_of(c * tk, tk), tk)

    cw = [jnp.maximum(c_d - 2, 0), jnp.maximum(c_d - 1, 0), c_d]
    vis_w = [(d_diag - 2 * tk > -NSA_WINDOW) & (c_d >= 2), jnp.broadcast_to(c_d >= 1, d_diag.shape), d_diag <= 0]
    _, l_w, acc_w = _window_attention(q, [kw_ref[0, chunk(c), :] + ktab_ref[chunk(c), :] for c in cw],
                                      [vw_ref[0, chunk(c), :] for c in cw], vis_w)
    o_w = acc_w / l_w

    gh, glw = _split_bf16(jax.nn.sigmoid(gl_ref[0, rows, :]))
    gr = jnp.dot(gh, rep_ref[0], preferred_element_type=F32) + jnp.dot(glw, rep_ref[0], preferred_element_type=F32)
    w = hpg * NSA_HD
    ocw_ref[rows, :] = gr[:, 0:w] * _pack_heads(o_c, tq) + gr[:, 2 * w:3 * w] * _pack_heads(o_w, tq)
    gs_ref[rows, :] = gr[:, w:2 * w]


def _nsa_mixer(x, w_in, w_phi_k, w_phi_v, cmp_pos, w_o, ln_g, ln_b, B, S):
    N, D = x.shape
    H, G, d = NSA_HEADS, NSA_GROUPS, NSA_HD
    hpg = H // G
    L, st = NSA_CMP_LEN, NSA_CMP_STRIDE
    n_slc = S // NSA_SLC_LEN
    assert S % 256 == 0 and n_slc <= ALIBI_LANE0 - SEL_LANE0 and L == 2 * st
    cuts = [H * d + i * G * d for i in range(7)]
    wq, wkc, wvc, wks, wvs, wkw, wvw, wgl = jnp.split(w_in, cuts, axis=1)
    z = lambda *s: jnp.zeros(s, F32)
    pad_heads = lambda w, n: jnp.concatenate([w.reshape(D, n, d), z(D, n, LANE - d)], axis=-1).reshape(D, n * LANE)
    dup_heads = lambda w, n: jnp.concatenate([w.reshape(D, n, d)] * 2, axis=-1).reshape(D, n * LANE)
    wgl_p = jnp.concatenate([wgl, z(D, LANE - wgl.shape[1])], axis=1)
    wcat = jnp.concatenate([pad_heads(wq * (d ** -0.5 * LOG2E), H), wkc, wvc, pad_heads(wks, G), dup_heads(wvs, G),
                            pad_heads(wkw, G), dup_heads(wvw, G), wgl_p], axis=1).astype(BF16)
    q, kc, vc, ks, vs, kw, vw, gl = _linear(
        x, wcat, [H * LANE, G * d, G * d, G * LANE, G * LANE, G * LANE, G * LANE, LANE],
        [BF16, F32, F32, BF16, BF16, BF16, BF16, F32], name="nsa_proj")

    nrow = S // st
    eye = jnp.eye(G, dtype=F32)

    def phi_w(w_phi, half, dup):
        w = w_phi.reshape(L, d, d)[half * st:(half + 1) * st]
        wd = jnp.concatenate([w, w if dup else jnp.zeros_like(w)], axis=-1)
        return jnp.einsum('ldc,gh->lgdhc', wd, eye).reshape(st * G * d, G * LANE).astype(BF16)

    pos = lambda half: jnp.broadcast_to(cmp_pos[half * st:(half + 1) * st, None, :], (st, G, d)).reshape(1, st * G * d)
    wide = st * G * d
    cspec = pl.BlockSpec((1, nrow, wide), lambda b: (b, 0, 0))
    wspec = pl.BlockSpec((wide, G * LANE), lambda b: (0, 0))
    pspec = pl.BlockSpec((1, wide), lambda b: (0, 0))
    ospec = pl.BlockSpec((1, nrow, G * LANE), lambda b: (b, 0, 0))
    kcmp, vcmp = pl.pallas_call(
        _nsa_compress_kernel,
        out_shape=[jax.ShapeDtypeStruct((B, nrow, G * LANE), BF16)] * 2,
        grid=(B,),
        in_specs=[cspec, cspec, pspec, pspec, wspec, wspec, wspec, wspec],
        out_specs=[ospec, ospec],
        compiler_params=_cparams("parallel"),
        name="nsa_compress",
    )(kc.reshape(B, nrow, wide), vc.reshape(B, nrow, wide), pos(0), pos(1),
      phi_w(w_phi_k, 0, False), phi_w(w_phi_k, 1, False), phi_w(w_phi_v, 0, True), phi_w(w_phi_v, 1, True))

    def pos_lanes(tab, pos):
        tab[:, ALIBI_LANE0:ALIBI_LANE0 + 3] = (pos // POS_SPLIT)[:, None]
        tab[:, ALIBI_LANE0 + 3:ALIBI_LANE0 + 6] = (pos % POS_SPLIT)[:, None]
        return tab

    ktab = np.zeros((S, LANE), np.float32)
    ktab[np.arange(S), SEL_LANE0 + np.arange(S) // NSA_SLC_LEN] = 1.0
    ktab = pos_lanes(ktab, np.arange(S))
    ctab = pos_lanes(np.zeros((nrow, LANE), np.float32), np.arange(nrow) * st + (L - 1))
    s2 = jnp.asarray(_alibi(H) * LOG2E, F32)
    parts = []
    for _ in range(3):
        part = s2.astype(BF16).astype(F32)
        parts.append(part)
        s2 = s2 - part
    qcoef = jnp.zeros((H, LANE), F32).at[:, ALIBI_LANE0:ALIBI_LANE0 + 6].set(
        jnp.stack([POS_SPLIT * p for p in parts] + parts, axis=1)).reshape(G, hpg, LANE)
    cmp_start = np.arange(nrow) * st
    slc_start = np.arange(n_slc) * NSA_SLC_LEN
    ov = (cmp_start[:, None] < slc_start[None, :] + NSA_SLC_LEN) & (cmp_start[:, None] + L > slc_start[None, :])
    ov[(S - L) // st + 1:] = False
    c2s = np.zeros((nrow, LANE), np.float32)
    c2s[:, SEL_LANE0:SEL_LANE0 + n_slc] = ov
    rep = np.zeros((G, LANE, 3 * hpg * d), np.float32)
    for g in range(G):
        for j in range(hpg):
            for br in range(3):
                rep[g, (g * hpg + j) * 3 + br, br * hpg * d + j * d:br * hpg * d + (j + 1) * d] = 1.0

    tq, tk = 128, 256
    tqs, tks = ATTN_TQ, ATTN_TK
    assert NSA_WINDOW == 2 * tk and tk % tq == 0
    kvspec = pl.BlockSpec((1, S, LANE), lambda b, g: (b, 0, g))
    cmpspec = pl.BlockSpec((1, nrow, LANE), lambda b, g: (b, 0, g))
    o = pl.pallas_call(
        functools.partial(_nsa_attn_kernel, tq=tq, tk=tk, tqs=tqs, tks=tks, n_slc=n_slc,
                          items=_causal_item_list(S // tqs, tqs, tks)),
        out_shape=jax.ShapeDtypeStruct((B, S, H * d), BF16),
        grid=(B, G),
        in_specs=[pl.BlockSpec((1, S, hpg * LANE), lambda b, g: (b, 0, g)),
                  pl.BlockSpec((1, hpg, LANE), lambda b, g: (g, 0, 0)),
                  pl.BlockSpec((1, S, LANE), lambda b, g: (b, 0, 0)),
                  cmpspec, cmpspec, kvspec, kvspec, kvspec, kvspec,
                  pl.BlockSpec((S, LANE), lambda b, g: (0, 0)),
                  pl.BlockSpec((nrow, LANE), lambda b, g: (0, 0)),
                  pl.BlockSpec((nrow, LANE), lambda b, g: (0, 0)),
                  pl.BlockSpec((1, LANE, 3 * hpg * d), lambda b, g: (g, 0, 0))],
        out_specs=pl.BlockSpec((1, S, hpg * d), lambda b, g: (b, 0, g)),
        scratch_shapes=[pltpu.VMEM((S, LANE), BF16), pltpu.VMEM((S, hpg * d), F32), pltpu.VMEM((S, hpg * d), F32)]
        + _flat_scratch(hpg // 2, 2 * tqs, tks),
        compiler_params=_cparams("parallel", "parallel"),
        name="nsa_attn",
    )(q.reshape(B, S, H * LANE), qcoef, gl.reshape(B, S, LANE), kcmp, vcmp,
      ks.reshape(B, S, G * LANE), vs.reshape(B, S, G * LANE), kw.reshape(B, S, G * LANE), vw.reshape(B, S, G * LANE),
      jnp.asarray(ktab, BF16), jnp.asarray(ctab, BF16), jnp.asarray(c2s, BF16), jnp.asarray(rep, BF16))
    return _outproj_ln(o.reshape(N, H * d), w_o, x, ln_g, ln_b, name="nsa_out_ln")


def _dil_attn_kernel(slope_ref, q_ref, k_ref, v_ref, o_ref, lse_ref, *, t, dil):
    i = pl.program_id(2)
    d = DIL_HD
    rel = _rel_pos(t, t)
    relf = rel.astype(F32)
    rel2 = jnp.concatenate([rel, rel], axis=0)
    lane = lax.broadcasted_iota(jnp.int32, (t, LANE), 1)
    scale = d ** -0.5 * LOG2E

    def chunk(c):
        return pl.ds(pl.multiple_of(c * t, t), t)

    cs = [jnp.maximum(i - 1, 0), i]
    viss = [(rel2 >= 0) & (i >= 1), rel2 <= 0]
    lse_t = jnp.zeros((t, LANE), F32)
    for hp in range(DIL_HEADS // 2):
        sl = slice(hp * LANE, (hp + 1) * LANE)
        qf = q_ref[0, :, sl].astype(F32) * scale
        q2 = jnp.concatenate([jnp.where(lane < d, qf, 0.0), jnp.where(lane < d, 0.0, qf)], axis=0).astype(BF16)
        sa, sb = slope_ref[2 * hp] * (dil * LOG2E), slope_ref[2 * hp + 1] * (dil * LOG2E)
        bias = jnp.concatenate([sa * relf, sb * relf], axis=0)
        shift = jnp.concatenate([jnp.full((t, 1), sa * t, F32), jnp.full((t, 1), sb * t, F32)], axis=0)
        m, l, acc = _window_attention(q2, [k_ref[0, chunk(c), sl] for c in cs], [v_ref[0, chunk(c), sl] for c in cs],
                                      viss, [bias - shift, bias])
        o = acc / l
        o_ref[0, :, hp * LANE:(hp + 1) * LANE] = jnp.where(lane < d, o[:t], o[t:]).astype(o_ref.dtype)
        lse = m + jnp.log2(l)
        lse_t = jnp.where(lane == 2 * hp, lse[:t], lse_t)
        lse_t = jnp.where(lane == 2 * hp + 1, lse[t:], lse_t)
    lse_ref[0] = lse_t


def _dil_proj_kernel(x_ref, w_ref, *refs, tm):
    outs, scr = refs[:-1], refs[-1]
    xb = x_ref[...].astype(BF16)
    width = scr.shape[0] * LANE
    for p, (o_ref, (_, dil)) in enumerate(zip(outs, DIL_PATTERNS)):
        for c0 in range(0, width, 512):
            y = jnp.dot(xb, w_ref[:, p * width + c0:p * width + c0 + 512], preferred_element_type=F32)
            if dil == 1:
                o_ref[:, c0:c0 + 512] = y.astype(o_ref.dtype)
            else:
                for j in range(512 // LANE):
                    scr[c0 // LANE + j] = y[:, j * LANE:(j + 1) * LANE]
        for c in range(dil if dil > 1 else 0):
            for j in range(width // LANE):
                o_ref[:, c * width + j * LANE:c * width + (j + 1) * LANE] = (
                    scr[j, pl.ds(c, tm // dil, stride=dil), :].astype(o_ref.dtype))


def _dil_merge_ln_kernel(o0_ref, o1_ref, o2_ref, l0_ref, l1_ref, l2_ref, rep_ref, w_ref, x_ref, g_ref, b_ref, out_ref,
                         scr_o, scr_l):
    tm = x_ref.shape[0]

    def token_order(ref, dil, scr):
        if dil == 1:
            return ref[...].astype(F32)
        w = ref.shape[1] // dil
        for c in range(dil):
            for j in range(w // LANE):
                scr[j, pl.ds(c, tm // dil, stride=dil), :] = (
                    ref[:, c * w + j * LANE:c * w + (j + 1) * LANE].astype(F32))
        return jnp.concatenate([scr[j] for j in range(w // LANE)], axis=1)

    ls = [token_order(l_ref, dil, scr_l) for l_ref, (_, dil) in zip((l0_ref, l1_ref, l2_ref), DIL_PATTERNS)]
    mx = jnp.maximum(jnp.maximum(ls[0], ls[1]), ls[2])
    es = [jnp.exp2(v - mx) for v in ls]
    tot = es[0] + es[1] + es[2]
    o = None
    for e, o_ref, (_, dil) in zip(es, (o0_ref, o1_ref, o2_ref), DIL_PATTERNS):
        wh, wl = _split_bf16(e / tot)
        wrep = jnp.dot(wh, rep_ref[...], preferred_element_type=F32) + jnp.dot(wl, rep_ref[...], preferred_element_type=F32)
        term = wrep * token_order(o_ref, dil, scr_o)
        o = term if o is None else o + term
    y = jnp.dot(o.astype(BF16), w_ref[...], preferred_element_type=F32)
    out_ref[...] = _layer_norm_rows(DN_ALPHA * x_ref[...] + y, g_ref[...], b_ref[...])


def _dil_mixer(x, w_in, w_o, ln_g, ln_b, B, S):
    N, D = x.shape
    H, d = DIL_HEADS, DIL_HD
    n_pat = len(DIL_PATTERNS)
    hd = H * d
    width = 3 * hd
    ncol = n_pat * width
    tm = 512
    qkvs = pl.pallas_call(
        functools.partial(_dil_proj_kernel, tm=tm),
        out_shape=[jax.ShapeDtypeStruct((N // dil, dil * width), BF16) for _, dil in DIL_PATTERNS],
        grid=(N // tm,),
        in_specs=[pl.BlockSpec((tm, D), lambda i: (i, 0)), pl.BlockSpec((D, ncol), lambda i: (0, 0))],
        out_specs=[pl.BlockSpec((tm // dil, dil * width), lambda i: (i, 0)) for _, dil in DIL_PATTERNS],
        scratch_shapes=[pltpu.VMEM((width // LANE, tm, LANE), F32)],
        compiler_params=_cparams("parallel"),
        name="dil_proj",
    )(x, w_in.astype(BF16))
    tq = 128
    slopes = jnp.asarray(_alibi(H))
    outs, lses = [], []
    for p, (win, dil) in enumerate(DIL_PATTERNS):
        ls = S // dil
        assert ls % tq == 0 and win == dil * tq
        view = qkvs[p].reshape(B, ls, dil * width)
        o, lse = pl.pallas_call(
            functools.partial(_dil_attn_kernel, t=tq, dil=dil),
            out_shape=[jax.ShapeDtypeStruct((B, ls, dil * hd), BF16),
                       jax.ShapeDtypeStruct((B, ls, dil * LANE), F32)],
            grid=(B, dil, ls // tq),
            in_specs=[pl.BlockSpec(memory_space=pltpu.SMEM),
                      pl.BlockSpec((1, tq, hd), lambda b, r, i: (b, i, 3 * r)),
                      pl.BlockSpec((1, ls, hd), lambda b, r, i: (b, 0, 3 * r + 1)),
                      pl.BlockSpec((1, ls, hd), lambda b, r, i: (b, 0, 3 * r + 2))],
            out_specs=[pl.BlockSpec((1, tq, hd), lambda b, r, i: (b, i, r)),
                       pl.BlockSpec((1, tq, LANE), lambda b, r, i: (b, i, r))],
            compiler_params=_cparams("parallel", "parallel", "arbitrary"),
            name=f"dil_attn_{p}",
        )(slopes, view, view, view)
        outs.append(o.reshape(N // dil, dil * hd))
        lses.append(lse.reshape(N // dil, dil * LANE))

    rep = np.zeros((LANE, hd), np.float32)
    for h in range(H):
        rep[h, h * d:(h + 1) * d] = 1.0
    row = lambda n: pl.BlockSpec((tm, n), lambda i: (i, 0))
    rowp = lambda n: [pl.BlockSpec((tm // dil, dil * n), lambda i: (i, 0)) for _, dil in DIL_PATTERNS]
    full = lambda a, b: pl.BlockSpec((a, b), lambda i: (0, 0))
    return pl.pallas_call(
        _dil_merge_ln_kernel,
        out_shape=jax.ShapeDtypeStruct((N, D), F32),
        grid=(N // tm,),
        in_specs=rowp(hd) + rowp(LANE) + [full(LANE, hd), full(hd, D), row(D), full(1, D), full(1, D)],
        out_specs=row(D),
        scratch_shapes=[pltpu.VMEM((hd // LANE, tm, LANE), F32), pltpu.VMEM((1, tm, LANE), F32)],
        compiler_params=_cparams("parallel"),
        name="dil_merge_out_ln",
    )(*outs, *lses, jnp.asarray(rep, BF16), w_o.astype(BF16), x, ln_g.reshape(1, D), ln_b.reshape(1, D))


def _split_bf16(a):
    hi = a.astype(BF16)
    return hi, (a - hi.astype(F32)).astype(BF16)


META_GRP, META_RANK = MOE_EPG, MOE_EPG + 1


def _router_kernel(x_ref, wh_ref, wl_ref, b_ref, tri_ref, meta_ref, cnt_ref, run_ref):
    @pl.when(pl.program_id(0) == 0)
    def _():
        run_ref[...] = jnp.zeros_like(run_ref)

    xh, xl = _split_bf16(x_ref[...])
    wh, wl = wh_ref[...], wl_ref[...]
    logits = (jnp.dot(xh, wh, preferred_element_type=F32) + jnp.dot(xl, wh, preferred_element_type=F32)
              + jnp.dot(xh, wl, preferred_element_type=F32)) + b_ref[...]
    tm = logits.shape[0]
    lane = lax.broadcasted_iota(jnp.int32, (tm, LANE), 1)
    big = jnp.int32(LANE)
    lg = jnp.where(lane < MOE_GROUPS, logits, -jnp.inf)
    mg = jnp.max(lg, axis=-1, keepdims=True)
    sg = jnp.sum(jnp.exp(lg - mg), axis=-1, keepdims=True)
    pg_top = 1.0 / sg
    g_top = jnp.min(jnp.where(lg == mg, lane, big), axis=-1, keepdims=True)
    e_lo = MOE_GROUPS + MOE_EPG * g_top
    in_grp = (lane >= e_lo) & (lane < e_lo + MOE_EPG)
    le = jnp.where(in_grp, logits, -jnp.inf)
    me = jnp.max(le, axis=-1, keepdims=True)
    ee = jnp.exp(le - me)
    se = jnp.sum(ee, axis=-1, keepdims=True)
    pe = jnp.where(in_grp, ee / se, -1.0)
    v1 = jnp.max(pe, axis=-1, keepdims=True)
    i1 = jnp.min(jnp.where(pe == v1, lane, big), axis=-1, keepdims=True)
    pe2 = jnp.where(lane == i1, -1.0, pe)
    v2 = jnp.max(pe2, axis=-1, keepdims=True)
    i2 = jnp.min(jnp.where(pe2 == v2, lane, big), axis=-1, keepdims=True)
    tot = v1 + v2
    gates = (jnp.where(lane == i1 - e_lo, (v1 / tot) * pg_top, 0.0)
             + jnp.where(lane == i2 - e_lo, (v2 / tot) * pg_top, 0.0))
    onehot = lane == g_top
    prefix = jnp.dot(tri_ref[...], jnp.where(onehot, 1.0, 0.0).astype(BF16), preferred_element_type=F32)
    rank = jnp.sum(jnp.where(onehot, prefix + run_ref[...] - 1.0, 0.0), axis=-1, keepdims=True)
    meta_ref[...] = (gates + jnp.where(lane == META_GRP, g_top.astype(F32), 0.0)
                     + jnp.where(lane == META_RANK, rank, 0.0))
    run_ref[...] += prefix[tm - 1:tm, :]
    cnt_ref[...] = run_ref[...]


def _moe_dispatch_kernel(pos_ref, x_ref, meta_ref, zeros_hbm, xs_hbm, buf, sem):
    del zeros_hbm
    tm, d = x_ref.shape
    buf[:, :d] = x_ref[...]
    buf[:, d:] = meta_ref[...]

    def issue(r, carry):
        pltpu.make_async_copy(buf.at[pl.ds(r, 1)], xs_hbm.at[pl.ds(pos_ref[0, 0, r], 1)], sem).start()
        return carry

    lax.fori_loop(0, tm, issue, 0, unroll=8)
    pltpu.make_async_copy(buf, xs_hbm.at[pl.ds(0, tm)], sem).wait()


def _moe_group_kernel(tile_grp_ref, xs_ref, w1_ref, w3_ref, w2_ref, g_ref, b_ref, ys_ref):
    del tile_grp_ref
    d = ys_ref.shape[1]
    x = xs_ref[:, :d]
    gates = xs_ref[:, d:]
    lane = lax.broadcasted_iota(jnp.int32, gates.shape, 1)
    xb = x.astype(BF16)
    y = None
    for e in range(MOE_EPG):
        h1 = jnp.dot(xb, w1_ref[0, e], preferred_element_type=F32)
        h3 = jnp.dot(xb, w3_ref[0, e], preferred_element_type=F32)
        hid = (h1 * jax.nn.sigmoid(h1) * h3).astype(BF16)
        ge = jnp.sum(jnp.where(lane == e, gates, 0.0), axis=-1, keepdims=True)
        term = ge * jnp.dot(hid, w2_ref[0, e], preferred_element_type=F32)
        y = term if y is None else y + term
    ys_ref[...] = _layer_norm_rows(DN_ALPHA * x + y, g_ref[...], b_ref[...])


def _moe_collect_kernel(pos_ref, ys_hbm, out_ref, sem):
    tm = out_ref.shape[0]

    def issue(r, carry):
        pltpu.make_async_copy(ys_hbm.at[pl.ds(pos_ref[0, 0, r], 1)], out_ref.at[pl.ds(r, 1)], sem).start()
        return carry

    lax.fori_loop(0, tm, issue, 0, unroll=8)
    pltpu.make_async_copy(ys_hbm.at[pl.ds(0, tm)], out_ref, sem).wait()


def _hier_moe_ln(x, wg, bg, we, be, w1, w3, w2, ln_g, ln_b):
    N, D = x.shape
    G, E, FF = MOE_GROUPS, MOE_EPG, MOE_FF
    wr = jnp.concatenate([wg, jnp.moveaxis(we, 0, 1).reshape(D, MOE_EXPERTS),
                          jnp.zeros((D, LANE - G - MOE_EXPERTS), F32)], axis=1)
    br = jnp.concatenate([bg, be.reshape(-1), jnp.zeros((LANE - G - MOE_EXPERTS,), F32)]).reshape(1, LANE)
    wrh, wrl = _split_bf16(wr)
    tm = MOE_TM
    tri = jnp.asarray(np.tril(np.ones((tm, tm), np.float32)), BF16)
    meta, cnt = pl.pallas_call(
        _router_kernel,
        out_shape=[jax.ShapeDtypeStruct((N, LANE), F32), jax.ShapeDtypeStruct((1, LANE), F32)],
        grid=(N // tm,),
        in_specs=[pl.BlockSpec((tm, D), lambda i: (i, 0)),
                  pl.BlockSpec((D, LANE), lambda i: (0, 0)),
                  pl.BlockSpec((D, LANE), lambda i: (0, 0)),
                  pl.BlockSpec((1, LANE), lambda i: (0, 0)),
                  pl.BlockSpec((tm, tm), lambda i: (0, 0))],
        out_specs=[pl.BlockSpec((tm, LANE), lambda i: (i, 0)), pl.BlockSpec((1, LANE), lambda i: (0, 0))],
        scratch_shapes=[pltpu.VMEM((1, LANE), F32)],
        compiler_params=_cparams("arbitrary"),
        name="moe_router",
    )(x, wrh, wrl, br, tri)

    counts = cnt[0, :G].astype(jnp.int32)
    padded = (counts + tm - 1) // tm * tm
    ends = jnp.cumsum(padded)
    starts = ends - padded
    pos = (starts[meta[:, META_GRP].astype(jnp.int32)] + meta[:, META_RANK].astype(jnp.int32)).reshape(N // tm, 1, tm)
    n_pad = N + G * tm
    n_tiles = n_pad // tm
    tile_grp = jnp.minimum(jnp.searchsorted(ends, jnp.arange(n_tiles, dtype=jnp.int32) * tm, side="right"),
                           G - 1).astype(jnp.int32)

    pos_spec = pl.BlockSpec((1, 1, tm), lambda i: (i, 0, 0), memory_space=pltpu.SMEM)
    xs = pl.pallas_call(
        _moe_dispatch_kernel,
        out_shape=jax.ShapeDtypeStruct((n_pad, D + LANE), F32),
        grid=(N // tm,),
        in_specs=[pos_spec,
                  pl.BlockSpec((tm, D), lambda i: (i, 0)),
                  pl.BlockSpec((tm, LANE), lambda i: (i, 0)),
                  pl.BlockSpec(memory_space=pl.ANY)],
        out_specs=pl.BlockSpec(memory_space=pl.ANY),
        scratch_shapes=[pltpu.VMEM((tm, D + LANE), F32), pltpu.SemaphoreType.DMA],
        input_output_aliases={3: 0},
        compiler_params=_cparams("arbitrary"),
        name="moe_dispatch",
    )(pos, x, meta, jnp.zeros((n_pad, D + LANE), F32))

    wspec = lambda a, b: pl.BlockSpec((1, E, a, b), lambda t, tg: (tg[t], 0, 0, 0))
    ys = pl.pallas_call(
        _moe_group_kernel,
        out_shape=jax.ShapeDtypeStruct((n_pad, D), F32),
        grid_spec=pltpu.PrefetchScalarGridSpec(
            num_scalar_prefetch=1,
            grid=(n_tiles,),
            in_specs=[pl.BlockSpec((tm, D + LANE), lambda t, tg: (t, 0)),
                      wspec(D, FF), wspec(D, FF), wspec(FF, D),
                      pl.BlockSpec((1, D), lambda t, tg: (0, 0)),
                      pl.BlockSpec((1, D), lambda t, tg: (0, 0))],
            out_specs=pl.BlockSpec((tm, D), lambda t, tg: (t, 0))),
        compiler_params=_cparams("arbitrary"),
        name="moe_experts_ln",
    )(tile_grp, xs, w1.astype(BF16).reshape(G, E, D, FF), w3.astype(BF16).reshape(G, E, D, FF),
      w2.astype(BF16).reshape(G, E, FF, D), ln_g.reshape(1, D), ln_b.reshape(1, D))

    return pl.pallas_call(
        _moe_collect_kernel,
        out_shape=jax.ShapeDtypeStruct((N, D), F32),
        grid=(N // tm,),
        in_specs=[pos_spec, pl.BlockSpec(memory_space=pl.ANY)],
        out_specs=pl.BlockSpec((tm, D), lambda i: (i, 0)),
        scratch_shapes=[pltpu.SemaphoreType.DMA],
        compiler_params=_cparams("arbitrary"),
        name="moe_collect",
    )(pos, ys)


def kernel(x, mla_w_in, mla_q_norm, mla_kv_norm, mla_w_qb, mla_w_kvb, mla_w_o, nsa_w_in, nsa_w_phi_k, nsa_w_phi_v, nsa_cmp_pos, nsa_w_o, diff_w_in, diff_lam_q1, diff_lam_k1, diff_lam_q2, diff_lam_k2, diff_subln, diff_w_o, dil_w_in, dil_w_o, ln1_g, ln1_b, ln2_g, ln2_b, moe_wg, moe_bg, moe_we, moe_be, moe_w1, moe_w3, moe_w2):
    B, S, D = x.shape
    h = x.reshape(B * S, D)
    for i in range(DEPTH):
        m, j = i % 4, i // 4
        if m == 0:
            h = _mla_mixer(h, mla_w_in[j], mla_q_norm[j], mla_kv_norm[j], mla_w_qb[j], mla_w_kvb[j], mla_w_o[j],
                           ln1_g[i], ln1_b[i], B, S)
        elif m == 1:
            h = _nsa_mixer(h, nsa_w_in[j], nsa_w_phi_k[j], nsa_w_phi_v[j], nsa_cmp_pos[j], nsa_w_o[j],
                           ln1_g[i], ln1_b[i], B, S)
        elif m == 2:
            h = _diff_mixer(h, diff_w_in[j], diff_lam_q1[j], diff_lam_k1[j], diff_lam_q2[j], diff_lam_k2[j],
                            diff_subln[j], diff_w_o[j], i, ln1_g[i], ln1_b[i], B, S)
        else:
            h = _dil_mixer(h, dil_w_in[j], dil_w_o[j], ln1_g[i], ln1_b[i], B, S)
        h = _hier_moe_ln(h, moe_wg[i], moe_bg[i], moe_we[i], moe_be[i], moe_w1[i], moe_w3[i], moe_w2[i],
                         ln2_g[i], ln2_b[i])
    return h.reshape(B, S, D)
```

```python
import functools
import math

import numpy as np
import jax
import jax.numpy as jnp
from jax import lax
from jax.experimental import pallas as pl
from jax.experimental.pallas import tpu as pltpu

F32 = jnp.float32
BF16 = jnp.bfloat16

DEPTH = 4
DN_ALPHA = (2.0 * DEPTH) ** 0.25
LN_EPS = 1e-5
NEG_INF = -1e30
LOG2E = math.log2(math.e)
LANE = 128
VMEM_LIMIT = 56 * 1024 * 1024
ATTN_TQ, ATTN_TK = 256, 512
MOE_TM = 512

MLA_HEADS, MLA_Q_RANK, MLA_KV_RANK, MLA_NOPE, MLA_ROPE, MLA_V = 16, 384, 256, 64, 32, 64
ROPE_THETA = 10000.0
NSA_HEADS, NSA_GROUPS, NSA_HD = 16, 4, 64
NSA_CMP_LEN, NSA_CMP_STRIDE, NSA_SLC_LEN, NSA_TOP_N, NSA_WINDOW, NSA_FORCE = 32, 16, 64, 8, 512, 1e4
DIFF_HEADS, DIFF_HD = 8, 64
DIL_PATTERNS = ((128, 1), (512, 4), (2048, 16))
DIL_HEADS, DIL_HD = 8, 64
MOE_GROUPS, MOE_EPG, MOE_EXPERTS, MOE_FF = 4, 4, 16, 512


def _cparams(*sem):
    return pltpu.CompilerParams(dimension_semantics=sem, vmem_limit_bytes=VMEM_LIMIT)


def _alibi(n):
    return np.asarray(2.0 ** (-8.0 * np.arange(1, n + 1) / n), np.float32)


def _linear_kernel(x_ref, w_ref, *out_refs, splits, chunk):
    xb = x_ref[...].astype(BF16)
    col = 0
    for o_ref, n in zip(out_refs, splits):
        for c0 in range(0, n, chunk):
            cw = min(chunk, n - c0)
            o_ref[:, c0:c0 + cw] = jnp.dot(
                xb, w_ref[:, col + c0:col + c0 + cw], preferred_element_type=F32).astype(o_ref.dtype)
        col += n


def _linear(x, w, splits, dtypes, tm=512, name="linear"):
    M, K = x.shape
    ntot = sum(splits)
    assert w.shape == (K, ntot) and M % tm == 0 and all(n % LANE == 0 for n in splits)
    outs = pl.pallas_call(
        functools.partial(_linear_kernel, splits=tuple(splits), chunk=512),
        out_shape=[jax.ShapeDtypeStruct((M, n), d) for n, d in zip(splits, dtypes)],
        grid=(M // tm,),
        in_specs=[pl.BlockSpec((tm, K), lambda i: (i, 0)),
                  pl.BlockSpec((K, ntot), lambda i: (0, 0))],
        out_specs=[pl.BlockSpec((tm, n), lambda i: (i, 0)) for n in splits],
        compiler_params=_cparams("parallel"),
        name=name,
    )(x, w)
    return outs


def _layer_norm_rows(z, g, b):
    mu = jnp.mean(z, axis=-1, keepdims=True)
    zc = z - mu
    var = jnp.mean(zc * zc, axis=-1, keepdims=True)
    return zc * lax.rsqrt(var + LN_EPS) * g + b


def _outproj_ln_kernel(o_ref, w_ref, x_ref, g_ref, b_ref, out_ref):
    y = jnp.dot(o_ref[...].astype(BF16), w_ref[...], preferred_element_type=F32)
    out_ref[...] = _layer_norm_rows(DN_ALPHA * x_ref[...] + y, g_ref[...], b_ref[...])


def _outproj_ln(o, w_o, x, g, b, tm=512, name="outproj_ln"):
    M, K = o.shape
    D = x.shape[1]
    return pl.pallas_call(
        _outproj_ln_kernel,
        out_shape=jax.ShapeDtypeStruct((M, D), F32),
        grid=(M // tm,),
        in_specs=[pl.BlockSpec((tm, K), lambda i: (i, 0)),
                  pl.BlockSpec((K, D), lambda i: (0, 0)),
                  pl.BlockSpec((tm, D), lambda i: (i, 0)),
                  pl.BlockSpec((1, D), lambda i: (0, 0)),
                  pl.BlockSpec((1, D), lambda i: (0, 0))],
        out_specs=pl.BlockSpec((tm, D), lambda i: (i, 0)),
        compiler_params=_cparams("parallel"),
        name=name,
    )(o, w_o.astype(BF16), x, g.reshape(1, D), b.reshape(1, D))


def _scores(q, k, bias):
    s = lax.dot_general(q, k, (((1,), (1,)), ((), ())), preferred_element_type=F32)
    return s if bias is None else s + bias


def _causal_item_list(n_q, tq, tk):
    items = []
    for i in range(n_q):
        c_d = (i * tq) // tk
        items += [(i, c, c == 0, c == c_d) for c in range(c_d + 1)]
    return tuple(items)


def _flat_scratch(n_streams, M, t):
    return [pltpu.VMEM((n_streams, 2, M, t), F32), pltpu.VMEM((n_streams, 2, M, t), BF16),
            pltpu.VMEM((n_streams, M, LANE), F32), pltpu.VMEM((n_streams, 2, 3, M, LANE), F32)]


def _flash_static(items, streams, finalize, scratch, tq, tk):
    s_ref, p_ref, acc_ref, st_ref = scratch
    ns = len(streams)
    reps = tk // LANE
    n_rep = s_ref.shape[2] // tq

    def scores(n, item):
        q_at, k_at, _, bias_at = streams[n]
        return _scores(q_at(item[0]), k_at(item[1]), None if bias_at is None else bias_at(item[1]))

    def lane_partial_sum(p):
        return functools.reduce(jnp.add, [p[:, r * LANE:(r + 1) * LANE] for r in range(reps)])

    def value_update(n, prev, slot_prev):
        pv = jnp.dot(p_ref[n, slot_prev], streams[n][2](prev[1]), preferred_element_type=F32)
        acc_ref[n] = pv if prev[2] else st_ref[n, slot_prev, 2] * acc_ref[n] + pv

    def finish(prev, slot_prev):
        finalize(prev[0], [(jnp.sum(st_ref[n, slot_prev, 1], axis=-1, keepdims=True), acc_ref[n]) for n in range(ns)])

    for n in range(ns):
        s_ref[n, 0] = scores(n, items[0])
    for j, item in enumerate(items):
        slot = j % 2
        i, c, first, last = item
        if j + 1 < len(items):
            for n in range(ns):
                s_ref[n, 1 - slot] = scores(n, items[j + 1])
        if j > 0:
            for n in range(ns):
                value_update(n, items[j - 1], 1 - slot)
        vis = None
        if (c + 1) * tk > i * tq + 1:
            vis = _rel_pos(tq, tk) + (c * tk - i * tq) <= 0
            if n_rep > 1:
                vis = jnp.concatenate([vis] * n_rep, axis=0)
        for n in range(ns):
            s = s_ref[n, slot]
            if vis is not None:
                s = jnp.where(vis, s, NEG_INF)
            m_cur = jnp.max(s, axis=-1, keepdims=True)
            if first:
                m_new = jnp.broadcast_to(m_cur, (s.shape[0], LANE))
            else:
                m_old = st_ref[n, 1 - slot, 0]
                m_new = jnp.maximum(m_old, m_cur)
                alpha = jnp.exp2(m_old - m_new)
                st_ref[n, slot, 2] = alpha
            p = jnp.exp2(s - jnp.concatenate([m_new] * reps, axis=1))
            st_ref[n, slot, 0] = m_new
            st_ref[n, slot, 1] = lane_partial_sum(p) if first else alpha * st_ref[n, 1 - slot, 1] + lane_partial_sum(p)
            p_ref[n, slot] = p.astype(BF16)
        if j > 0 and items[j - 1][3]:
            finish(items[j - 1], 1 - slot)
    last_slot = (len(items) - 1) % 2
    for n in range(ns):
        value_update(n, items[-1], last_slot)
    finish(items[-1], last_slot)


def _window_attention(q, ks, vs, viss, biases=None):
    ss = []
    for j, (k, vis) in enumerate(zip(ks, viss)):
        s = _scores(q, k, None if biases is None else biases[j])
        ss.append(s if vis is None else jnp.where(vis, s, NEG_INF))
    m = jnp.max(functools.reduce(jnp.maximum, ss), axis=-1, keepdims=True)
    ps = [jnp.exp2(s - m) for s in ss]
    l = jnp.sum(functools.reduce(jnp.add, ps), axis=-1, keepdims=True)
    acc = functools.reduce(jnp.add, [jnp.dot(p.astype(BF16), v, preferred_element_type=F32) for p, v in zip(ps, vs)])
    return m, l, acc


def _rel_pos(rows, cols):
    return (lax.broadcasted_iota(jnp.int32, (rows, cols), 1)
            - lax.broadcasted_iota(jnp.int32, (rows, cols), 0))


def _rms_rows(c, g, eps):
    return c * lax.rsqrt(jnp.mean(c * c, axis=-1, keepdims=True) + eps) * g


def _mla_proj_kernel(x_ref, win_ref, qn_ref, kvn_ref, wq_ref, wqs_ref, wk_ref, wv_ref, cos_ref, sin_ref,
                     q_out, k_out, v_out, *, scale):
    xb = x_ref[...].astype(BF16)
    c = jnp.dot(xb, win_ref[...], preferred_element_type=F32)
    r0, r1 = MLA_Q_RANK, MLA_Q_RANK + MLA_KV_RANK
    cq = _rms_rows(c[:, :r0], qn_ref[...], 1e-6).astype(BF16)
    ckv = _rms_rows(c[:, r0:r1], kvn_ref[...], 1e-6).astype(BF16)
    cos, sin = cos_ref[...], sin_ref[...]
    kr = c[:, r1:r1 + LANE] * cos + c[:, r1 + LANE:r1 + 2 * LANE] * sin
    v_out[...] = jnp.dot(ckv, wv_ref[...], preferred_element_type=F32).astype(v_out.dtype)
    cos2, sin2, kr2 = (jnp.concatenate([t, t], axis=1) for t in (cos, sin, kr))
    for hp in range(MLA_HEADS // 2):
        sl = slice(2 * hp * LANE, 2 * (hp + 1) * LANE)
        qh = jnp.dot(cq, wq_ref[:, sl], preferred_element_type=F32)
        qhs = jnp.dot(cq, wqs_ref[:, sl], preferred_element_type=F32)
        q_out[:, sl] = ((qh * cos2 + qhs * sin2) * scale).astype(q_out.dtype)
        kh = jnp.dot(ckv, wk_ref[:, sl], preferred_element_type=F32)
        k_out[:, sl] = (kh + kr2).astype(k_out.dtype)


def _mla_attn_kernel(q_ref, k_ref, v_ref, o_ref, *scratch, tq, tk, items):
    def rows(i, t):
        return pl.ds(i * t, t)

    def v_at(c):
        return v_ref[0, rows(c, tk), :]

    streams = []
    for hh in range(2):
        sl = slice(hh * LANE, (hh + 1) * LANE)
        streams.append((lambda i, sl=sl: q_ref[0, rows(i, tq), sl], lambda c, sl=sl: k_ref[0, rows(c, tk), sl],
                        v_at, None))
    lane = lax.broadcasted_iota(jnp.int32, (tq, LANE), 1)

    def finalize(i, res):
        (la, acca), (lb, accb) = res
        o_ref[0, rows(i, tq), :] = jnp.where(lane < MLA_V, acca / la, accb / lb).astype(o_ref.dtype)

    _flash_static(items, streams, finalize, scratch, tq, tk)


def _mla_mixer(x, w_in, q_norm, kv_norm, w_qb, w_kvb, w_o, ln_g, ln_b, B, S):
    N, D = x.shape
    H, dq = MLA_HEADS, MLA_NOPE + MLA_ROPE
    half = MLA_ROPE // 2
    r0, r1 = MLA_Q_RANK, MLA_Q_RANK + MLA_KV_RANK
    z = lambda *s: jnp.zeros(s, F32)
    swap = lambda a: jnp.concatenate([a[..., half:], a[..., :half]], axis=-1)
    kr_w = w_in[:, r1:]
    win = jnp.concatenate([w_in[:, :r1],
                           z(D, MLA_NOPE), kr_w, z(D, LANE - dq),
                           z(D, MLA_NOPE), swap(kr_w), z(D, LANE - dq)], axis=1).astype(BF16)
    wq3 = w_qb.reshape(r0, H, dq)
    wq = jnp.concatenate([wq3, z(r0, H, LANE - dq)], axis=-1).reshape(r0, H * LANE).astype(BF16)
    wqs = jnp.concatenate([z(r0, H, MLA_NOPE), swap(wq3[..., MLA_NOPE:]), z(r0, H, LANE - dq)],
                          axis=-1).reshape(r0, H * LANE).astype(BF16)
    wkv3 = w_kvb.reshape(MLA_KV_RANK, H, MLA_NOPE + MLA_V)
    wk = jnp.concatenate([wkv3[..., :MLA_NOPE], z(MLA_KV_RANK, H, LANE - MLA_NOPE)],
                         axis=-1).reshape(MLA_KV_RANK, H * LANE).astype(BF16)
    wv = wkv3[..., MLA_NOPE:].reshape(MLA_KV_RANK, H * MLA_V).astype(BF16)
    freq = ROPE_THETA ** (-jnp.arange(half, dtype=F32) / half)
    ang = jnp.arange(S, dtype=F32)[:, None] * freq
    cos, sin = jnp.cos(ang), jnp.sin(ang)
    ones, zer = jnp.ones((S, MLA_NOPE), F32), jnp.zeros((S, LANE - dq), F32)
    cos_t = jnp.concatenate([ones, cos, cos, zer], axis=1)
    sin_t = jnp.concatenate([0 * ones, -sin, sin, zer], axis=1)

    tm = 256
    nwin = win.shape[1]
    q, k, v = pl.pallas_call(
        functools.partial(_mla_proj_kernel, scale=dq ** -0.5 * LOG2E),
        out_shape=[jax.ShapeDtypeStruct((N, H * LANE), BF16),
                   jax.ShapeDtypeStruct((N, H * LANE), BF16),
                   jax.ShapeDtypeStruct((N, H * MLA_V), BF16)],
        grid=(N // tm,),
        in_specs=[pl.BlockSpec((tm, D), lambda i: (i, 0)),
                  pl.BlockSpec((D, nwin), lambda i: (0, 0)),
                  pl.BlockSpec((1, r0), lambda i: (0, 0)),
                  pl.BlockSpec((1, MLA_KV_RANK), lambda i: (0, 0)),
                  pl.BlockSpec((r0, H * LANE), lambda i: (0, 0)),
                  pl.BlockSpec((r0, H * LANE), lambda i: (0, 0)),
                  pl.BlockSpec((MLA_KV_RANK, H * LANE), lambda i: (0, 0)),
                  pl.BlockSpec((MLA_KV_RANK, H * MLA_V), lambda i: (0, 0)),
                  pl.BlockSpec((tm, LANE), lambda i: (i % (S // tm), 0)),
                  pl.BlockSpec((tm, LANE), lambda i: (i % (S // tm), 0))],
        out_specs=[pl.BlockSpec((tm, H * LANE), lambda i: (i, 0)),
                   pl.BlockSpec((tm, H * LANE), lambda i: (i, 0)),
                   pl.BlockSpec((tm, H * MLA_V), lambda i: (i, 0))],
        compiler_params=_cparams("parallel"),
        name="mla_proj",
    )(x, win, q_norm.reshape(1, r0), kv_norm.reshape(1, MLA_KV_RANK), wq, wqs, wk, wv, cos_t, sin_t)

    tq, tk = ATTN_TQ, ATTN_TK
    o = pl.pallas_call(
        functools.partial(_mla_attn_kernel, tq=tq, tk=tk, items=_causal_item_list(S // tq, tq, tk)),
        out_shape=jax.ShapeDtypeStruct((B, S, H * MLA_V), BF16),
        grid=(B, H // 2),
        in_specs=[pl.BlockSpec((1, S, 2 * LANE), lambda b, h: (b, 0, h)),
                  pl.BlockSpec((1, S, 2 * LANE), lambda b, h: (b, 0, h)),
                  pl.BlockSpec((1, S, LANE), lambda b, h: (b, 0, h))],
        out_specs=pl.BlockSpec((1, S, LANE), lambda b, h: (b, 0, h)),
        scratch_shapes=_flat_scratch(2, tq, tk),
        compiler_params=_cparams("parallel", "parallel"),
        name="mla_attn",
    )(q.reshape(B, S, H * LANE), k.reshape(B, S, H * LANE), v.reshape(B, S, H * MLA_V))
    return _outproj_ln(o.reshape(N, H * MLA_V), w_o, x, ln_g, ln_b, name="mla_out_ln")


def _diff_attn_kernel(slope_ref, lam_ref, sub_ref, q_ref, k_ref, v_ref, o_ref, *scratch, tq, tk, items, lam_init):
    slope = slope_ref[pl.program_id(1)] * LOG2E
    lv = lam_ref[...]
    lam = (jnp.exp(jnp.sum(lv[0:1] * lv[1:2], axis=-1, keepdims=True))
           - jnp.exp(jnp.sum(lv[2:3] * lv[3:4], axis=-1, keepdims=True)) + lam_init)
    lane = lax.broadcasted_iota(jnp.int32, (tq, LANE), 1)
    colf = lax.broadcasted_iota(jnp.int32, (1, tk), 1).astype(F32)

    def rows(i, t):
        return pl.ds(i * t, t)

    def q_at(i, first_map):
        qf = q_ref[0, rows(i, tq), :].astype(F32) * (DIFF_HD ** -0.5 * LOG2E)
        return jnp.where((lane < DIFF_HD) == first_map, qf, 0.0).astype(BF16)

    def k_at(c):
        return k_ref[0, rows(c, tk), :]

    def v_at(c):
        return v_ref[0, rows(c, tk), :]

    def bias_at(c):
        return slope * (colf + float(c * tk))

    def finalize(i, res):
        (l0, acc0), (l1, acc1) = res
        a = _rms_rows(acc0 / l0 - lam * (acc1 / l1), sub_ref[...], 1e-5) * (1.0 - lam_init)
        o_ref[0, rows(i, tq), :] = a.astype(o_ref.dtype)

    streams = [(functools.partial(q_at, first_map=fm), k_at, v_at, bias_at) for fm in (True, False)]
    _flash_static(items, streams, finalize, scratch, tq, tk)


def _diff_mixer(x, w_in, lam_q1, lam_k1, lam_q2, lam_k2, subln, w_o, layer_idx, ln_g, ln_b, B, S):
    N, D = x.shape
    H, d = DIFF_HEADS, DIFF_HD
    nq = H * 2 * d
    (qkv,) = _linear(x, w_in.astype(BF16), [3 * nq], [BF16], name="diff_proj")
    qkv = qkv.reshape(B, S, 3 * nq)
    lam_init = 0.8 - 0.6 * math.exp(-0.3 * layer_idx)
    lamv = jnp.zeros((8, LANE), F32).at[:4, :d].set(jnp.stack([lam_q1, lam_k1, lam_q2, lam_k2]).astype(F32))
    tq, tk = ATTN_TQ, ATTN_TK
    o = pl.pallas_call(
        functools.partial(_diff_attn_kernel, tq=tq, tk=tk, items=_causal_item_list(S // tq, tq, tk), lam_init=lam_init),
        out_shape=jax.ShapeDtypeStruct((B, S, nq), BF16),
        grid=(B, H),
        in_specs=[pl.BlockSpec(memory_space=pltpu.SMEM),
                  pl.BlockSpec((8, LANE), lambda b, h: (0, 0)),
                  pl.BlockSpec((1, 2 * d), lambda b, h: (0, 0)),
                  pl.BlockSpec((1, S, LANE), lambda b, h: (b, 0, h)),
                  pl.BlockSpec((1, S, LANE), lambda b, h: (b, 0, H + h)),
                  pl.BlockSpec((1, S, LANE), lambda b, h: (b, 0, 2 * H + h))],
        out_specs=pl.BlockSpec((1, S, LANE), lambda b, h: (b, 0, h)),
        scratch_shapes=_flat_scratch(2, tq, tk),
        compiler_params=_cparams("parallel", "parallel"),
        name="diff_attn",
    )(jnp.asarray(_alibi(H)), lamv, subln.reshape(1, 2 * d).astype(F32), qkv, qkv, qkv)
    return _outproj_ln(o.reshape(N, nq), w_o, x, ln_g, ln_b, name="diff_out_ln")


SEL_LANE0 = NSA_HD
SEL_OFF = 1e30
ALIBI_LANE0 = 96
POS_SPLIT = 64


def _nsa_compress_kernel(ak_ref, av_ref, plo_ref, phi_ref, wklo_ref, wkhi_ref, wvlo_ref, wvhi_ref, kc_out, vc_out):
    def one(a_ref, wlo_ref, whi_ref, out):
        a = a_ref[0]
        lo = jnp.dot((a + plo_ref[...]).astype(BF16), wlo_ref[...], preferred_element_type=F32)
        hi = jnp.dot((a + phi_ref[...]).astype(BF16), whi_ref[...], preferred_element_type=F32)
        out[0] = (lo + pltpu.roll(hi, hi.shape[0] - 1, 0)).astype(out.dtype)

    one(ak_ref, wklo_ref, wkhi_ref, kc_out)
    one(av_ref, wvlo_ref, wvhi_ref, vc_out)


def _pack_heads(o, tq):
    lane = lax.broadcasted_iota(jnp.int32, (tq, LANE), 1)
    p01 = jnp.where(lane < NSA_HD, o[0:tq], o[tq:2 * tq])
    p23 = jnp.where(lane < NSA_HD, o[2 * tq:3 * tq], o[3 * tq:4 * tq])
    return jnp.concatenate([p01, p23], axis=1)


def _nsa_attn_kernel(q_ref, qc_ref, gl_ref, kc_ref, vc_ref, ks_ref, vs_ref, kw_ref, vw_ref, ktab_ref, ctab_ref,
                     c2s_ref, rep_ref, o_ref, selb_ref, ocw_ref, gs_ref, *scratch, tq, tk, tqs, tks, n_slc, items):
    hpg = NSA_HEADS // NSA_GROUPS

    def head_q(rows, j):
        return q_ref[0, rows, j * LANE:(j + 1) * LANE] + jnp.broadcast_to(qc_ref[0, j:j + 1, :],
                                                                         (rows.size, LANE)).astype(BF16)

    def tile(i, carry):
        _nsa_tile(i, head_q, gl_ref, kc_ref, vc_ref, kw_ref, vw_ref, ktab_ref, ctab_ref, c2s_ref, rep_ref,
                  selb_ref, ocw_ref, gs_ref, tq=tq, tk=tk, n_slc=n_slc)
        return carry

    lax.fori_loop(0, q_ref.shape[1] // tq, tile, 0, unroll=2)

    def rows_s(i, t):
        return pl.ds(i * t, t)

    def q_at(i, pair):
        r = rows_s(i, tqs)
        return jnp.concatenate([head_q(r, j) + selb_ref[r, :] for j in (2 * pair, 2 * pair + 1)], axis=0)

    def ks_at(c):
        return ks_ref[0, rows_s(c, tks), :] + ktab_ref[rows_s(c, tks), :]

    def vs_at(c):
        return vs_ref[0, rows_s(c, tks), :]

    def finalize(i, res):
        r = rows_s(i, tqs)
        o_s = jnp.concatenate([acc / l for l, acc in res], axis=0)
        o_ref[0, r, :] = (ocw_ref[r, :] + gs_ref[r, :] * _pack_heads(o_s, tqs)).astype(o_ref.dtype)

    streams = [(functools.partial(q_at, pair=pr), ks_at, vs_at, None) for pr in range(hpg // 2)]
    _flash_static(items, streams, finalize, scratch, tqs, tks)


def _nsa_tile(i, head_q, gl_ref, kc_ref, vc_ref, kw_ref, vw_ref, ktab_ref, ctab_ref, c2s_ref, rep_ref,
              selb_ref, ocw_ref, gs_ref, *, tq, tk, n_slc):
    t0 = i * tq
    rows = pl.ds(pl.multiple_of(t0, tq), tq)
    hpg = NSA_HEADS // NSA_GROUPS
    M = hpg * tq
    q = jnp.concatenate([head_q(rows, j) for j in range(hpg)], axis=0)
    trow1 = t0 + lax.broadcasted_iota(jnp.int32, (tq, 1), 0)
    trow = jnp.concatenate([trow1] * hpg, axis=0)
    lane_m = lax.broadcasted_iota(jnp.int32, (M, LANE), 1)

    sc = lax.dot_general(q, kc_ref[0] + ctab_ref[...], (((1,), (1,)), ((), ())), preferred_element_type=F32)
    sc = jnp.where(trow >= lane_m * NSA_CMP_STRIDE + (NSA_CMP_LEN - 1), sc, NEG_INF)
    e = jnp.exp2(sc - jnp.max(sc, axis=-1, keepdims=True))
    p_c = jnp.where(trow >= NSA_CMP_LEN - 1, e / jnp.sum(e, axis=-1, keepdims=True), 0.0)
    o_c = jnp.dot(p_c.astype(BF16), vc_ref[0], preferred_element_type=F32)
    psum = p_c[0:tq]
    for j in range(1, hpg):
        psum = psum + p_c[j * tq:(j + 1) * tq]
    ph, plw = _split_bf16(psum)
    imp = (jnp.dot(ph, c2s_ref[...], preferred_element_type=F32)
           + jnp.dot(plw, c2s_ref[...], preferred_element_type=F32))

    lane = lax.broadcasted_iota(jnp.int32, (tq, LANE), 1)
    sidx = lane - SEL_LANE0
    valid = (sidx >= 0) & (sidx < n_slc)
    forced = (sidx == 0) | (sidx == trow1 // NSA_SLC_LEN)
    future = sidx * NSA_SLC_LEN > trow1
    score = imp + jnp.where(forced, NSA_FORCE, 0.0) - jnp.where(future, 2.0 * NSA_FORCE, 0.0)
    score = jnp.where(valid, score, -jnp.inf)
    n_rows = ALIBI_LANE0 - SEL_LANE0
    st = score.T[SEL_LANE0:ALIBI_LANE0]
    blk = lax.broadcasted_iota(jnp.int32, (n_rows, tq), 0)
    sel = jnp.zeros((n_rows, tq), jnp.bool_)
    for _ in range(min(NSA_TOP_N, n_slc)):
        mx = jnp.max(st, axis=0, keepdims=True)
        pick = blk == jnp.min(jnp.where(st == mx, blk, n_rows), axis=0, keepdims=True)
        sel = sel | pick
        st = jnp.where(pick, -jnp.inf, st)
    off = jnp.where(sel | (blk >= n_slc), 0.0, -SEL_OFF)
    selbias = jnp.concatenate([jnp.zeros((SEL_LANE0, tq), F32), off, jnp.zeros((LANE - ALIBI_LANE0, tq), F32)],
                              axis=0).T.astype(BF16)

    selb_ref[rows, :] = selbias

    c_d = t0 // tk
    d_diag = jnp.concatenate([_rel_pos(tq, tk)] * hpg, axis=0) + (c_d * tk - t0)

    def chunk(c):
        return pl.ds(pl.multiple_of(c * tk, tk), tk)

    cw = [jnp.maximum(c_d - 2, 0), jnp.maximum(c_d - 1, 0), c_d]
    vis_w = [(d_diag - 2 * tk > -NSA_WINDOW) & (c_d >= 2), jnp.broadcast_to(c_d >= 1, d_diag.shape), d_diag <= 0]
    _, l_w, acc_w = _window_attention(q, [kw_ref[0, chunk(c), :] + ktab_ref[chunk(c), :] for c in cw],
                                      [vw_ref[0, chunk(c), :] for c in cw], vis_w)
    o_w = acc_w / l_w

    gh, glw = _split_bf16(jax.nn.sigmoid(gl_ref[0, rows, :]))
    gr = jnp.dot(gh, rep_ref[0], preferred_element_type=F32) + jnp.dot(glw, rep_ref[0], preferred_element_type=F32)
    w = hpg * NSA_HD
    ocw_ref[rows, :] = gr[:, 0:w] * _pack_heads(o_c, tq) + gr[:, 2 * w:3 * w] * _pack_heads(o_w, tq)
    gs_ref[rows, :] = gr[:, w:2 * w]


def _nsa_mixer(x, w_in, w_phi_k, w_phi_v, cmp_pos, w_o, ln_g, ln_b, B, S):
    N, D = x.shape
    H, G, d = NSA_HEADS, NSA_GROUPS, NSA_HD
    hpg = H // G
    L, st = NSA_CMP_LEN, NSA_CMP_STRIDE
    n_slc = S // NSA_SLC_LEN
    assert S % 256 == 0 and n_slc <= ALIBI_LANE0 - SEL_LANE0 and L == 2 * st
    cuts = [H * d + i * G * d for i in range(7)]
    wq, wkc, wvc, wks, wvs, wkw, wvw, wgl = jnp.split(w_in, cuts, axis=1)
    z = lambda *s: jnp.zeros(s, F32)
    pad_heads = lambda w, n: jnp.concatenate([w.reshape(D, n, d), z(D, n, LANE - d)], axis=-1).reshape(D, n * LANE)
    dup_heads = lambda w, n: jnp.concatenate([w.reshape(D, n, d)] * 2, axis=-1).reshape(D, n * LANE)
    wgl_p = jnp.concatenate([wgl, z(D, LANE - wgl.shape[1])], axis=1)
    wcat = jnp.concatenate([pad_heads(wq * (d ** -0.5 * LOG2E), H), wkc, wvc, pad_heads(wks, G), dup_heads(wvs, G),
                            pad_heads(wkw, G), dup_heads(wvw, G), wgl_p], axis=1).astype(BF16)
    q, kc, vc, ks, vs, kw, vw, gl = _linear(
        x, wcat, [H * LANE, G * d, G * d, G * LANE, G * LANE, G * LANE, G * LANE, LANE],
        [BF16, F32, F32, BF16, BF16, BF16, BF16, F32], name="nsa_proj")

    nrow = S // st
    eye = jnp.eye(G, dtype=F32)

    def phi_w(w_phi, half, dup):
        w = w_phi.reshape(L, d, d)[half * st:(half + 1) * st]
        wd = jnp.concatenate([w, w if dup else jnp.zeros_like(w)], axis=-1)
        return jnp.einsum('ldc,gh->lgdhc', wd, eye).reshape(st * G * d, G * LANE).astype(BF16)

    pos = lambda half: jnp.broadcast_to(cmp_pos[half * st:(half + 1) * st, None, :], (st, G, d)).reshape(1, st * G * d)
    wide = st * G * d
    cspec = pl.BlockSpec((1, nrow, wide), lambda b: (b, 0, 0))
    wspec = pl.BlockSpec((wide, G * LANE), lambda b: (0, 0))
    pspec = pl.BlockSpec((1, wide), lambda b: (0, 0))
    ospec = pl.BlockSpec((1, nrow, G * LANE), lambda b: (b, 0, 0))
    kcmp, vcmp = pl.pallas_call(
        _nsa_compress_kernel,
        out_shape=[jax.ShapeDtypeStruct((B, nrow, G * LANE), BF16)] * 2,
        grid=(B,),
        in_specs=[cspec, cspec, pspec, pspec, wspec, wspec, wspec, wspec],
        out_specs=[ospec, ospec],
        compiler_params=_cparams("parallel"),
        name="nsa_compress",
    )(kc.reshape(B, nrow, wide), vc.reshape(B, nrow, wide), pos(0), pos(1),
      phi_w(w_phi_k, 0, False), phi_w(w_phi_k, 1, False), phi_w(w_phi_v, 0, True), phi_w(w_phi_v, 1, True))

    def pos_lanes(tab, pos):
        tab[:, ALIBI_LANE0:ALIBI_LANE0 + 3] = (pos // POS_SPLIT)[:, None]
        tab[:, ALIBI_LANE0 + 3:ALIBI_LANE0 + 6] = (pos % POS_SPLIT)[:, None]
        return tab

    ktab = np.zeros((S, LANE), np.float32)
    ktab[np.arange(S), SEL_LANE0 + np.arange(S) // NSA_SLC_LEN] = 1.0
    ktab = pos_lanes(ktab, np.arange(S))
    ctab = pos_lanes(np.zeros((nrow, LANE), np.float32), np.arange(nrow) * st + (L - 1))
    s2 = jnp.asarray(_alibi(H) * LOG2E, F32)
    parts = []
    for _ in range(3):
        part = s2.astype(BF16).astype(F32)
        parts.append(part)
        s2 = s2 - part
    qcoef = jnp.zeros((H, LANE), F32).at[:, ALIBI_LANE0:ALIBI_LANE0 + 6].set(
        jnp.stack([POS_SPLIT * p for p in parts] + parts, axis=1)).reshape(G, hpg, LANE)
    cmp_start = np.arange(nrow) * st
    slc_start = np.arange(n_slc) * NSA_SLC_LEN
    ov = (cmp_start[:, None] < slc_start[None, :] + NSA_SLC_LEN) & (cmp_start[:, None] + L > slc_start[None, :])
    ov[(S - L) // st + 1:] = False
    c2s = np.zeros((nrow, LANE), np.float32)
    c2s[:, SEL_LANE0:SEL_LANE0 + n_slc] = ov
    rep = np.zeros((G, LANE, 3 * hpg * d), np.float32)
    for g in range(G):
        for j in range(hpg):
            for br in range(3):
                rep[g, (g * hpg + j) * 3 + br, br * hpg * d + j * d:br * hpg * d + (j + 1) * d] = 1.0

    tq, tk = 128, 256
    tqs, tks = ATTN_TQ, ATTN_TK
    assert NSA_WINDOW == 2 * tk and tk % tq == 0
    kvspec = pl.BlockSpec((1, S, LANE), lambda b, g: (b, 0, g))
    cmpspec = pl.BlockSpec((1, nrow, LANE), lambda b, g: (b, 0, g))
    o = pl.pallas_call(
        functools.partial(_nsa_attn_kernel, tq=tq, tk=tk, tqs=tqs, tks=tks, n_slc=n_slc,
                          items=_causal_item_list(S // tqs, tqs, tks)),
        out_shape=jax.ShapeDtypeStruct((B, S, H * d), BF16),
        grid=(B, G),
        in_specs=[pl.BlockSpec((1, S, hpg * LANE), lambda b, g: (b, 0, g)),
                  pl.BlockSpec((1, hpg, LANE), lambda b, g: (g, 0, 0)),
                  pl.BlockSpec((1, S, LANE), lambda b, g: (b, 0, 0)),
                  cmpspec, cmpspec, kvspec, kvspec, kvspec, kvspec,
                  pl.BlockSpec((S, LANE), lambda b, g: (0, 0)),
                  pl.BlockSpec((nrow, LANE), lambda b, g: (0, 0)),
                  pl.BlockSpec((nrow, LANE), lambda b, g: (0, 0)),
                  pl.BlockSpec((1, LANE, 3 * hpg * d), lambda b, g: (g, 0, 0))],
        out_specs=pl.BlockSpec((1, S, hpg * d), lambda b, g: (b, 0, g)),
        scratch_shapes=[pltpu.VMEM((S, LANE), BF16), pltpu.VMEM((S, hpg * d), F32), pltpu.VMEM((S, hpg * d), F32)]
        + _flat_scratch(hpg // 2, 2 * tqs, tks),
        compiler_params=_cparams("parallel", "parallel"),
        name="nsa_attn",
    )(q.reshape(B, S, H * LANE), qcoef, gl.reshape(B, S, LANE), kcmp, vcmp,
      ks.reshape(B, S, G * LANE), vs.reshape(B, S, G * LANE), kw.reshape(B, S, G * LANE), vw.reshape(B, S, G * LANE),
      jnp.asarray(ktab, BF16), jnp.asarray(ctab, BF16), jnp.asarray(c2s, BF16), jnp.asarray(rep, BF16))
    return _outproj_ln(o.reshape(N, H * d), w_o, x, ln_g, ln_b, name="nsa_out_ln")


def _dil_attn_kernel(slope_ref, q_ref, k_ref, v_ref, o_ref, lse_ref, *, t, dil):
    i = pl.program_id(2)
    d = DIL_HD
    rel = _rel_pos(t, t)
    relf = rel.astype(F32)
    rel2 = jnp.concatenate([rel, rel], axis=0)
    lane = lax.broadcasted_iota(jnp.int32, (t, LANE), 1)
    scale = d ** -0.5 * LOG2E

    def chunk(c):
        return pl.ds(pl.multiple_of(c * t, t), t)

    cs = [jnp.maximum(i - 1, 0), i]
    viss = [(rel2 >= 0) & (i >= 1), rel2 <= 0]
    lse_t = jnp.zeros((t, LANE), F32)
    for hp in range(DIL_HEADS // 2):
        sl = slice(hp * LANE, (hp + 1) * LANE)
        qf = q_ref[0, :, sl].astype(F32) * scale
        q2 = jnp.concatenate([jnp.where(lane < d, qf, 0.0), jnp.where(lane < d, 0.0, qf)], axis=0).astype(BF16)
        sa, sb = slope_ref[2 * hp] * (dil * LOG2E), slope_ref[2 * hp + 1] * (dil * LOG2E)
        bias = jnp.concatenate([sa * relf, sb * relf], axis=0)
        shift = jnp.concatenate([jnp.full((t, 1), sa * t, F32), jnp.full((t, 1), sb * t, F32)], axis=0)
        m, l, acc = _window_attention(q2, [k_ref[0, chunk(c), sl] for c in cs], [v_ref[0, chunk(c), sl] for c in cs],
                                      viss, [bias - shift, bias])
        o = acc / l
        o_ref[0, :, hp * LANE:(hp + 1) * LANE] = jnp.where(lane < d, o[:t], o[t:]).astype(o_ref.dtype)
        lse = m + jnp.log2(l)
        lse_t = jnp.where(lane == 2 * hp, lse[:t], lse_t)
        lse_t = jnp.where(lane == 2 * hp + 1, lse[t:], lse_t)
    lse_ref[0] = lse_t


def _dil_proj_kernel(x_ref, w_ref, *refs, tm):
    outs, scr = refs[:-1], refs[-1]
    xb = x_ref[...].astype(BF16)
    width = scr.shape[0] * LANE
    for p, (o_ref, (_, dil)) in enumerate(zip(outs, DIL_PATTERNS)):
        for c0 in range(0, width, 512):
            y = jnp.dot(xb, w_ref[:, p * width + c0:p * width + c0 + 512], preferred_element_type=F32)
            if dil == 1:
                o_ref[:, c0:c0 + 512] = y.astype(o_ref.dtype)
            else:
                for j in range(512 // LANE):
                    scr[c0 // LANE + j] = y[:, j * LANE:(j + 1) * LANE]
        for c in range(dil if dil > 1 else 0):
            for j in range(width // LANE):
                o_ref[:, c * width + j * LANE:c * width + (j + 1) * LANE] = (
                    scr[j, pl.ds(c, tm // dil, stride=dil), :].astype(o_ref.dtype))


def _dil_merge_ln_kernel(o0_ref, o1_ref, o2_ref, l0_ref, l1_ref, l2_ref, rep_ref, w_ref, x_ref, g_ref, b_ref, out_ref,
                         scr_o, scr_l):
    tm = x_ref.shape[0]

    def token_order(ref, dil, scr):
        if dil == 1:
            return ref[...].astype(F32)
        w = ref.shape[1] // dil
        for c in range(dil):
            for j in range(w // LANE):
                scr[j, pl.ds(c, tm // dil, stride=dil), :] = (
                    ref[:, c * w + j * LANE:c * w + (j + 1) * LANE].astype(F32))
        return jnp.concatenate([scr[j] for j in range(w // LANE)], axis=1)

    ls = [token_order(l_ref, dil, scr_l) for l_ref, (_, dil) in zip((l0_ref, l1_ref, l2_ref), DIL_PATTERNS)]
    mx = jnp.maximum(jnp.maximum(ls[0], ls[1]), ls[2])
    es = [jnp.exp2(v - mx) for v in ls]
    tot = es[0] + es[1] + es[2]
    o = None
    for e, o_ref, (_, dil) in zip(es, (o0_ref, o1_ref, o2_ref), DIL_PATTERNS):
        wh, wl = _split_bf16(e / tot)
        wrep = jnp.dot(wh, rep_ref[...], preferred_element_type=F32) + jnp.dot(wl, rep_ref[...], preferred_element_type=F32)
        term = wrep * token_order(o_ref, dil, scr_o)
        o = term if o is None else o + term
    y = jnp.dot(o.astype(BF16), w_ref[...], preferred_element_type=F32)
    out_ref[...] = _layer_norm_rows(DN_ALPHA * x_ref[...] + y, g_ref[...], b_ref[...])


def _dil_mixer(x, w_in, w_o, ln_g, ln_b, B, S):
    N, D = x.shape
    H, d = DIL_HEADS, DIL_HD
    n_pat = len(DIL_PATTERNS)
    hd = H * d
    width = 3 * hd
    ncol = n_pat * width
    tm = 512
    qkvs = pl.pallas_call(
        functools.partial(_dil_proj_kernel, tm=tm),
        out_shape=[jax.ShapeDtypeStruct((N // dil, dil * width), BF16) for _, dil in DIL_PATTERNS],
        grid=(N // tm,),
        in_specs=[pl.BlockSpec((tm, D), lambda i: (i, 0)), pl.BlockSpec((D, ncol), lambda i: (0, 0))],
        out_specs=[pl.BlockSpec((tm // dil, dil * width), lambda i: (i, 0)) for _, dil in DIL_PATTERNS],
        scratch_shapes=[pltpu.VMEM((width // LANE, tm, LANE), F32)],
        compiler_params=_cparams("parallel"),
        name="dil_proj",
    )(x, w_in.astype(BF16))
    tq = 128
    slopes = jnp.asarray(_alibi(H))
    outs, lses = [], []
    for p, (win, dil) in enumerate(DIL_PATTERNS):
        ls = S // dil
        assert ls % tq == 0 and win == dil * tq
        view = qkvs[p].reshape(B, ls, dil * width)
        o, lse = pl.pallas_call(
            functools.partial(_dil_attn_kernel, t=tq, dil=dil),
            out_shape=[jax.ShapeDtypeStruct((B, ls, dil * hd), BF16),
                       jax.ShapeDtypeStruct((B, ls, dil * LANE), F32)],
            grid=(B, dil, ls // tq),
            in_specs=[pl.BlockSpec(memory_space=pltpu.SMEM),
                      pl.BlockSpec((1, tq, hd), lambda b, r, i: (b, i, 3 * r)),
                      pl.BlockSpec((1, ls, hd), lambda b, r, i: (b, 0, 3 * r + 1)),
                      pl.BlockSpec((1, ls, hd), lambda b, r, i: (b, 0, 3 * r + 2))],
            out_specs=[pl.BlockSpec((1, tq, hd), lambda b, r, i: (b, i, r)),
                       pl.BlockSpec((1, tq, LANE), lambda b, r, i: (b, i, r))],
            compiler_params=_cparams("parallel", "parallel", "arbitrary"),
            name=f"dil_attn_{p}",
        )(slopes, view, view, view)
        outs.append(o.reshape(N // dil, dil * hd))
        lses.append(lse.reshape(N // dil, dil * LANE))

    rep = np.zeros((LANE, hd), np.float32)
    for h in range(H):
        rep[h, h * d:(h + 1) * d] = 1.0
    row = lambda n: pl.BlockSpec((tm, n), lambda i: (i, 0))
    rowp = lambda n: [pl.BlockSpec((tm // dil, dil * n), lambda i: (i, 0)) for _, dil in DIL_PATTERNS]
    full = lambda a, b: pl.BlockSpec((a, b), lambda i: (0, 0))
    return pl.pallas_call(
        _dil_merge_ln_kernel,
        out_shape=jax.ShapeDtypeStruct((N, D), F32),
        grid=(N // tm,),
        in_specs=rowp(hd) + rowp(LANE) + [full(LANE, hd), full(hd, D), row(D), full(1, D), full(1, D)],
        out_specs=row(D),
        scratch_shapes=[pltpu.VMEM((hd // LANE, tm, LANE), F32), pltpu.VMEM((1, tm, LANE), F32)],
        compiler_params=_cparams("parallel"),
        name="dil_merge_out_ln",
    )(*outs, *lses, jnp.asarray(rep, BF16), w_o.astype(BF16), x, ln_g.reshape(1, D), ln_b.reshape(1, D))


def _split_bf16(a):
    hi = a.astype(BF16)
    return hi, (a - hi.astype(F32)).astype(BF16)


META_BKT, META_RANK = 2, 3
MOE_PAIRS = MOE_EPG * (MOE_EPG - 1) // 2
MOE_BUCKETS = MOE_GROUPS * MOE_PAIRS
PAIR_LO = [a for a in range(MOE_EPG) for b in range(a + 1, MOE_EPG)]
PAIR_HI = [b for a in range(MOE_EPG) for b in range(a + 1, MOE_EPG)]


def _router_kernel(x_ref, wh_ref, wl_ref, b_ref, tri_ref, meta_ref, cnt_ref, run_ref):
    @pl.when(pl.program_id(0) == 0)
    def _():
        run_ref[...] = jnp.zeros_like(run_ref)

    xh, xl = _split_bf16(x_ref[...])
    wh, wl = wh_ref[...], wl_ref[...]
    logits = (jnp.dot(xh, wh, preferred_element_type=F32) + jnp.dot(xl, wh, preferred_element_type=F32)
              + jnp.dot(xh, wl, preferred_element_type=F32)) + b_ref[...]
    tm = logits.shape[0]
    lane = lax.broadcasted_iota(jnp.int32, (tm, LANE), 1)
    big = jnp.int32(LANE)
    lg = jnp.where(lane < MOE_GROUPS, logits, -jnp.inf)
    mg = jnp.max(lg, axis=-1, keepdims=True)
    sg = jnp.sum(jnp.exp(lg - mg), axis=-1, keepdims=True)
    pg_top = 1.0 / sg
    g_top = jnp.min(jnp.where(lg == mg, lane, big), axis=-1, keepdims=True)
    e_lo = MOE_GROUPS + MOE_EPG * g_top
    in_grp = (lane >= e_lo) & (lane < e_lo + MOE_EPG)
    le = jnp.where(in_grp, logits, -jnp.inf)
    me = jnp.max(le, axis=-1, keepdims=True)
    ee = jnp.exp(le - me)
    se = jnp.sum(ee, axis=-1, keepdims=True)
    pe = jnp.where(in_grp, ee / se, -1.0)
    v1 = jnp.max(pe, axis=-1, keepdims=True)
    i1 = jnp.min(jnp.where(pe == v1, lane, big), axis=-1, keepdims=True)
    pe2 = jnp.where(lane == i1, -1.0, pe)
    v2 = jnp.max(pe2, axis=-1, keepdims=True)
    i2 = jnp.min(jnp.where(pe2 == v2, lane, big), axis=-1, keepdims=True)
    tot = v1 + v2
    a1, a2 = i1 - e_lo, i2 - e_lo
    lo, hi = jnp.minimum(a1, a2), jnp.maximum(a1, a2)
    pair = lo * (MOE_EPG - 1) - (lo * (lo - 1)) // 2 + hi - lo - 1
    bucket = g_top * MOE_PAIRS + pair
    g_first, g_second = (v1 / tot) * pg_top, (v2 / tot) * pg_top
    gates = (jnp.where(lane == 0, jnp.where(a1 < a2, g_first, g_second), 0.0)
             + jnp.where(lane == 1, jnp.where(a1 < a2, g_second, g_first), 0.0))
    onehot = lane == bucket
    prefix = jnp.dot(tri_ref[...], jnp.where(onehot, 1.0, 0.0).astype(BF16), preferred_element_type=F32)
    rank = jnp.sum(jnp.where(onehot, prefix + run_ref[...] - 1.0, 0.0), axis=-1, keepdims=True)
    meta_ref[...] = (gates + jnp.where(lane == META_BKT, bucket.astype(F32), 0.0)
                     + jnp.where(lane == META_RANK, rank, 0.0))
    run_ref[...] += prefix[tm - 1:tm, :]
    cnt_ref[...] = run_ref[...]


def _moe_dispatch_kernel(pos_ref, x_ref, meta_ref, zeros_hbm, xs_hbm, buf, sem):
    del zeros_hbm
    tm, d = x_ref.shape
    buf[:, :d] = x_ref[...]
    buf[:, d:] = meta_ref[...]

    def issue(r, carry):
        pltpu.make_async_copy(buf.at[pl.ds(r, 1)], xs_hbm.at[pl.ds(pos_ref[0, 0, r], 1)], sem).start()
        return carry

    lax.fori_loop(0, tm, issue, 0, unroll=8)
    pltpu.make_async_copy(buf, xs_hbm.at[pl.ds(0, tm)], sem).wait()


def _moe_pair_kernel(ea_ref, eb_ref, used_ref, xs_ref, w1a_ref, w3a_ref, w2a_ref, w1b_ref, w3b_ref, w2b_ref, g_ref,
                     b_ref, ys_ref):
    del ea_ref, eb_ref
    d = ys_ref.shape[1]

    @pl.when(pl.program_id(0) < used_ref[0])
    def _():
        x = xs_ref[:, :d]
        gates = xs_ref[:, d:]
        lane = lax.broadcasted_iota(jnp.int32, gates.shape, 1)
        xb = x.astype(BF16)
        y = None
        for e, (w1_ref, w3_ref, w2_ref) in enumerate(((w1a_ref, w3a_ref, w2a_ref), (w1b_ref, w3b_ref, w2b_ref))):
            h1 = jnp.dot(xb, w1_ref[0], preferred_element_type=F32)
            h3 = jnp.dot(xb, w3_ref[0], preferred_element_type=F32)
            hid = (h1 * jax.nn.sigmoid(h1) * h3).astype(BF16)
            ge = jnp.sum(jnp.where(lane == e, gates, 0.0), axis=-1, keepdims=True)
            term = ge * jnp.dot(hid, w2_ref[0], preferred_element_type=F32)
            y = term if y is None else y + term
        ys_ref[...] = _layer_norm_rows(DN_ALPHA * x + y, g_ref[...], b_ref[...])

    @pl.when(pl.program_id(0) >= used_ref[0])
    def _():
        ys_ref[...] = jnp.zeros_like(ys_ref)


def _moe_collect_kernel(pos_ref, ys_hbm, out_ref, sem):
    tm = out_ref.shape[0]

    def issue(r, carry):
        pltpu.make_async_copy(ys_hbm.at[pl.ds(pos_ref[0, 0, r], 1)], out_ref.at[pl.ds(r, 1)], sem).start()
        return carry

    lax.fori_loop(0, tm, issue, 0, unroll=8)
    pltpu.make_async_copy(ys_hbm.at[pl.ds(0, tm)], out_ref, sem).wait()


def _hier_moe_ln(x, wg, bg, we, be, w1, w3, w2, ln_g, ln_b):
    N, D = x.shape
    G, E, FF, NB = MOE_GROUPS, MOE_EPG, MOE_FF, MOE_BUCKETS
    wr = jnp.concatenate([wg, jnp.moveaxis(we, 0, 1).reshape(D, MOE_EXPERTS),
                          jnp.zeros((D, LANE - G - MOE_EXPERTS), F32)], axis=1)
    br = jnp.concatenate([bg, be.reshape(-1), jnp.zeros((LANE - G - MOE_EXPERTS,), F32)]).reshape(1, LANE)
    wrh, wrl = _split_bf16(wr)
    tm = MOE_TM
    tri = jnp.asarray(np.tril(np.ones((tm, tm), np.float32)), BF16)
    meta, cnt = pl.pallas_call(
        _router_kernel,
        out_shape=[jax.ShapeDtypeStruct((N, LANE), F32), jax.ShapeDtypeStruct((1, LANE), F32)],
        grid=(N // tm,),
        in_specs=[pl.BlockSpec((tm, D), lambda i: (i, 0)),
                  pl.BlockSpec((D, LANE), lambda i: (0, 0)),
                  pl.BlockSpec((D, LANE), lambda i: (0, 0)),
                  pl.BlockSpec((1, LANE), lambda i: (0, 0)),
                  pl.BlockSpec((tm, tm), lambda i: (0, 0))],
        out_specs=[pl.BlockSpec((tm, LANE), lambda i: (i, 0)), pl.BlockSpec((1, LANE), lambda i: (0, 0))],
        scratch_shapes=[pltpu.VMEM((1, LANE), F32)],
        compiler_params=_cparams("arbitrary"),
        name="moe_router",
    )(x, wrh, wrl, br, tri)

    counts = cnt[0, :NB].astype(jnp.int32)
    padded = (counts + tm - 1) // tm * tm
    ends = jnp.cumsum(padded)
    starts = ends - padded
    pos = (starts[meta[:, META_BKT].astype(jnp.int32)] + meta[:, META_RANK].astype(jnp.int32)).reshape(N // tm, 1, tm)
    n_pad = N + NB * tm
    n_tiles = n_pad // tm
    tile_bkt = jnp.minimum(jnp.searchsorted(ends, jnp.arange(n_tiles, dtype=jnp.int32) * tm, side="right"),
                           NB - 1).astype(jnp.int32)
    tile_ea = (tile_bkt // MOE_PAIRS) * E + jnp.asarray(PAIR_LO, jnp.int32)[tile_bkt % MOE_PAIRS]
    tile_eb = (tile_bkt // MOE_PAIRS) * E + jnp.asarray(PAIR_HI, jnp.int32)[tile_bkt % MOE_PAIRS]

    pos_spec = pl.BlockSpec((1, 1, tm), lambda i: (i, 0, 0), memory_space=pltpu.SMEM)
    xs = pl.pallas_call(
        _moe_dispatch_kernel,
        out_shape=jax.ShapeDtypeStruct((n_pad, D + LANE), F32),
        grid=(N // tm,),
        in_specs=[pos_spec,
                  pl.BlockSpec((tm, D), lambda i: (i, 0)),
                  pl.BlockSpec((tm, LANE), lambda i: (i, 0)),
                  pl.BlockSpec(memory_space=pl.ANY)],
        out_specs=pl.BlockSpec(memory_space=pl.ANY),
        scratch_shapes=[pltpu.VMEM((tm, D + LANE), F32), pltpu.SemaphoreType.DMA],
        input_output_aliases={3: 0},
        compiler_params=_cparams("arbitrary"),
        name="moe_dispatch",
    )(pos, x, meta, jnp.zeros((n_pad, D + LANE), F32))

    wa = lambda a, b: pl.BlockSpec((1, a, b), lambda t, ea, eb, nu: (ea[t], 0, 0))
    wb = lambda a, b: pl.BlockSpec((1, a, b), lambda t, ea, eb, nu: (eb[t], 0, 0))
    w1b, w3b, w2b = w1.astype(BF16), w3.astype(BF16), w2.astype(BF16)
    ys = pl.pallas_call(
        _moe_pair_kernel,
        out_shape=jax.ShapeDtypeStruct((n_pad, D), F32),
        grid_spec=pltpu.PrefetchScalarGridSpec(
            num_scalar_prefetch=3,
            grid=(n_tiles,),
            in_specs=[pl.BlockSpec((tm, D + LANE), lambda t, ea, eb, nu: (t, 0)),
                      wa(D, FF), wa(D, FF), wa(FF, D), wb(D, FF), wb(D, FF), wb(FF, D),
                      pl.BlockSpec((1, D), lambda t, ea, eb, nu: (0, 0)),
                      pl.BlockSpec((1, D), lambda t, ea, eb, nu: (0, 0))],
            out_specs=pl.BlockSpec((tm, D), lambda t, ea, eb, nu: (t, 0))),
        compiler_params=_cparams("arbitrary"),
        name="moe_experts_ln",
    )(tile_ea, tile_eb, (ends[-1:] // tm).astype(jnp.int32), xs, w1b, w3b, w2b, w1b, w3b, w2b,
      ln_g.reshape(1, D), ln_b.reshape(1, D))

    return pl.pallas_call(
        _moe_collect_kernel,
        out_shape=jax.ShapeDtypeStruct((N, D), F32),
        grid=(N // tm,),
        in_specs=[pos_spec, pl.BlockSpec(memory_space=pl.ANY)],
        out_specs=pl.BlockSpec((tm, D), lambda i: (i, 0)),
        scratch_shapes=[pltpu.SemaphoreType.DMA],
        compiler_params=_cparams("arbitrary"),
        name="moe_collect",
    )(pos, ys)


def kernel(x, mla_w_in, mla_q_norm, mla_kv_norm, mla_w_qb, mla_w_kvb, mla_w_o, nsa_w_in, nsa_w_phi_k, nsa_w_phi_v, nsa_cmp_pos, nsa_w_o, diff_w_in, diff_lam_q1, diff_lam_k1, diff_lam_q2, diff_lam_k2, diff_subln, diff_w_o, dil_w_in, dil_w_o, ln1_g, ln1_b, ln2_g, ln2_b, moe_wg, moe_bg, moe_we, moe_be, moe_w1, moe_w3, moe_w2):
    B, S, D = x.shape
    h = x.reshape(B * S, D)
    for i in range(DEPTH):
        m, j = i % 4, i // 4
        if m == 0:
            h = _mla_mixer(h, mla_w_in[j], mla_q_norm[j], mla_kv_norm[j], mla_w_qb[j], mla_w_kvb[j], mla_w_o[j],
                           ln1_g[i], ln1_b[i], B, S)
        elif m == 1:
            h = _nsa_mixer(h, nsa_w_in[j], nsa_w_phi_k[j], nsa_w_phi_v[j], nsa_cmp_pos[j], nsa_w_o[j],
                           ln1_g[i], ln1_b[i], B, S)
        elif m == 2:
            h = _diff_mixer(h, diff_w_in[j], diff_lam_q1[j], diff_lam_k1[j], diff_lam_q2[j], diff_lam_k2[j],
                            diff_subln[j], diff_w_o[j], i, ln1_g[i], ln1_b[i], B, S)
        else:
            h = _dil_mixer(h, dil_w_in[j], dil_w_o[j], ln1_g[i], ln1_b[i], B, S)
        h = _hier_moe_ln(h, moe_wg[i], moe_bg[i], moe_we[i], moe_be[i], moe_w1[i], moe_w3[i], moe_w2[i],
                         ln2_g[i], ln2_b[i])
    return h.reshape(B, S, D)
```

```python
import functools
import math

import numpy as np
import jax
import jax.numpy as jnp
from jax import lax
from jax.experimental import pallas as pl
from jax.experimental.pallas import tpu as pltpu

F32 = jnp.float32
BF16 = jnp.bfloat16

DEPTH = 4
DN_ALPHA = (2.0 * DEPTH) ** 0.25
LN_EPS = 1e-5
NEG_INF = -1e30
LOG2E = math.log2(math.e)
LANE = 128
VMEM_LIMIT = 56 * 1024 * 1024
ATTN_TQ, ATTN_TK = 256, 512
MOE_TM = 512

MLA_HEADS, MLA_Q_RANK, MLA_KV_RANK, MLA_NOPE, MLA_ROPE, MLA_V = 16, 384, 256, 64, 32, 64
ROPE_THETA = 10000.0
NSA_HEADS, NSA_GROUPS, NSA_HD = 16, 4, 64
NSA_CMP_LEN, NSA_CMP_STRIDE, NSA_SLC_LEN, NSA_TOP_N, NSA_WINDOW, NSA_FORCE = 32, 16, 64, 8, 512, 1e4
DIFF_HEADS, DIFF_HD = 8, 64
DIL_PATTERNS = ((128, 1), (512, 4), (2048, 16))
DIL_HEADS, DIL_HD = 8, 64
MOE_GROUPS, MOE_EPG, MOE_EXPERTS, MOE_FF = 4, 4, 16, 512


def _cparams(*sem):
    return pltpu.CompilerParams(dimension_semantics=sem, vmem_limit_bytes=VMEM_LIMIT)


def _alibi(n):
    return np.asarray(2.0 ** (-8.0 * np.arange(1, n + 1) / n), np.float32)


def _linear_kernel(x_ref, w_ref, *out_refs, splits, chunk):
    xb = x_ref[...].astype(BF16)
    col = 0
    for o_ref, n in zip(out_refs, splits):
        for c0 in range(0, n, chunk):
            cw = min(chunk, n - c0)
            o_ref[:, c0:c0 + cw] = jnp.dot(
                xb, w_ref[:, col + c0:col + c0 + cw], preferred_element_type=F32).astype(o_ref.dtype)
        col += n


def _linear(x, w, splits, dtypes, tm=512, name="linear"):
    M, K = x.shape
    ntot = sum(splits)
    assert w.shape == (K, ntot) and M % tm == 0 and all(n % LANE == 0 for n in splits)
    outs = pl.pallas_call(
        functools.partial(_linear_kernel, splits=tuple(splits), chunk=512),
        out_shape=[jax.ShapeDtypeStruct((M, n), d) for n, d in zip(splits, dtypes)],
        grid=(M // tm,),
        in_specs=[pl.BlockSpec((tm, K), lambda i: (i, 0)),
                  pl.BlockSpec((K, ntot), lambda i: (0, 0))],
        out_specs=[pl.BlockSpec((tm, n), lambda i: (i, 0)) for n in splits],
        compiler_params=_cparams("parallel"),
        name=name,
    )(x, w)
    return outs


def _layer_norm_rows(z, g, b):
    mu = jnp.mean(z, axis=-1, keepdims=True)
    zc = z - mu
    var = jnp.mean(zc * zc, axis=-1, keepdims=True)
    return zc * lax.rsqrt(var + LN_EPS) * g + b


def _outproj_ln_kernel(o_ref, w_ref, x_ref, g_ref, b_ref, out_ref):
    y = jnp.dot(o_ref[...].astype(BF16), w_ref[...], preferred_element_type=F32)
    out_ref[...] = _layer_norm_rows(DN_ALPHA * x_ref[...] + y, g_ref[...], b_ref[...])


def _outproj_ln(o, w_o, x, g, b, tm=512, name="outproj_ln"):
    M, K = o.shape
    D = x.shape[1]
    return pl.pallas_call(
        _outproj_ln_kernel,
        out_shape=jax.ShapeDtypeStruct((M, D), F32),
        grid=(M // tm,),
        in_specs=[pl.BlockSpec((tm, K), lambda i: (i, 0)),
                  pl.BlockSpec((K, D), lambda i: (0, 0)),
                  pl.BlockSpec((tm, D), lambda i: (i, 0)),
                  pl.BlockSpec((1, D), lambda i: (0, 0)),
                  pl.BlockSpec((1, D), lambda i: (0, 0))],
        out_specs=pl.BlockSpec((tm, D), lambda i: (i, 0)),
        compiler_params=_cparams("parallel"),
        name=name,
    )(o, w_o.astype(BF16), x, g.reshape(1, D), b.reshape(1, D))


def _scores(q, k, bias):
    s = lax.dot_general(q, k, (((1,), (1,)), ((), ())), preferred_element_type=F32)
    return s if bias is None else s + bias


def _causal_item_list(n_q, tq, tk):
    items = []
    for i in range(n_q):
        c_d = (i * tq) // tk
        items += [(i, c, c == 0, c == c_d) for c in range(c_d + 1)]
    return tuple(items)


def _flat_scratch(n_streams, M, t):
    return [pltpu.VMEM((n_streams, 2, M, t), F32), pltpu.VMEM((n_streams, 2, M, t), BF16),
            pltpu.VMEM((n_streams, M, LANE), F32), pltpu.VMEM((n_streams, 2, 3, M, LANE), F32)]


def _flash_static(items, streams, finalize, scratch, tq, tk):
    s_ref, p_ref, acc_ref, st_ref = scratch
    ns = len(streams)
    reps = tk // LANE
    n_rep = s_ref.shape[2] // tq

    def scores(n, item):
        q_at, k_at, _, bias_at = streams[n]
        return _scores(q_at(item[0]), k_at(item[1]), None if bias_at is None else bias_at(item[1]))

    def lane_partial_sum(p):
        return functools.reduce(jnp.add, [p[:, r * LANE:(r + 1) * LANE] for r in range(reps)])

    def value_update(n, prev, slot_prev):
        pv = jnp.dot(p_ref[n, slot_prev], streams[n][2](prev[1]), preferred_element_type=F32)
        acc_ref[n] = pv if prev[2] else st_ref[n, slot_prev, 2] * acc_ref[n] + pv

    def finish(prev, slot_prev):
        finalize(prev[0], [(jnp.sum(st_ref[n, slot_prev, 1], axis=-1, keepdims=True), acc_ref[n]) for n in range(ns)])

    for n in range(ns):
        s_ref[n, 0] = scores(n, items[0])
    for j, item in enumerate(items):
        slot = j % 2
        i, c, first, last = item
        if j + 1 < len(items):
            for n in range(ns):
                s_ref[n, 1 - slot] = scores(n, items[j + 1])
        if j > 0:
            for n in range(ns):
                value_update(n, items[j - 1], 1 - slot)
        vis = None
        if (c + 1) * tk > i * tq + 1:
            vis = _rel_pos(tq, tk) + (c * tk - i * tq) <= 0
            if n_rep > 1:
                vis = jnp.concatenate([vis] * n_rep, axis=0)
        for n in range(ns):
            s = s_ref[n, slot]
            if vis is not None:
                s = jnp.where(vis, s, NEG_INF)
            m_cur = jnp.max(s, axis=-1, keepdims=True)
            if first:
                m_new = jnp.broadcast_to(m_cur, (s.shape[0], LANE))
            else:
                m_old = st_ref[n, 1 - slot, 0]
                m_new = jnp.maximum(m_old, m_cur)
                alpha = jnp.exp2(m_old - m_new)
                st_ref[n, slot, 2] = alpha
            p = jnp.exp2(s - jnp.concatenate([m_new] * reps, axis=1))
            st_ref[n, slot, 0] = m_new
            st_ref[n, slot, 1] = lane_partial_sum(p) if first else alpha * st_ref[n, 1 - slot, 1] + lane_partial_sum(p)
            p_ref[n, slot] = p.astype(BF16)
        if j > 0 and items[j - 1][3]:
            finish(items[j - 1], 1 - slot)
    last_slot = (len(items) - 1) % 2
    for n in range(ns):
        value_update(n, items[-1], last_slot)
    finish(items[-1], last_slot)


def _window_attention(q, ks, vs, viss, biases=None):
    ss = []
    for j, (k, vis) in enumerate(zip(ks, viss)):
        s = _scores(q, k, None if biases is None else biases[j])
        ss.append(s if vis is None else jnp.where(vis, s, NEG_INF))
    m = jnp.max(functools.reduce(jnp.maximum, ss), axis=-1, keepdims=True)
    ps = [jnp.exp2(s - m) for s in ss]
    l = jnp.sum(functools.reduce(jnp.add, ps), axis=-1, keepdims=True)
    acc = functools.reduce(jnp.add, [jnp.dot(p.astype(BF16), v, preferred_element_type=F32) for p, v in zip(ps, vs)])
    return m, l, acc


def _rel_pos(rows, cols):
    return (lax.broadcasted_iota(jnp.int32, (rows, cols), 1)
            - lax.broadcasted_iota(jnp.int32, (rows, cols), 0))


def _rms_rows(c, g, eps):
    return c * lax.rsqrt(jnp.mean(c * c, axis=-1, keepdims=True) + eps) * g


def _mla_proj_kernel(x_ref, win_ref, qn_ref, kvn_ref, wq_ref, wqs_ref, wk_ref, wv_ref, cos_ref, sin_ref,
                     q_out, k_out, v_out, *, scale):
    xb = x_ref[...].astype(BF16)
    c = jnp.dot(xb, win_ref[...], preferred_element_type=F32)
    r0, r1 = MLA_Q_RANK, MLA_Q_RANK + MLA_KV_RANK
    cq = _rms_rows(c[:, :r0], qn_ref[...], 1e-6).astype(BF16)
    ckv = _rms_rows(c[:, r0:r1], kvn_ref[...], 1e-6).astype(BF16)
    cos, sin = cos_ref[...], sin_ref[...]
    kr = c[:, r1:r1 + LANE] * cos + c[:, r1 + LANE:r1 + 2 * LANE] * sin
    v_out[...] = jnp.dot(ckv, wv_ref[...], preferred_element_type=F32).astype(v_out.dtype)
    cos2, sin2, kr2 = (jnp.concatenate([t, t], axis=1) for t in (cos, sin, kr))
    for hp in range(MLA_HEADS // 2):
        sl = slice(2 * hp * LANE, 2 * (hp + 1) * LANE)
        qh = jnp.dot(cq, wq_ref[:, sl], preferred_element_type=F32)
        qhs = jnp.dot(cq, wqs_ref[:, sl], preferred_element_type=F32)
        q_out[:, sl] = ((qh * cos2 + qhs * sin2) * scale).astype(q_out.dtype)
        kh = jnp.dot(ckv, wk_ref[:, sl], preferred_element_type=F32)
        k_out[:, sl] = (kh + kr2).astype(k_out.dtype)


def _mla_attn_kernel(q_ref, k_ref, v_ref, o_ref, *scratch, tq, tk, items):
    def rows(i, t):
        return pl.ds(i * t, t)

    def v_at(c):
        return v_ref[0, rows(c, tk), :]

    streams = []
    for hh in range(2):
        sl = slice(hh * LANE, (hh + 1) * LANE)
        streams.append((lambda i, sl=sl: q_ref[0, rows(i, tq), sl], lambda c, sl=sl: k_ref[0, rows(c, tk), sl],
                        v_at, None))
    lane = lax.broadcasted_iota(jnp.int32, (tq, LANE), 1)

    def finalize(i, res):
        (la, acca), (lb, accb) = res
        o_ref[0, rows(i, tq), :] = jnp.where(lane < MLA_V, acca / la, accb / lb).astype(o_ref.dtype)

    _flash_static(items, streams, finalize, scratch, tq, tk)


def _mla_mixer(x, w_in, q_norm, kv_norm, w_qb, w_kvb, w_o, ln_g, ln_b, B, S):
    N, D = x.shape
    H, dq = MLA_HEADS, MLA_NOPE + MLA_ROPE
    half = MLA_ROPE // 2
    r0, r1 = MLA_Q_RANK, MLA_Q_RANK + MLA_KV_RANK
    z = lambda *s: jnp.zeros(s, F32)
    swap = lambda a: jnp.concatenate([a[..., half:], a[..., :half]], axis=-1)
    kr_w = w_in[:, r1:]
    win = jnp.concatenate([w_in[:, :r1],
                           z(D, MLA_NOPE), kr_w, z(D, LANE - dq),
                           z(D, MLA_NOPE), swap(kr_w), z(D, LANE - dq)], axis=1).astype(BF16)
    wq3 = w_qb.reshape(r0, H, dq)
    wq = jnp.concatenate([wq3, z(r0, H, LANE - dq)], axis=-1).reshape(r0, H * LANE).astype(BF16)
    wqs = jnp.concatenate([z(r0, H, MLA_NOPE), swap(wq3[..., MLA_NOPE:]), z(r0, H, LANE - dq)],
                          axis=-1).reshape(r0, H * LANE).astype(BF16)
    wkv3 = w_kvb.reshape(MLA_KV_RANK, H, MLA_NOPE + MLA_V)
    wk = jnp.concatenate([wkv3[..., :MLA_NOPE], z(MLA_KV_RANK, H, LANE - MLA_NOPE)],
                         axis=-1).reshape(MLA_KV_RANK, H * LANE).astype(BF16)
    wv = wkv3[..., MLA_NOPE:].reshape(MLA_KV_RANK, H * MLA_V).astype(BF16)
    freq = ROPE_THETA ** (-jnp.arange(half, dtype=F32) / half)
    ang = jnp.arange(S, dtype=F32)[:, None] * freq
    cos, sin = jnp.cos(ang), jnp.sin(ang)
    ones, zer = jnp.ones((S, MLA_NOPE), F32), jnp.zeros((S, LANE - dq), F32)
    cos_t = jnp.concatenate([ones, cos, cos, zer], axis=1)
    sin_t = jnp.concatenate([0 * ones, -sin, sin, zer], axis=1)

    tm = 256
    nwin = win.shape[1]
    q, k, v = pl.pallas_call(
        functools.partial(_mla_proj_kernel, scale=dq ** -0.5 * LOG2E),
        out_shape=[jax.ShapeDtypeStruct((N, H * LANE), BF16),
                   jax.ShapeDtypeStruct((N, H * LANE), BF16),
                   jax.ShapeDtypeStruct((N, H * MLA_V), BF16)],
        grid=(N // tm,),
        in_specs=[pl.BlockSpec((tm, D), lambda i: (i, 0)),
                  pl.BlockSpec((D, nwin), lambda i: (0, 0)),
                  pl.BlockSpec((1, r0), lambda i: (0, 0)),
                  pl.BlockSpec((1, MLA_KV_RANK), lambda i: (0, 0)),
                  pl.BlockSpec((r0, H * LANE), lambda i: (0, 0)),
                  pl.BlockSpec((r0, H * LANE), lambda i: (0, 0)),
                  pl.BlockSpec((MLA_KV_RANK, H * LANE), lambda i: (0, 0)),
                  pl.BlockSpec((MLA_KV_RANK, H * MLA_V), lambda i: (0, 0)),
                  pl.BlockSpec((tm, LANE), lambda i: (i % (S // tm), 0)),
                  pl.BlockSpec((tm, LANE), lambda i: (i % (S // tm), 0))],
        out_specs=[pl.BlockSpec((tm, H * LANE), lambda i: (i, 0)),
                   pl.BlockSpec((tm, H * LANE), lambda i: (i, 0)),
                   pl.BlockSpec((tm, H * MLA_V), lambda i: (i, 0))],
        compiler_params=_cparams("parallel"),
        name="mla_proj",
    )(x, win, q_norm.reshape(1, r0), kv_norm.reshape(1, MLA_KV_RANK), wq, wqs, wk, wv, cos_t, sin_t)

    tq, tk = ATTN_TQ, ATTN_TK
    o = pl.pallas_call(
        functools.partial(_mla_attn_kernel, tq=tq, tk=tk, items=_causal_item_list(S // tq, tq, tk)),
        out_shape=jax.ShapeDtypeStruct((B, S, H * MLA_V), BF16),
        grid=(B, H // 2),
        in_specs=[pl.BlockSpec((1, S, 2 * LANE), lambda b, h: (b, 0, h)),
                  pl.BlockSpec((1, S, 2 * LANE), lambda b, h: (b, 0, h)),
                  pl.BlockSpec((1, S, LANE), lambda b, h: (b, 0, h))],
        out_specs=pl.BlockSpec((1, S, LANE), lambda b, h: (b, 0, h)),
        scratch_shapes=_flat_scratch(2, tq, tk),
        compiler_params=_cparams("parallel", "parallel"),
        name="mla_attn",
    )(q.reshape(B, S, H * LANE), k.reshape(B, S, H * LANE), v.reshape(B, S, H * MLA_V))
    return _outproj_ln(o.reshape(N, H * MLA_V), w_o, x, ln_g, ln_b, name="mla_out_ln")


def _diff_attn_kernel(slope_ref, lam_ref, sub_ref, q_ref, k_ref, v_ref, o_ref, *scratch, tq, tk, items, lam_init):
    slope = slope_ref[pl.program_id(1)] * LOG2E
    lv = lam_ref[...]
    lam = (jnp.exp(jnp.sum(lv[0:1] * lv[1:2], axis=-1, keepdims=True))
           - jnp.exp(jnp.sum(lv[2:3] * lv[3:4], axis=-1, keepdims=True)) + lam_init)
    lane = lax.broadcasted_iota(jnp.int32, (tq, LANE), 1)
    colf = lax.broadcasted_iota(jnp.int32, (1, tk), 1).astype(F32)

    def rows(i, t):
        return pl.ds(i * t, t)

    def q_at(i, first_map):
        qf = q_ref[0, rows(i, tq), :].astype(F32) * (DIFF_HD ** -0.5 * LOG2E)
        return jnp.where((lane < DIFF_HD) == first_map, qf, 0.0).astype(BF16)

    def k_at(c):
        return k_ref[0, rows(c, tk), :]

    def v_at(c):
        return v_ref[0, rows(c, tk), :]

    def bias_at(c):
        return slope * (colf + float(c * tk))

    def finalize(i, res):
        (l0, acc0), (l1, acc1) = res
        a = _rms_rows(acc0 / l0 - lam * (acc1 / l1), sub_ref[...], 1e-5) * (1.0 - lam_init)
        o_ref[0, rows(i, tq), :] = a.astype(o_ref.dtype)

    streams = [(functools.partial(q_at, first_map=fm), k_at, v_at, bias_at) for fm in (True, False)]
    _flash_static(items, streams, finalize, scratch, tq, tk)


def _diff_mixer(x, w_in, lam_q1, lam_k1, lam_q2, lam_k2, subln, w_o, layer_idx, ln_g, ln_b, B, S):
    N, D = x.shape
    H, d = DIFF_HEADS, DIFF_HD
    nq = H * 2 * d
    (qkv,) = _linear(x, w_in.astype(BF16), [3 * nq], [BF16], name="diff_proj")
    qkv = qkv.reshape(B, S, 3 * nq)
    lam_init = 0.8 - 0.6 * math.exp(-0.3 * layer_idx)
    lamv = jnp.zeros((8, LANE), F32).at[:4, :d].set(jnp.stack([lam_q1, lam_k1, lam_q2, lam_k2]).astype(F32))
    tq, tk = ATTN_TQ, ATTN_TK
    o = pl.pallas_call(
        functools.partial(_diff_attn_kernel, tq=tq, tk=tk, items=_causal_item_list(S // tq, tq, tk), lam_init=lam_init),
        out_shape=jax.ShapeDtypeStruct((B, S, nq), BF16),
        grid=(B, H),
        in_specs=[pl.BlockSpec(memory_space=pltpu.SMEM),
                  pl.BlockSpec((8, LANE), lambda b, h: (0, 0)),
                  pl.BlockSpec((1, 2 * d), lambda b, h: (0, 0)),
                  pl.BlockSpec((1, S, LANE), lambda b, h: (b, 0, h)),
                  pl.BlockSpec((1, S, LANE), lambda b, h: (b, 0, H + h)),
                  pl.BlockSpec((1, S, LANE), lambda b, h: (b, 0, 2 * H + h))],
        out_specs=pl.BlockSpec((1, S, LANE), lambda b, h: (b, 0, h)),
        scratch_shapes=_flat_scratch(2, tq, tk),
        compiler_params=_cparams("parallel", "parallel"),
        name="diff_attn",
    )(jnp.asarray(_alibi(H)), lamv, subln.reshape(1, 2 * d).astype(F32), qkv, qkv, qkv)
    return _outproj_ln(o.reshape(N, nq), w_o, x, ln_g, ln_b, name="diff_out_ln")


SEL_LANE0 = NSA_HD
SEL_OFF = 1e30
ALIBI_LANE0 = 96
POS_SPLIT = 64


def _nsa_compress_kernel(ak_ref, av_ref, plo_ref, phi_ref, wklo_ref, wkhi_ref, wvlo_ref, wvhi_ref, kc_out, vc_out):
    def one(a_ref, wlo_ref, whi_ref, out):
        a = a_ref[0]
        lo = jnp.dot((a + plo_ref[...]).astype(BF16), wlo_ref[...], preferred_element_type=F32)
        hi = jnp.dot((a + phi_ref[...]).astype(BF16), whi_ref[...], preferred_element_type=F32)
        out[0] = (lo + pltpu.roll(hi, hi.shape[0] - 1, 0)).astype(out.dtype)

    one(ak_ref, wklo_ref, wkhi_ref, kc_out)
    one(av_ref, wvlo_ref, wvhi_ref, vc_out)


def _pack_heads(o, tq):
    lane = lax.broadcasted_iota(jnp.int32, (tq, LANE), 1)
    p01 = jnp.where(lane < NSA_HD, o[0:tq], o[tq:2 * tq])
    p23 = jnp.where(lane < NSA_HD, o[2 * tq:3 * tq], o[3 * tq:4 * tq])
    return jnp.concatenate([p01, p23], axis=1)


def _nsa_attn_kernel(q_ref, qc_ref, gl_ref, kc_ref, vc_ref, ks_ref, vs_ref, kw_ref, vw_ref, ktab_ref, ctab_ref,
                     c2s_ref, rep_ref, o_ref, selb_ref, ocw_ref, gs_ref, *scratch, tq, tk, tqs, tks, n_slc, items):
    hpg = NSA_HEADS // NSA_GROUPS

    def head_q(rows, j):
        return q_ref[0, rows, j * LANE:(j + 1) * LANE] + jnp.broadcast_to(qc_ref[0, j:j + 1, :],
                                                                         (rows.size, LANE)).astype(BF16)

    def tile(i, carry):
        _nsa_tile(i, head_q, gl_ref, kc_ref, vc_ref, kw_ref, vw_ref, ktab_ref, ctab_ref, c2s_ref, rep_ref,
                  selb_ref, ocw_ref, gs_ref, tq=tq, tk=tk, n_slc=n_slc)
        return carry

    lax.fori_loop(0, q_ref.shape[1] // tq, tile, 0, unroll=2)

    def rows_s(i, t):
        return pl.ds(i * t, t)

    def q_at(i, pair):
        r = rows_s(i, tqs)
        return jnp.concatenate([head_q(r, j) + selb_ref[r, :] for j in (2 * pair, 2 * pair + 1)], axis=0)

    def ks_at(c):
        return ks_ref[0, rows_s(c, tks), :] + ktab_ref[rows_s(c, tks), :]

    def vs_at(c):
        return vs_ref[0, rows_s(c, tks), :]

    def finalize(i, res):
        r = rows_s(i, tqs)
        o_s = jnp.concatenate([acc / l for l, acc in res], axis=0)
        o_ref[0, r, :] = (ocw_ref[r, :] + gs_ref[r, :] * _pack_heads(o_s, tqs)).astype(o_ref.dtype)

    streams = [(functools.partial(q_at, pair=pr), ks_at, vs_at, None) for pr in range(hpg // 2)]
    _flash_static(items, streams, finalize, scratch, tqs, tks)


def _nsa_tile(i, head_q, gl_ref, kc_ref, vc_ref, kw_ref, vw_ref, ktab_ref, ctab_ref, c2s_ref, rep_ref,
              selb_ref, ocw_ref, gs_ref, *, tq, tk, n_slc):
    t0 = i * tq
    rows = pl.ds(pl.multiple_of(t0, tq), tq)
    hpg = NSA_HEADS // NSA_GROUPS
    M = hpg * tq
    q = jnp.concatenate([head_q(rows, j) for j in range(hpg)], axis=0)
    trow1 = t0 + lax.broadcasted_iota(jnp.int32, (tq, 1), 0)
    trow = jnp.concatenate([trow1] * hpg, axis=0)
    lane_m = lax.broadcasted_iota(jnp.int32, (M, LANE), 1)

    sc = lax.dot_general(q, kc_ref[0] + ctab_ref[...], (((1,), (1,)), ((), ())), preferred_element_type=F32)
    sc = jnp.where(trow >= lane_m * NSA_CMP_STRIDE + (NSA_CMP_LEN - 1), sc, NEG_INF)
    e = jnp.exp2(sc - jnp.max(sc, axis=-1, keepdims=True))
    p_c = jnp.where(trow >= NSA_CMP_LEN - 1, e / jnp.sum(e, axis=-1, keepdims=True), 0.0)
    o_c = jnp.dot(p_c.astype(BF16), vc_ref[0], preferred_element_type=F32)
    psum = p_c[0:tq]
    for j in range(1, hpg):
        psum = psum + p_c[j * tq:(j + 1) * tq]
    ph, plw = _split_bf16(psum)
    imp = (jnp.dot(ph, c2s_ref[...], preferred_element_type=F32)
           + jnp.dot(plw, c2s_ref[...], preferred_element_type=F32))

    lane = lax.broadcasted_iota(jnp.int32, (tq, LANE), 1)
    sidx = lane - SEL_LANE0
    valid = (sidx >= 0) & (sidx < n_slc)
    forced = (sidx == 0) | (sidx == trow1 // NSA_SLC_LEN)
    future = sidx * NSA_SLC_LEN > trow1
    score = imp + jnp.where(forced, NSA_FORCE, 0.0) - jnp.where(future, 2.0 * NSA_FORCE, 0.0)
    score = jnp.where(valid, score, -jnp.inf)
    n_rows = ALIBI_LANE0 - SEL_LANE0
    st = score.T[SEL_LANE0:ALIBI_LANE0]
    blk = lax.broadcasted_iota(jnp.int32, (n_rows, tq), 0)
    beaten_by = jnp.zeros((n_rows, tq), jnp.int32)
    for other in range(n_slc):
        row = st[other:other + 1, :]
        beaten_by = beaten_by + jnp.where((row > st) | ((row == st) & (blk > other)), 1, 0)
    sel = beaten_by < min(NSA_TOP_N, n_slc)
    off = jnp.where(sel | (blk >= n_slc), 0.0, -SEL_OFF)
    selbias = jnp.concatenate([jnp.zeros((SEL_LANE0, tq), F32), off, jnp.zeros((LANE - ALIBI_LANE0, tq), F32)],
                              axis=0).T.astype(BF16)

    selb_ref[rows, :] = selbias

    c_d = t0 // tk
    d_diag = jnp.concatenate([_rel_pos(tq, tk)] * hpg, axis=0) + (c_d * tk - t0)

    def chunk(c):
        return pl.ds(pl.multiple_of(c * tk, tk), tk)

    cw = [jnp.maximum(c_d - 2, 0), jnp.maximum(c_d - 1, 0), c_d]
    vis_w = [(d_diag - 2 * tk > -NSA_WINDOW) & (c_d >= 2), jnp.broadcast_to(c_d >= 1, d_diag.shape), d_diag <= 0]
    _, l_w, acc_w = _window_attention(q, [kw_ref[0, chunk(c), :] + ktab_ref[chunk(c), :] for c in cw],
                                      [vw_ref[0, chunk(c), :] for c in cw], vis_w)
    o_w = acc_w / l_w

    gh, glw = _split_bf16(jax.nn.sigmoid(gl_ref[0, rows, :]))
    gr = jnp.dot(gh, rep_ref[0], preferred_element_type=F32) + jnp.dot(glw, rep_ref[0], preferred_element_type=F32)
    w = hpg * NSA_HD
    ocw_ref[rows, :] = gr[:, 0:w] * _pack_heads(o_c, tq) + gr[:, 2 * w:3 * w] * _pack_heads(o_w, tq)
    gs_ref[rows, :] = gr[:, w:2 * w]


def _nsa_mixer(x, w_in, w_phi_k, w_phi_v, cmp_pos, w_o, ln_g, ln_b, B, S):
    N, D = x.shape
    H, G, d = NSA_HEADS, NSA_GROUPS, NSA_HD
    hpg = H // G
    L, st = NSA_CMP_LEN, NSA_CMP_STRIDE
    n_slc = S // NSA_SLC_LEN
    assert S % 256 == 0 and n_slc <= ALIBI_LANE0 - SEL_LANE0 and L == 2 * st
    cuts = [H * d + i * G * d for i in range(7)]
    wq, wkc, wvc, wks, wvs, wkw, wvw, wgl = jnp.split(w_in, cuts, axis=1)
    z = lambda *s: jnp.zeros(s, F32)
    pad_heads = lambda w, n: jnp.concatenate([w.reshape(D, n, d), z(D, n, LANE - d)], axis=-1).reshape(D, n * LANE)
    dup_heads = lambda w, n: jnp.concatenate([w.reshape(D, n, d)] * 2, axis=-1).reshape(D, n * LANE)
    wgl_p = jnp.concatenate([wgl, z(D, LANE - wgl.shape[1])], axis=1)
    wcat = jnp.concatenate([pad_heads(wq * (d ** -0.5 * LOG2E), H), wkc, wvc, pad_heads(wks, G), dup_heads(wvs, G),
                            pad_heads(wkw, G), dup_heads(wvw, G), wgl_p], axis=1).astype(BF16)
    q, kc, vc, ks, vs, kw, vw, gl = _linear(
        x, wcat, [H * LANE, G * d, G * d, G * LANE, G * LANE, G * LANE, G * LANE, LANE],
        [BF16, F32, F32, BF16, BF16, BF16, BF16, F32], name="nsa_proj")

    nrow = S // st
    eye = jnp.eye(G, dtype=F32)

    def phi_w(w_phi, half, dup):
        w = w_phi.reshape(L, d, d)[half * st:(half + 1) * st]
        wd = jnp.concatenate([w, w if dup else jnp.zeros_like(w)], axis=-1)
        return jnp.einsum('ldc,gh->lgdhc', wd, eye).reshape(st * G * d, G * LANE).astype(BF16)

    pos = lambda half: jnp.broadcast_to(cmp_pos[half * st:(half + 1) * st, None, :], (st, G, d)).reshape(1, st * G * d)
    wide = st * G * d
    cspec = pl.BlockSpec((1, nrow, wide), lambda b: (b, 0, 0))
    wspec = pl.BlockSpec((wide, G * LANE), lambda b: (0, 0))
    pspec = pl.BlockSpec((1, wide), lambda b: (0, 0))
    ospec = pl.BlockSpec((1, nrow, G * LANE), lambda b: (b, 0, 0))
    kcmp, vcmp = pl.pallas_call(
        _nsa_compress_kernel,
        out_shape=[jax.ShapeDtypeStruct((B, nrow, G * LANE), BF16)] * 2,
        grid=(B,),
        in_specs=[cspec, cspec, pspec, pspec, wspec, wspec, wspec, wspec],
        out_specs=[ospec, ospec],
        compiler_params=_cparams("parallel"),
        name="nsa_compress",
    )(kc.reshape(B, nrow, wide), vc.reshape(B, nrow, wide), pos(0), pos(1),
      phi_w(w_phi_k, 0, False), phi_w(w_phi_k, 1, False), phi_w(w_phi_v, 0, True), phi_w(w_phi_v, 1, True))

    def pos_lanes(tab, pos):
        tab[:, ALIBI_LANE0:ALIBI_LANE0 + 3] = (pos // POS_SPLIT)[:, None]
        tab[:, ALIBI_LANE0 + 3:ALIBI_LANE0 + 6] = (pos % POS_SPLIT)[:, None]
        return tab

    ktab = np.zeros((S, LANE), np.float32)
    ktab[np.arange(S), SEL_LANE0 + np.arange(S) // NSA_SLC_LEN] = 1.0
    ktab = pos_lanes(ktab, np.arange(S))
    ctab = pos_lanes(np.zeros((nrow, LANE), np.float32), np.arange(nrow) * st + (L - 1))
    s2 = jnp.asarray(_alibi(H) * LOG2E, F32)
    parts = []
    for _ in range(3):
        part = s2.astype(BF16).astype(F32)
        parts.append(part)
        s2 = s2 - part
    qcoef = jnp.zeros((H, LANE), F32).at[:, ALIBI_LANE0:ALIBI_LANE0 + 6].set(
        jnp.stack([POS_SPLIT * p for p in parts] + parts, axis=1)).reshape(G, hpg, LANE)
    cmp_start = np.arange(nrow) * st
    slc_start = np.arange(n_slc) * NSA_SLC_LEN
    ov = (cmp_start[:, None] < slc_start[None, :] + NSA_SLC_LEN) & (cmp_start[:, None] + L > slc_start[None, :])
    ov[(S - L) // st + 1:] = False
    c2s = np.zeros((nrow, LANE), np.float32)
    c2s[:, SEL_LANE0:SEL_LANE0 + n_slc] = ov
    rep = np.zeros((G, LANE, 3 * hpg * d), np.float32)
    for g in range(G):
        for j in range(hpg):
            for br in range(3):
                rep[g, (g * hpg + j) * 3 + br, br * hpg * d + j * d:br * hpg * d + (j + 1) * d] = 1.0

    tq, tk = 128, 256
    tqs, tks = ATTN_TQ, ATTN_TK
    assert NSA_WINDOW == 2 * tk and tk % tq == 0
    kvspec = pl.BlockSpec((1, S, LANE), lambda b, g: (b, 0, g))
    cmpspec = pl.BlockSpec((1, nrow, LANE), lambda b, g: (b, 0, g))
    o = pl.pallas_call(
        functools.partial(_nsa_attn_kernel, tq=tq, tk=tk, tqs=tqs, tks=tks, n_slc=n_slc,
                          items=_causal_item_list(S // tqs, tqs, tks)),
        out_shape=jax.ShapeDtypeStruct((B, S, H * d), BF16),
        grid=(B, G),
        in_specs=[pl.BlockSpec((1, S, hpg * LANE), lambda b, g: (b, 0, g)),
                  pl.BlockSpec((1, hpg, LANE), lambda b, g: (g, 0, 0)),
                  pl.BlockSpec((1, S, LANE), lambda b, g: (b, 0, 0)),
                  cmpspec, cmpspec, kvspec, kvspec, kvspec, kvspec,
                  pl.BlockSpec((S, LANE), lambda b, g: (0, 0)),
                  pl.BlockSpec((nrow, LANE), lambda b, g: (0, 0)),
                  pl.BlockSpec((nrow, LANE), lambda b, g: (0, 0)),
                  pl.BlockSpec((1, LANE, 3 * hpg * d), lambda b, g: (g, 0, 0))],
        out_specs=pl.BlockSpec((1, S, hpg * d), lambda b, g: (b, 0, g)),
        scratch_shapes=[pltpu.VMEM((S, LANE), BF16), pltpu.VMEM((S, hpg * d), F32), pltpu.VMEM((S, hpg * d), F32)]
        + _flat_scratch(hpg // 2, 2 * tqs, tks),
        compiler_params=_cparams("parallel", "parallel"),
        name="nsa_attn",
    )(q.reshape(B, S, H * LANE), qcoef, gl.reshape(B, S, LANE), kcmp, vcmp,
      ks.reshape(B, S, G * LANE), vs.reshape(B, S, G * LANE), kw.reshape(B, S, G * LANE), vw.reshape(B, S, G * LANE),
      jnp.asarray(ktab, BF16), jnp.asarray(ctab, BF16), jnp.asarray(c2s, BF16), jnp.asarray(rep, BF16))
    return _outproj_ln(o.reshape(N, H * d), w_o, x, ln_g, ln_b, name="nsa_out_ln")


def _dil_attn_kernel(slope_ref, q_ref, k_ref, v_ref, o_ref, lse_ref, *, t, dil):
    i = pl.program_id(2)
    d = DIL_HD
    rel = _rel_pos(t, t)
    relf = rel.astype(F32)
    rel2 = jnp.concatenate([rel, rel], axis=0)
    lane = lax.broadcasted_iota(jnp.int32, (t, LANE), 1)
    scale = d ** -0.5 * LOG2E

    def chunk(c):
        return pl.ds(pl.multiple_of(c * t, t), t)

    cs = [jnp.maximum(i - 1, 0), i]
    viss = [(rel2 >= 0) & (i >= 1), rel2 <= 0]
    lse_t = jnp.zeros((t, LANE), F32)
    for hp in range(DIL_HEADS // 2):
        sl = slice(hp * LANE, (hp + 1) * LANE)
        qf = q_ref[0, :, sl].astype(F32) * scale
        q2 = jnp.concatenate([jnp.where(lane < d, qf, 0.0), jnp.where(lane < d, 0.0, qf)], axis=0).astype(BF16)
        sa, sb = slope_ref[2 * hp] * (dil * LOG2E), slope_ref[2 * hp + 1] * (dil * LOG2E)
        bias = jnp.concatenate([sa * relf, sb * relf], axis=0)
        shift = jnp.concatenate([jnp.full((t, 1), sa * t, F32), jnp.full((t, 1), sb * t, F32)], axis=0)
        m, l, acc = _window_attention(q2, [k_ref[0, chunk(c), sl] for c in cs], [v_ref[0, chunk(c), sl] for c in cs],
                                      viss, [bias - shift, bias])
        o = acc / l
        o_ref[0, :, hp * LANE:(hp + 1) * LANE] = jnp.where(lane < d, o[:t], o[t:]).astype(o_ref.dtype)
        lse = m + jnp.log2(l)
        lse_t = jnp.where(lane == 2 * hp, lse[:t], lse_t)
        lse_t = jnp.where(lane == 2 * hp + 1, lse[t:], lse_t)
    lse_ref[0] = lse_t


def _dil_proj_kernel(x_ref, w_ref, *refs, tm):
    outs, scr = refs[:-1], refs[-1]
    xb = x_ref[...].astype(BF16)
    width = scr.shape[0] * LANE
    for p, (o_ref, (_, dil)) in enumerate(zip(outs, DIL_PATTERNS)):
        for c0 in range(0, width, 512):
            y = jnp.dot(xb, w_ref[:, p * width + c0:p * width + c0 + 512], preferred_element_type=F32)
            if dil == 1:
                o_ref[:, c0:c0 + 512] = y.astype(o_ref.dtype)
            else:
                for j in range(512 // LANE):
                    scr[c0 // LANE + j] = y[:, j * LANE:(j + 1) * LANE]
        for c in range(dil if dil > 1 else 0):
            for j in range(width // LANE):
                o_ref[:, c * width + j * LANE:c * width + (j + 1) * LANE] = (
                    scr[j, pl.ds(c, tm // dil, stride=dil), :].astype(o_ref.dtype))


def _dil_merge_ln_kernel(o0_ref, o1_ref, o2_ref, l0_ref, l1_ref, l2_ref, rep_ref, w_ref, x_ref, g_ref, b_ref, out_ref,
                         scr_o, scr_l):
    tm = x_ref.shape[0]

    def token_order(ref, dil, scr):
        if dil == 1:
            return ref[...].astype(F32)
        w = ref.shape[1] // dil
        for c in range(dil):
            for j in range(w // LANE):
                scr[j, pl.ds(c, tm // dil, stride=dil), :] = (
                    ref[:, c * w + j * LANE:c * w + (j + 1) * LANE].astype(F32))
        return jnp.concatenate([scr[j] for j in range(w // LANE)], axis=1)

    ls = [token_order(l_ref, dil, scr_l) for l_ref, (_, dil) in zip((l0_ref, l1_ref, l2_ref), DIL_PATTERNS)]
    mx = jnp.maximum(jnp.maximum(ls[0], ls[1]), ls[2])
    es = [jnp.exp2(v - mx) for v in ls]
    tot = es[0] + es[1] + es[2]
    o = None
    for e, o_ref, (_, dil) in zip(es, (o0_ref, o1_ref, o2_ref), DIL_PATTERNS):
        wh, wl = _split_bf16(e / tot)
        wrep = jnp.dot(wh, rep_ref[...], preferred_element_type=F32) + jnp.dot(wl, rep_ref[...], preferred_element_type=F32)
        term = wrep * token_order(o_ref, dil, scr_o)
        o = term if o is None else o + term
    y = jnp.dot(o.astype(BF16), w_ref[...], preferred_element_type=F32)
    out_ref[...] = _layer_norm_rows(DN_ALPHA * x_ref[...] + y, g_ref[...], b_ref[...])


def _dil_mixer(x, w_in, w_o, ln_g, ln_b, B, S):
    N, D = x.shape
    H, d = DIL_HEADS, DIL_HD
    n_pat = len(DIL_PATTERNS)
    hd = H * d
    width = 3 * hd
    ncol = n_pat * width
    tm = 512
    qkvs = pl.pallas_call(
        functools.partial(_dil_proj_kernel, tm=tm),
        out_shape=[jax.ShapeDtypeStruct((N // dil, dil * width), BF16) for _, dil in DIL_PATTERNS],
        grid=(N // tm,),
        in_specs=[pl.BlockSpec((tm, D), lambda i: (i, 0)), pl.BlockSpec((D, ncol), lambda i: (0, 0))],
        out_specs=[pl.BlockSpec((tm // dil, dil * width), lambda i: (i, 0)) for _, dil in DIL_PATTERNS],
        scratch_shapes=[pltpu.VMEM((width // LANE, tm, LANE), F32)],
        compiler_params=_cparams("parallel"),
        name="dil_proj",
    )(x, w_in.astype(BF16))
    tq = 128
    slopes = jnp.asarray(_alibi(H))
    outs, lses = [], []
    for p, (win, dil) in enumerate(DIL_PATTERNS):
        ls = S // dil
        assert ls % tq == 0 and win == dil * tq
        view = qkvs[p].reshape(B, ls, dil * width)
        o, lse = pl.pallas_call(
            functools.partial(_dil_attn_kernel, t=tq, dil=dil),
            out_shape=[jax.ShapeDtypeStruct((B, ls, dil * hd), BF16),
                       jax.ShapeDtypeStruct((B, ls, dil * LANE), F32)],
            grid=(B, dil, ls // tq),
            in_specs=[pl.BlockSpec(memory_space=pltpu.SMEM),
                      pl.BlockSpec((1, tq, hd), lambda b, r, i: (b, i, 3 * r)),
                      pl.BlockSpec((1, ls, hd), lambda b, r, i: (b, 0, 3 * r + 1)),
                      pl.BlockSpec((1, ls, hd), lambda b, r, i: (b, 0, 3 * r + 2))],
            out_specs=[pl.BlockSpec((1, tq, hd), lambda b, r, i: (b, i, r)),
                       pl.BlockSpec((1, tq, LANE), lambda b, r, i: (b, i, r))],
            compiler_params=_cparams("parallel", "parallel", "arbitrary"),
            name=f"dil_attn_{p}",
        )(slopes, view, view, view)
        outs.append(o.reshape(N // dil, dil * hd))
        lses.append(lse.reshape(N // dil, dil * LANE))

    rep = np.zeros((LANE, hd), np.float32)
    for h in range(H):
        rep[h, h * d:(h + 1) * d] = 1.0
    row = lambda n: pl.BlockSpec((tm, n), lambda i: (i, 0))
    rowp = lambda n: [pl.BlockSpec((tm // dil, dil * n), lambda i: (i, 0)) for _, dil in DIL_PATTERNS]
    full = lambda a, b: pl.BlockSpec((a, b), lambda i: (0, 0))
    return pl.pallas_call(
        _dil_merge_ln_kernel,
        out_shape=jax.ShapeDtypeStruct((N, D), F32),
        grid=(N // tm,),
        in_specs=rowp(hd) + rowp(LANE) + [full(LANE, hd), full(hd, D), row(D), full(1, D), full(1, D)],
        out_specs=row(D),
        scratch_shapes=[pltpu.VMEM((hd // LANE, tm, LANE), F32), pltpu.VMEM((1, tm, LANE), F32)],
        compiler_params=_cparams("parallel"),
        name="dil_merge_out_ln",
    )(*outs, *lses, jnp.asarray(rep, BF16), w_o.astype(BF16), x, ln_g.reshape(1, D), ln_b.reshape(1, D))


def _split_bf16(a):
    hi = a.astype(BF16)
    return hi, (a - hi.astype(F32)).astype(BF16)


META_BKT, META_RANK = 2, 3
MOE_PAIRS = MOE_EPG * (MOE_EPG - 1) // 2
MOE_BUCKETS = MOE_GROUPS * MOE_PAIRS
PAIR_LO = [a for a in range(MOE_EPG) for b in range(a + 1, MOE_EPG)]
PAIR_HI = [b for a in range(MOE_EPG) for b in range(a + 1, MOE_EPG)]


def _router_kernel(x_ref, wh_ref, wl_ref, b_ref, tri_ref, meta_ref, cnt_ref, run_ref):
    @pl.when(pl.program_id(0) == 0)
    def _():
        run_ref[...] = jnp.zeros_like(run_ref)

    xh, xl = _split_bf16(x_ref[...])
    wh, wl = wh_ref[...], wl_ref[...]
    logits = (jnp.dot(xh, wh, preferred_element_type=F32) + jnp.dot(xl, wh, preferred_element_type=F32)
              + jnp.dot(xh, wl, preferred_element_type=F32)) + b_ref[...]
    tm = logits.shape[0]
    lane = lax.broadcasted_iota(jnp.int32, (tm, LANE), 1)
    big = jnp.int32(LANE)
    lg = jnp.where(lane < MOE_GROUPS, logits, -jnp.inf)
    mg = jnp.max(lg, axis=-1, keepdims=True)
    sg = jnp.sum(jnp.exp(lg - mg), axis=-1, keepdims=True)
    pg_top = 1.0 / sg
    g_top = jnp.min(jnp.where(lg == mg, lane, big), axis=-1, keepdims=True)
    e_lo = MOE_GROUPS + MOE_EPG * g_top
    in_grp = (lane >= e_lo) & (lane < e_lo + MOE_EPG)
    le = jnp.where(in_grp, logits, -jnp.inf)
    me = jnp.max(le, axis=-1, keepdims=True)
    ee = jnp.exp(le - me)
    se = jnp.sum(ee, axis=-1, keepdims=True)
    pe = jnp.where(in_grp, ee / se, -1.0)
    v1 = jnp.max(pe, axis=-1, keepdims=True)
    i1 = jnp.min(jnp.where(pe == v1, lane, big), axis=-1, keepdims=True)
    pe2 = jnp.where(lane == i1, -1.0, pe)
    v2 = jnp.max(pe2, axis=-1, keepdims=True)
    i2 = jnp.min(jnp.where(pe2 == v2, lane, big), axis=-1, keepdims=True)
    tot = v1 + v2
    a1, a2 = i1 - e_lo, i2 - e_lo
    lo, hi = jnp.minimum(a1, a2), jnp.maximum(a1, a2)
    pair = lo * (MOE_EPG - 1) - (lo * (lo - 1)) // 2 + hi - lo - 1
    bucket = g_top * MOE_PAIRS + pair
    g_first, g_second = (v1 / tot) * pg_top, (v2 / tot) * pg_top
    gates = (jnp.where(lane == 0, jnp.where(a1 < a2, g_first, g_second), 0.0)
             + jnp.where(lane == 1, jnp.where(a1 < a2, g_second, g_first), 0.0))
    onehot = lane == bucket
    prefix = jnp.dot(tri_ref[...], jnp.where(onehot, 1.0, 0.0).astype(BF16), preferred_element_type=F32)
    rank = jnp.sum(jnp.where(onehot, prefix + run_ref[...] - 1.0, 0.0), axis=-1, keepdims=True)
    meta_ref[...] = (gates + jnp.where(lane == META_BKT, bucket.astype(F32), 0.0)
                     + jnp.where(lane == META_RANK, rank, 0.0))
    run_ref[...] += prefix[tm - 1:tm, :]
    cnt_ref[...] = run_ref[...]


def _moe_dispatch_kernel(starts_ref, ends_ref, bkt_ref, rank_ref, x_ref, meta_ref, xs_hbm, buf, sem):
    tm, d = x_ref.shape

    @pl.when(pl.program_id(0) == 0)
    def _():
        buf[...] = jnp.zeros_like(buf)

        def fill_tile(t, carry):
            fill = pltpu.make_async_copy(buf, xs_hbm.at[pl.ds(pl.multiple_of(t * tm, tm), tm)], sem)
            fill.start()
            fill.wait()
            return carry

        for b in range(MOE_BUCKETS):
            @pl.when(ends_ref[b] > starts_ref[b])
            def _(b=b):
                fill_tile(ends_ref[b] // tm - 1, 0)
        lax.fori_loop(ends_ref[MOE_BUCKETS - 1] // tm, xs_hbm.shape[0] // tm, fill_tile, 0)

    buf[:, :d] = x_ref[...]
    buf[:, d:] = meta_ref[...]

    def issue(r, carry):
        slot = starts_ref[bkt_ref[0, 0, r]] + rank_ref[0, 0, r]
        pltpu.make_async_copy(buf.at[pl.ds(r, 1)], xs_hbm.at[pl.ds(slot, 1)], sem).start()
        return carry

    lax.fori_loop(0, tm, issue, 0, unroll=8)
    pltpu.make_async_copy(buf, xs_hbm.at[pl.ds(0, tm)], sem).wait()


def _moe_pair_kernel(ea_ref, eb_ref, used_ref, xs_ref, w1a_ref, w3a_ref, w2a_ref, w1b_ref, w3b_ref, w2b_ref, g_ref,
                     b_ref, ys_ref):
    del ea_ref, eb_ref
    d = ys_ref.shape[1]

    @pl.when(pl.program_id(0) < used_ref[0])
    def _():
        x = xs_ref[:, :d]
        gates = xs_ref[:, d:]
        lane = lax.broadcasted_iota(jnp.int32, gates.shape, 1)
        xb = x.astype(BF16)
        y = None
        for e, (w1_ref, w3_ref, w2_ref) in enumerate(((w1a_ref, w3a_ref, w2a_ref), (w1b_ref, w3b_ref, w2b_ref))):
            h1 = jnp.dot(xb, w1_ref[0], preferred_element_type=F32)
            h3 = jnp.dot(xb, w3_ref[0], preferred_element_type=F32)
            hid = (h1 * jax.nn.sigmoid(h1) * h3).astype(BF16)
            ge = jnp.sum(jnp.where(lane == e, gates, 0.0), axis=-1, keepdims=True)
            term = ge * jnp.dot(hid, w2_ref[0], preferred_element_type=F32)
            y = term if y is None else y + term
        ys_ref[...] = _layer_norm_rows(DN_ALPHA * x + y, g_ref[...], b_ref[...])

    @pl.when(pl.program_id(0) >= used_ref[0])
    def _():
        ys_ref[...] = jnp.zeros_like(ys_ref)


def _moe_collect_kernel(starts_ref, bkt_ref, rank_ref, ys_hbm, out_ref, sem):
    tm = out_ref.shape[0]

    def issue(r, carry):
        slot = starts_ref[bkt_ref[0, 0, r]] + rank_ref[0, 0, r]
        pltpu.make_async_copy(ys_hbm.at[pl.ds(slot, 1)], out_ref.at[pl.ds(r, 1)], sem).start()
        return carry

    lax.fori_loop(0, tm, issue, 0, unroll=8)
    pltpu.make_async_copy(ys_hbm.at[pl.ds(0, tm)], out_ref, sem).wait()


def _hier_moe_ln(x, wg, bg, we, be, w1, w3, w2, ln_g, ln_b):
    N, D = x.shape
    G, E, FF, NB = MOE_GROUPS, MOE_EPG, MOE_FF, MOE_BUCKETS
    wr = jnp.concatenate([wg, jnp.moveaxis(we, 0, 1).reshape(D, MOE_EXPERTS),
                          jnp.zeros((D, LANE - G - MOE_EXPERTS), F32)], axis=1)
    br = jnp.concatenate([bg, be.reshape(-1), jnp.zeros((LANE - G - MOE_EXPERTS,), F32)]).reshape(1, LANE)
    wrh, wrl = _split_bf16(wr)
    tm = MOE_TM
    tri = jnp.asarray(np.tril(np.ones((tm, tm), np.float32)), BF16)
    meta, cnt = pl.pallas_call(
        _router_kernel,
        out_shape=[jax.ShapeDtypeStruct((N, LANE), F32), jax.ShapeDtypeStruct((1, LANE), F32)],
        grid=(N // tm,),
        in_specs=[pl.BlockSpec((tm, D), lambda i: (i, 0)),
                  pl.BlockSpec((D, LANE), lambda i: (0, 0)),
                  pl.BlockSpec((D, LANE), lambda i: (0, 0)),
                  pl.BlockSpec((1, LANE), lambda i: (0, 0)),
                  pl.BlockSpec((tm, tm), lambda i: (0, 0))],
        out_specs=[pl.BlockSpec((tm, LANE), lambda i: (i, 0)), pl.BlockSpec((1, LANE), lambda i: (0, 0))],
        scratch_shapes=[pltpu.VMEM((1, LANE), F32)],
        compiler_params=_cparams("arbitrary"),
        name="moe_router",
    )(x, wrh, wrl, br, tri)

    counts = cnt[0, :NB].astype(jnp.int32)
    padded = (counts + tm - 1) // tm * tm
    ends = jnp.cumsum(padded)
    starts = ends - padded
    per_token = lambda lane: meta[:, lane].astype(jnp.int32).reshape(N // tm, 1, tm)
    bkt, rank = per_token(META_BKT), per_token(META_RANK)
    n_pad = N + NB * tm
    n_tiles = n_pad // tm
    tile_row0 = jnp.arange(n_tiles, dtype=jnp.int32) * tm
    tile_bkt = jnp.minimum(jnp.sum((ends[None, :] <= tile_row0[:, None]).astype(jnp.int32), axis=1), NB - 1)
    tile_ea = (tile_bkt // MOE_PAIRS) * E + jnp.asarray(PAIR_LO, jnp.int32)[tile_bkt % MOE_PAIRS]
    tile_eb = (tile_bkt // MOE_PAIRS) * E + jnp.asarray(PAIR_HI, jnp.int32)[tile_bkt % MOE_PAIRS]

    xs = pl.pallas_call(
        _moe_dispatch_kernel,
        out_shape=jax.ShapeDtypeStruct((n_pad, D + LANE), F32),
        grid_spec=pltpu.PrefetchScalarGridSpec(
            num_scalar_prefetch=2,
            grid=(N // tm,),
            in_specs=[pl.BlockSpec((1, 1, tm), lambda i, st, en: (i, 0, 0), memory_space=pltpu.SMEM),
                      pl.BlockSpec((1, 1, tm), lambda i, st, en: (i, 0, 0), memory_space=pltpu.SMEM),
                      pl.BlockSpec((tm, D), lambda i, st, en: (i, 0)),
                      pl.BlockSpec((tm, LANE), lambda i, st, en: (i, 0))],
            out_specs=pl.BlockSpec(memory_space=pl.ANY),
            scratch_shapes=[pltpu.VMEM((tm, D + LANE), F32), pltpu.SemaphoreType.DMA]),
        compiler_params=_cparams("arbitrary"),
        name="moe_dispatch",
    )(starts, ends, bkt, rank, x, meta)

    wa = lambda a, b: pl.BlockSpec((1, a, b), lambda t, ea, eb, nu: (ea[t], 0, 0))
    wb = lambda a, b: pl.BlockSpec((1, a, b), lambda t, ea, eb, nu: (eb[t], 0, 0))
    w1b, w3b, w2b = w1.astype(BF16), w3.astype(BF16), w2.astype(BF16)
    ys = pl.pallas_call(
        _moe_pair_kernel,
        out_shape=jax.ShapeDtypeStruct((n_pad, D), F32),
        grid_spec=pltpu.PrefetchScalarGridSpec(
            num_scalar_prefetch=3,
            grid=(n_tiles,),
            in_specs=[pl.BlockSpec((tm, D + LANE), lambda t, ea, eb, nu: (t, 0)),
                      wa(D, FF), wa(D, FF), wa(FF, D), wb(D, FF), wb(D, FF), wb(FF, D),
                      pl.BlockSpec((1, D), lambda t, ea, eb, nu: (0, 0)),
                      pl.BlockSpec((1, D), lambda t, ea, eb, nu: (0, 0))],
            out_specs=pl.BlockSpec((tm, D), lambda t, ea, eb, nu: (t, 0))),
        compiler_params=_cparams("arbitrary"),
        name="moe_experts_ln",
    )(tile_ea, tile_eb, (ends[-1:] // tm).astype(jnp.int32), xs, w1b, w3b, w2b, w1b, w3b, w2b,
      ln_g.reshape(1, D), ln_b.reshape(1, D))

    return pl.pallas_call(
        _moe_collect_kernel,
        out_shape=jax.ShapeDtypeStruct((N, D), F32),
        grid_spec=pltpu.PrefetchScalarGridSpec(
            num_scalar_prefetch=1,
            grid=(N // tm,),
            in_specs=[pl.BlockSpec((1, 1, tm), lambda i, st: (i, 0, 0), memory_space=pltpu.SMEM),
                      pl.BlockSpec((1, 1, tm), lambda i, st: (i, 0, 0), memory_space=pltpu.SMEM),
                      pl.BlockSpec(memory_space=pl.ANY)],
            out_specs=pl.BlockSpec((tm, D), lambda i, st: (i, 0)),
            scratch_shapes=[pltpu.SemaphoreType.DMA]),
        compiler_params=_cparams("arbitrary"),
        name="moe_collect",
    )(starts, bkt, rank, ys)


def kernel(x, mla_w_in, mla_q_norm, mla_kv_norm, mla_w_qb, mla_w_kvb, mla_w_o, nsa_w_in, nsa_w_phi_k, nsa_w_phi_v, nsa_cmp_pos, nsa_w_o, diff_w_in, diff_lam_q1, diff_lam_k1, diff_lam_q2, diff_lam_k2, diff_subln, diff_w_o, dil_w_in, dil_w_o, ln1_g, ln1_b, ln2_g, ln2_b, moe_wg, moe_bg, moe_we, moe_be, moe_w1, moe_w3, moe_w2):
    B, S, D = x.shape
    h = x.reshape(B * S, D)
    for i in range(DEPTH):
        m, j = i % 4, i // 4
        if m == 0:
            h = _mla_mixer(h, mla_w_in[j], mla_q_norm[j], mla_kv_norm[j], mla_w_qb[j], mla_w_kvb[j], mla_w_o[j],
                           ln1_g[i], ln1_b[i], B, S)
        elif m == 1:
            h = _nsa_mixer(h, nsa_w_in[j], nsa_w_phi_k[j], nsa_w_phi_v[j], nsa_cmp_pos[j], nsa_w_o[j],
                           ln1_g[i], ln1_b[i], B, S)
        elif m == 2:
            h = _diff_mixer(h, diff_w_in[j], diff_lam_q1[j], diff_lam_k1[j], diff_lam_q2[j], diff_lam_k2[j],
                            diff_subln[j], diff_w_o[j], i, ln1_g[i], ln1_b[i], B, S)
        else:
            h = _dil_mixer(h, dil_w_in[j], dil_w_o[j], ln1_g[i], ln1_b[i], B, S)
        h = _hier_moe_ln(h, moe_wg[i], moe_bg[i], moe_we[i], moe_be[i], moe_w1[i], moe_w3[i], moe_w2[i],
                         ln2_g[i], ln2_b[i])
    return h.reshape(B, S, D)
```

```python
import functools
import math

import numpy as np
import jax
import jax.numpy as jnp
from jax import lax
from jax.experimental import pallas as pl
from jax.experimental.pallas import tpu as pltpu

F32 = jnp.float32
BF16 = jnp.bfloat16

DEPTH = 4
DN_ALPHA = (2.0 * DEPTH) ** 0.25
LN_EPS = 1e-5
NEG_INF = -1e30
LOG2E = math.log2(math.e)
LANE = 128
VMEM_LIMIT = 56 * 1024 * 1024
ATTN_TQ, ATTN_TK = 256, 512
MOE_TM = 512

MLA_HEADS, MLA_Q_RANK, MLA_KV_RANK, MLA_NOPE, MLA_ROPE, MLA_V = 16, 384, 256, 64, 32, 64
ROPE_THETA = 10000.0
NSA_HEADS, NSA_GROUPS, NSA_HD = 16, 4, 64
NSA_CMP_LEN, NSA_CMP_STRIDE, NSA_SLC_LEN, NSA_TOP_N, NSA_WINDOW, NSA_FORCE = 32, 16, 64, 8, 512, 1e4
DIFF_HEADS, DIFF_HD = 8, 64
DIL_PATTERNS = ((128, 1), (512, 4), (2048, 16))
DIL_HEADS, DIL_HD = 8, 64
MOE_GROUPS, MOE_EPG, MOE_EXPERTS, MOE_FF = 4, 4, 16, 512


def _cparams(*sem):
    return pltpu.CompilerParams(dimension_semantics=sem, vmem_limit_bytes=VMEM_LIMIT)


def _alibi(n):
    return np.asarray(2.0 ** (-8.0 * np.arange(1, n + 1) / n), np.float32)


def _linear_kernel(x_ref, w_ref, *out_refs, splits, chunk):
    xb = x_ref[...].astype(BF16)
    col = 0
    for o_ref, n in zip(out_refs, splits):
        for c0 in range(0, n, chunk):
            cw = min(chunk, n - c0)
            o_ref[:, c0:c0 + cw] = jnp.dot(
                xb, w_ref[:, col + c0:col + c0 + cw], preferred_element_type=F32).astype(o_ref.dtype)
        col += n


def _linear(x, w, splits, dtypes, tm=512, name="linear"):
    M, K = x.shape
    ntot = sum(splits)
    assert w.shape == (K, ntot) and M % tm == 0 and all(n % LANE == 0 for n in splits)
    outs = pl.pallas_call(
        functools.partial(_linear_kernel, splits=tuple(splits), chunk=512),
        out_shape=[jax.ShapeDtypeStruct((M, n), d) for n, d in zip(splits, dtypes)],
        grid=(M // tm,),
        in_specs=[pl.BlockSpec((tm, K), lambda i: (i, 0)),
                  pl.BlockSpec((K, ntot), lambda i: (0, 0))],
        out_specs=[pl.BlockSpec((tm, n), lambda i: (i, 0)) for n in splits],
        compiler_params=_cparams("parallel"),
        name=name,
    )(x, w)
    return outs


def _layer_norm_rows(z, g, b):
    mu = jnp.mean(z, axis=-1, keepdims=True)
    zc = z - mu
    var = jnp.mean(zc * zc, axis=-1, keepdims=True)
    return zc * lax.rsqrt(var + LN_EPS) * g + b


def _outproj_ln_kernel(o_ref, w_ref, x_ref, g_ref, b_ref, out_ref):
    y = jnp.dot(o_ref[...].astype(BF16), w_ref[...], preferred_element_type=F32)
    out_ref[...] = _layer_norm_rows(DN_ALPHA * x_ref[...] + y, g_ref[...], b_ref[...])


def _outproj_ln(o, w_o, x, g, b, tm=512, name="outproj_ln"):
    M, K = o.shape
    D = x.shape[1]
    return pl.pallas_call(
        _outproj_ln_kernel,
        out_shape=jax.ShapeDtypeStruct((M, D), F32),
        grid=(M // tm,),
        in_specs=[pl.BlockSpec((tm, K), lambda i: (i, 0)),
                  pl.BlockSpec((K, D), lambda i: (0, 0)),
                  pl.BlockSpec((tm, D), lambda i: (i, 0)),
                  pl.BlockSpec((1, D), lambda i: (0, 0)),
                  pl.BlockSpec((1, D), lambda i: (0, 0))],
        out_specs=pl.BlockSpec((tm, D), lambda i: (i, 0)),
        compiler_params=_cparams("parallel"),
        name=name,
    )(o, w_o.astype(BF16), x, g.reshape(1, D), b.reshape(1, D))


def _scores(q, k, bias):
    s = lax.dot_general(q, k, (((1,), (1,)), ((), ())), preferred_element_type=F32)
    return s if bias is None else s + bias


def _causal_item_list(n_q, tq, tk):
    items = []
    for i in range(n_q):
        c_d = (i * tq) // tk
        items += [(i, c, c == 0, c == c_d) for c in range(c_d + 1)]
    return tuple(items)


def _flat_scratch(n_streams, M, t):
    return [pltpu.VMEM((n_streams, 2, M, t), F32), pltpu.VMEM((n_streams, 2, M, t), BF16),
            pltpu.VMEM((n_streams, M, LANE), F32), pltpu.VMEM((n_streams, 2, 3, M, LANE), F32)]


def _flash_static(items, streams, finalize, scratch, tq, tk):
    s_ref, p_ref, acc_ref, st_ref = scratch
    ns = len(streams)
    reps = tk // LANE
    n_rep = s_ref.shape[2] // tq

    def scores(n, item):
        q_at, k_at, _, bias_at = streams[n]
        return _scores(q_at(item[0]), k_at(item[1]), None if bias_at is None else bias_at(item[1]))

    def lane_partial_sum(p):
        return functools.reduce(jnp.add, [p[:, r * LANE:(r + 1) * LANE] for r in range(reps)])

    def value_update(n, prev, slot_prev):
        pv = jnp.dot(p_ref[n, slot_prev], streams[n][2](prev[1]), preferred_element_type=F32)
        acc_ref[n] = pv if prev[2] else st_ref[n, slot_prev, 2] * acc_ref[n] + pv

    def finish(prev, slot_prev):
        finalize(prev[0], [(jnp.sum(st_ref[n, slot_prev, 1], axis=-1, keepdims=True), acc_ref[n]) for n in range(ns)])

    for n in range(ns):
        s_ref[n, 0] = scores(n, items[0])
    for j, item in enumerate(items):
        slot = j % 2
        i, c, first, last = item
        if j + 1 < len(items):
            for n in range(ns):
                s_ref[n, 1 - slot] = scores(n, items[j + 1])
        if j > 0:
            for n in range(ns):
                value_update(n, items[j - 1], 1 - slot)
        vis = None
        if (c + 1) * tk > i * tq + 1:
            vis = _rel_pos(tq, tk) + (c * tk - i * tq) <= 0
            if n_rep > 1:
                vis = jnp.concatenate([vis] * n_rep, axis=0)
        for n in range(ns):
            s = s_ref[n, slot]
            if vis is not None:
                s = jnp.where(vis, s, NEG_INF)
            m_cur = jnp.max(s, axis=-1, keepdims=True)
            if first:
                m_new = jnp.broadcast_to(m_cur, (s.shape[0], LANE))
            else:
                m_old = st_ref[n, 1 - slot, 0]
                m_new = jnp.maximum(m_old, m_cur)
                alpha = jnp.exp2(m_old - m_new)
                st_ref[n, slot, 2] = alpha
            p = jnp.exp2(s - jnp.concatenate([m_new] * reps, axis=1))
            st_ref[n, slot, 0] = m_new
            st_ref[n, slot, 1] = lane_partial_sum(p) if first else alpha * st_ref[n, 1 - slot, 1] + lane_partial_sum(p)
            p_ref[n, slot] = p.astype(BF16)
        if j > 0 and items[j - 1][3]:
            finish(items[j - 1], 1 - slot)
    last_slot = (len(items) - 1) % 2
    for n in range(ns):
        value_update(n, items[-1], last_slot)
    finish(items[-1], last_slot)


def _window_attention(q, ks, vs, viss, biases=None):
    ss = []
    for j, (k, vis) in enumerate(zip(ks, viss)):
        s = _scores(q, k, None if biases is None else biases[j])
        ss.append(s if vis is None else jnp.where(vis, s, NEG_INF))
    m = jnp.max(functools.reduce(jnp.maximum, ss), axis=-1, keepdims=True)
    ps = [jnp.exp2(s - m) for s in ss]
    l = jnp.sum(functools.reduce(jnp.add, ps), axis=-1, keepdims=True)
    acc = functools.reduce(jnp.add, [jnp.dot(p.astype(BF16), v, preferred_element_type=F32) for p, v in zip(ps, vs)])
    return m, l, acc


def _rel_pos(rows, cols):
    return (lax.broadcasted_iota(jnp.int32, (rows, cols), 1)
            - lax.broadcasted_iota(jnp.int32, (rows, cols), 0))


def _rms_rows(c, g, eps):
    return c * lax.rsqrt(jnp.mean(c * c, axis=-1, keepdims=True) + eps) * g


def _mla_proj_kernel(x_ref, win_ref, qn_ref, kvn_ref, wq_ref, wqs_ref, wk_ref, wv_ref, cos_ref, sin_ref,
                     q_out, k_out, v_out, *, scale):
    xb = x_ref[...].astype(BF16)
    c = jnp.dot(xb, win_ref[...], preferred_element_type=F32)
    r0, r1 = MLA_Q_RANK, MLA_Q_RANK + MLA_KV_RANK
    cq = _rms_rows(c[:, :r0], qn_ref[...], 1e-6).astype(BF16)
    ckv = _rms_rows(c[:, r0:r1], kvn_ref[...], 1e-6).astype(BF16)
    cos, sin = cos_ref[...], sin_ref[...]
    kr = c[:, r1:r1 + LANE] * cos + c[:, r1 + LANE:r1 + 2 * LANE] * sin
    v_out[...] = jnp.dot(ckv, wv_ref[...], preferred_element_type=F32).astype(v_out.dtype)
    cos2, sin2, kr2 = (jnp.concatenate([t, t], axis=1) for t in (cos, sin, kr))
    for hp in range(MLA_HEADS // 2):
        sl = slice(2 * hp * LANE, 2 * (hp + 1) * LANE)
        qh = jnp.dot(cq, wq_ref[:, sl], preferred_element_type=F32)
        qhs = jnp.dot(cq, wqs_ref[:, sl], preferred_element_type=F32)
        q_out[:, sl] = ((qh * cos2 + qhs * sin2) * scale).astype(q_out.dtype)
        kh = jnp.dot(ckv, wk_ref[:, sl], preferred_element_type=F32)
        k_out[:, sl] = (kh + kr2).astype(k_out.dtype)


def _mla_attn_kernel(q_ref, k_ref, v_ref, o_ref, *scratch, tq, tk, items):
    def rows(i, t):
        return pl.ds(i * t, t)

    def v_at(c):
        return v_ref[0, rows(c, tk), :]

    streams = []
    for hh in range(2):
        sl = slice(hh * LANE, (hh + 1) * LANE)
        streams.append((lambda i, sl=sl: q_ref[0, rows(i, tq), sl], lambda c, sl=sl: k_ref[0, rows(c, tk), sl],
                        v_at, None))
    lane = lax.broadcasted_iota(jnp.int32, (tq, LANE), 1)

    def finalize(i, res):
        (la, acca), (lb, accb) = res
        o_ref[0, rows(i, tq), :] = jnp.where(lane < MLA_V, acca / la, accb / lb).astype(o_ref.dtype)

    _flash_static(items, streams, finalize, scratch, tq, tk)


def _mla_mixer(x, w_in, q_norm, kv_norm, w_qb, w_kvb, w_o, ln_g, ln_b, B, S):
    N, D = x.shape
    H, dq = MLA_HEADS, MLA_NOPE + MLA_ROPE
    half = MLA_ROPE // 2
    r0, r1 = MLA_Q_RANK, MLA_Q_RANK + MLA_KV_RANK
    z = lambda *s: jnp.zeros(s, F32)
    swap = lambda a: jnp.concatenate([a[..., half:], a[..., :half]], axis=-1)
    kr_w = w_in[:, r1:]
    win = jnp.concatenate([w_in[:, :r1],
                           z(D, MLA_NOPE), kr_w, z(D, LANE - dq),
                           z(D, MLA_NOPE), swap(kr_w), z(D, LANE - dq)], axis=1).astype(BF16)
    wq3 = w_qb.reshape(r0, H, dq)
    wq = jnp.concatenate([wq3, z(r0, H, LANE - dq)], axis=-1).reshape(r0, H * LANE).astype(BF16)
    wqs = jnp.concatenate([z(r0, H, MLA_NOPE), swap(wq3[..., MLA_NOPE:]), z(r0, H, LANE - dq)],
                          axis=-1).reshape(r0, H * LANE).astype(BF16)
    wkv3 = w_kvb.reshape(MLA_KV_RANK, H, MLA_NOPE + MLA_V)
    wk = jnp.concatenate([wkv3[..., :MLA_NOPE], z(MLA_KV_RANK, H, LANE - MLA_NOPE)],
                         axis=-1).reshape(MLA_KV_RANK, H * LANE).astype(BF16)
    wv = wkv3[..., MLA_NOPE:].reshape(MLA_KV_RANK, H * MLA_V).astype(BF16)
    freq = ROPE_THETA ** (-jnp.arange(half, dtype=F32) / half)
    ang = jnp.arange(S, dtype=F32)[:, None] * freq
    cos, sin = jnp.cos(ang), jnp.sin(ang)
    ones, zer = jnp.ones((S, MLA_NOPE), F32), jnp.zeros((S, LANE - dq), F32)
    cos_t = jnp.concatenate([ones, cos, cos, zer], axis=1)
    sin_t = jnp.concatenate([0 * ones, -sin, sin, zer], axis=1)

    tm = 256
    nwin = win.shape[1]
    q, k, v = pl.pallas_call(
        functools.partial(_mla_proj_kernel, scale=dq ** -0.5 * LOG2E),
        out_shape=[jax.ShapeDtypeStruct((N, H * LANE), BF16),
                   jax.ShapeDtypeStruct((N, H * LANE), BF16),
                   jax.ShapeDtypeStruct((N, H * MLA_V), BF16)],
        grid=(N // tm,),
        in_specs=[pl.BlockSpec((tm, D), lambda i: (i, 0)),
                  pl.BlockSpec((D, nwin), lambda i: (0, 0)),
                  pl.BlockSpec((1, r0), lambda i: (0, 0)),
                  pl.BlockSpec((1, MLA_KV_RANK), lambda i: (0, 0)),
                  pl.BlockSpec((r0, H * LANE), lambda i: (0, 0)),
                  pl.BlockSpec((r0, H * LANE), lambda i: (0, 0)),
                  pl.BlockSpec((MLA_KV_RANK, H * LANE), lambda i: (0, 0)),
                  pl.BlockSpec((MLA_KV_RANK, H * MLA_V), lambda i: (0, 0)),
                  pl.BlockSpec((tm, LANE), lambda i: (i % (S // tm), 0)),
                  pl.BlockSpec((tm, LANE), lambda i: (i % (S // tm), 0))],
        out_specs=[pl.BlockSpec((tm, H * LANE), lambda i: (i, 0)),
                   pl.BlockSpec((tm, H * LANE), lambda i: (i, 0)),
                   pl.BlockSpec((tm, H * MLA_V), lambda i: (i, 0))],
        compiler_params=_cparams("parallel"),
        name="mla_proj",
    )(x, win, q_norm.reshape(1, r0), kv_norm.reshape(1, MLA_KV_RANK), wq, wqs, wk, wv, cos_t, sin_t)

    tq, tk = ATTN_TQ, ATTN_TK
    o = pl.pallas_call(
        functools.partial(_mla_attn_kernel, tq=tq, tk=tk, items=_causal_item_list(S // tq, tq, tk)),
        out_shape=jax.ShapeDtypeStruct((B, S, H * MLA_V), BF16),
        grid=(B, H // 2),
        in_specs=[pl.BlockSpec((1, S, 2 * LANE), lambda b, h: (b, 0, h)),
                  pl.BlockSpec((1, S, 2 * LANE), lambda b, h: (b, 0, h)),
                  pl.BlockSpec((1, S, LANE), lambda b, h: (b, 0, h))],
        out_specs=pl.BlockSpec((1, S, LANE), lambda b, h: (b, 0, h)),
        scratch_shapes=_flat_scratch(2, tq, tk),
        compiler_params=_cparams("parallel", "parallel"),
        name="mla_attn",
    )(q.reshape(B, S, H * LANE), k.reshape(B, S, H * LANE), v.reshape(B, S, H * MLA_V))
    return _outproj_ln(o.reshape(N, H * MLA_V), w_o, x, ln_g, ln_b, name="mla_out_ln")


def _diff_attn_kernel(slope_ref, lam_ref, sub_ref, q_ref, k_ref, v_ref, o_ref, *scratch, tq, tk, items, lam_init):
    slope = slope_ref[pl.program_id(1)] * LOG2E
    lv = lam_ref[...]
    lam = (jnp.exp(jnp.sum(lv[0:1] * lv[1:2], axis=-1, keepdims=True))
           - jnp.exp(jnp.sum(lv[2:3] * lv[3:4], axis=-1, keepdims=True)) + lam_init)
    lane = lax.broadcasted_iota(jnp.int32, (tq, LANE), 1)
    colf = lax.broadcasted_iota(jnp.int32, (1, tk), 1).astype(F32)

    def rows(i, t):
        return pl.ds(i * t, t)

    def q_at(i, first_map):
        qf = q_ref[0, rows(i, tq), :].astype(F32) * (DIFF_HD ** -0.5 * LOG2E)
        return jnp.where((lane < DIFF_HD) == first_map, qf, 0.0).astype(BF16)

    def k_at(c):
        return k_ref[0, rows(c, tk), :]

    def v_at(c):
        return v_ref[0, rows(c, tk), :]

    def bias_at(c):
        return slope * (colf + float(c * tk))

    def finalize(i, res):
        (l0, acc0), (l1, acc1) = res
        a = _rms_rows(acc0 / l0 - lam * (acc1 / l1), sub_ref[...], 1e-5) * (1.0 - lam_init)
        o_ref[0, rows(i, tq), :] = a.astype(o_ref.dtype)

    streams = [(functools.partial(q_at, first_map=fm), k_at, v_at, bias_at) for fm in (True, False)]
    _flash_static(items, streams, finalize, scratch, tq, tk)


def _diff_mixer(x, w_in, lam_q1, lam_k1, lam_q2, lam_k2, subln, w_o, layer_idx, ln_g, ln_b, B, S):
    N, D = x.shape
    H, d = DIFF_HEADS, DIFF_HD
    nq = H * 2 * d
    (qkv,) = _linear(x, w_in.astype(BF16), [3 * nq], [BF16], name="diff_proj")
    qkv = qkv.reshape(B, S, 3 * nq)
    lam_init = 0.8 - 0.6 * math.exp(-0.3 * layer_idx)
    lamv = jnp.zeros((8, LANE), F32).at[:4, :d].set(jnp.stack([lam_q1, lam_k1, lam_q2, lam_k2]).astype(F32))
    tq, tk = ATTN_TQ, ATTN_TK
    o = pl.pallas_call(
        functools.partial(_diff_attn_kernel, tq=tq, tk=tk, items=_causal_item_list(S // tq, tq, tk), lam_init=lam_init),
        out_shape=jax.ShapeDtypeStruct((B, S, nq), BF16),
        grid=(B, H),
        in_specs=[pl.BlockSpec(memory_space=pltpu.SMEM),
                  pl.BlockSpec((8, LANE), lambda b, h: (0, 0)),
                  pl.BlockSpec((1, 2 * d), lambda b, h: (0, 0)),
                  pl.BlockSpec((1, S, LANE), lambda b, h: (b, 0, h)),
                  pl.BlockSpec((1, S, LANE), lambda b, h: (b, 0, H + h)),
                  pl.BlockSpec((1, S, LANE), lambda b, h: (b, 0, 2 * H + h))],
        out_specs=pl.BlockSpec((1, S, LANE), lambda b, h: (b, 0, h)),
        scratch_shapes=_flat_scratch(2, tq, tk),
        compiler_params=_cparams("parallel", "parallel"),
        name="diff_attn",
    )(jnp.asarray(_alibi(H)), lamv, subln.reshape(1, 2 * d).astype(F32), qkv, qkv, qkv)
    return _outproj_ln(o.reshape(N, nq), w_o, x, ln_g, ln_b, name="diff_out_ln")


SEL_LANE0 = NSA_HD
SEL_OFF = 1e30
ALIBI_LANE0 = 96
POS_SPLIT = 64


def _nsa_compress_kernel(ak_ref, av_ref, plo_ref, phi_ref, wklo_ref, wkhi_ref, wvlo_ref, wvhi_ref, kc_out, vc_out):
    def one(a_ref, wlo_ref, whi_ref, out):
        a = a_ref[0]
        lo = jnp.dot((a + plo_ref[...]).astype(BF16), wlo_ref[...], preferred_element_type=F32)
        hi = jnp.dot((a + phi_ref[...]).astype(BF16), whi_ref[...], preferred_element_type=F32)
        out[0] = (lo + pltpu.roll(hi, hi.shape[0] - 1, 0)).astype(out.dtype)

    one(ak_ref, wklo_ref, wkhi_ref, kc_out)
    one(av_ref, wvlo_ref, wvhi_ref, vc_out)


def _pack_heads(o, tq):
    lane = lax.broadcasted_iota(jnp.int32, (tq, LANE), 1)
    p01 = jnp.where(lane < NSA_HD, o[0:tq], o[tq:2 * tq])
    p23 = jnp.where(lane < NSA_HD, o[2 * tq:3 * tq], o[3 * tq:4 * tq])
    return jnp.concatenate([p01, p23], axis=1)


def _nsa_attn_kernel(q_ref, qc_ref, gl_ref, kc_ref, vc_ref, ks_ref, vs_ref, kw_ref, vw_ref, ktab_ref, ctab_ref,
                     c2s_ref, rep_ref, o_ref, selb_ref, ocw_ref, gs_ref, *scratch, tq, tk, tqs, tks, n_slc, items):
    hpg = NSA_HEADS // NSA_GROUPS

    def head_q(rows, j):
        return q_ref[0, rows, j * LANE:(j + 1) * LANE] + jnp.broadcast_to(qc_ref[0, j:j + 1, :],
                                                                         (rows.size, LANE)).astype(BF16)

    def tile(i, carry):
        _nsa_tile(i, head_q, gl_ref, kc_ref, vc_ref, kw_ref, vw_ref, ktab_ref, ctab_ref, c2s_ref, rep_ref,
                  selb_ref, ocw_ref, gs_ref, tq=tq, tk=tk, n_slc=n_slc)
        return carry

    lax.fori_loop(0, q_ref.shape[1] // tq, tile, 0, unroll=2)

    def rows_s(i, t):
        return pl.ds(i * t, t)

    def q_at(i, pair):
        r = rows_s(i, tqs)
        return jnp.concatenate([head_q(r, j) + selb_ref[r, :] for j in (2 * pair, 2 * pair + 1)], axis=0)

    def ks_at(c):
        return ks_ref[0, rows_s(c, tks), :] + ktab_ref[rows_s(c, tks), :]

    def vs_at(c):
        return vs_ref[0, rows_s(c, tks), :]

    def finalize(i, res):
        r = rows_s(i, tqs)
        o_s = jnp.concatenate([acc / l for l, acc in res], axis=0)
        o_ref[0, r, :] = (ocw_ref[r, :] + gs_ref[r, :] * _pack_heads(o_s, tqs)).astype(o_ref.dtype)

    streams = [(functools.partial(q_at, pair=pr), ks_at, vs_at, None) for pr in range(hpg // 2)]
    _flash_static(items, streams, finalize, scratch, tqs, tks)


def _nsa_tile(i, head_q, gl_ref, kc_ref, vc_ref, kw_ref, vw_ref, ktab_ref, ctab_ref, c2s_ref, rep_ref,
              selb_ref, ocw_ref, gs_ref, *, tq, tk, n_slc):
    t0 = i * tq
    rows = pl.ds(pl.multiple_of(t0, tq), tq)
    hpg = NSA_HEADS // NSA_GROUPS
    M = hpg * tq
    q = jnp.concatenate([head_q(rows, j) for j in range(hpg)], axis=0)
    trow1 = t0 + lax.broadcasted_iota(jnp.int32, (tq, 1), 0)
    trow = jnp.concatenate([trow1] * hpg, axis=0)
    lane_m = lax.broadcasted_iota(jnp.int32, (M, LANE), 1)

    sc = lax.dot_general(q, kc_ref[0] + ctab_ref[...], (((1,), (1,)), ((), ())), preferred_element_type=F32)
    sc = jnp.where(trow >= lane_m * NSA_CMP_STRIDE + (NSA_CMP_LEN - 1), sc, NEG_INF)
    e = jnp.exp2(sc - jnp.max(sc, axis=-1, keepdims=True))
    p_c = jnp.where(trow >= NSA_CMP_LEN - 1, e / jnp.sum(e, axis=-1, keepdims=True), 0.0)
    o_c = jnp.dot(p_c.astype(BF16), vc_ref[0], preferred_element_type=F32)
    psum = p_c[0:tq]
    for j in range(1, hpg):
        psum = psum + p_c[j * tq:(j + 1) * tq]
    ph, plw = _split_bf16(psum)
    imp = (jnp.dot(ph, c2s_ref[...], preferred_element_type=F32)
           + jnp.dot(plw, c2s_ref[...], preferred_element_type=F32))

    lane = lax.broadcasted_iota(jnp.int32, (tq, LANE), 1)
    sidx = lane - SEL_LANE0
    valid = (sidx >= 0) & (sidx < n_slc)
    forced = (sidx == 0) | (sidx == trow1 // NSA_SLC_LEN)
    future = sidx * NSA_SLC_LEN > trow1
    score = imp + jnp.where(forced, NSA_FORCE, 0.0) - jnp.where(future, 2.0 * NSA_FORCE, 0.0)
    score = jnp.where(valid, score, -jnp.inf)
    n_rows = ALIBI_LANE0 - SEL_LANE0
    st = score.T[SEL_LANE0:ALIBI_LANE0]
    blk = lax.broadcasted_iota(jnp.int32, (n_rows, tq), 0)
    beaten_by = jnp.zeros((n_rows, tq), jnp.int32)
    for other in range(n_slc):
        row = st[other:other + 1, :]
        beaten_by = beaten_by + jnp.where((row > st) | ((row == st) & (blk > other)), 1, 0)
    sel = beaten_by < min(NSA_TOP_N, n_slc)
    off = jnp.where(sel | (blk >= n_slc), 0.0, -SEL_OFF)
    selbias = jnp.concatenate([jnp.zeros((SEL_LANE0, tq), F32), off, jnp.zeros((LANE - ALIBI_LANE0, tq), F32)],
                              axis=0).T.astype(BF16)

    selb_ref[rows, :] = selbias

    c_d = t0 // tk
    d_diag = jnp.concatenate([_rel_pos(tq, tk)] * hpg, axis=0) + (c_d * tk - t0)

    def chunk(c):
        return pl.ds(pl.multiple_of(c * tk, tk), tk)

    cw = [jnp.maximum(c_d - 2, 0), jnp.maximum(c_d - 1, 0), c_d]
    vis_w = [(d_diag - 2 * tk > -NSA_WINDOW) & (c_d >= 2), jnp.broadcast_to(c_d >= 1, d_diag.shape), d_diag <= 0]
    _, l_w, acc_w = _window_attention(q, [kw_ref[0, chunk(c), :] + ktab_ref[chunk(c), :] for c in cw],
                                      [vw_ref[0, chunk(c), :] for c in cw], vis_w)
    o_w = acc_w / l_w

    gh, glw = _split_bf16(jax.nn.sigmoid(gl_ref[0, rows, :]))
    gr = jnp.dot(gh, rep_ref[0], preferred_element_type=F32) + jnp.dot(glw, rep_ref[0], preferred_element_type=F32)
    w = hpg * NSA_HD
    ocw_ref[rows, :] = gr[:, 0:w] * _pack_heads(o_c, tq) + gr[:, 2 * w:3 * w] * _pack_heads(o_w, tq)
    gs_ref[rows, :] = gr[:, w:2 * w]


def _nsa_mixer(x, w_in, w_phi_k, w_phi_v, cmp_pos, w_o, ln_g, ln_b, B, S):
    N, D = x.shape
    H, G, d = NSA_HEADS, NSA_GROUPS, NSA_HD
    hpg = H // G
    L, st = NSA_CMP_LEN, NSA_CMP_STRIDE
    n_slc = S // NSA_SLC_LEN
    assert S % 256 == 0 and n_slc <= ALIBI_LANE0 - SEL_LANE0 and L == 2 * st
    cuts = [H * d + i * G * d for i in range(7)]
    wq, wkc, wvc, wks, wvs, wkw, wvw, wgl = jnp.split(w_in, cuts, axis=1)
    z = lambda *s: jnp.zeros(s, F32)
    pad_heads = lambda w, n: jnp.concatenate([w.reshape(D, n, d), z(D, n, LANE - d)], axis=-1).reshape(D, n * LANE)
    dup_heads = lambda w, n: jnp.concatenate([w.reshape(D, n, d)] * 2, axis=-1).reshape(D, n * LANE)
    wgl_p = jnp.concatenate([wgl, z(D, LANE - wgl.shape[1])], axis=1)
    wcat = jnp.concatenate([pad_heads(wq * (d ** -0.5 * LOG2E), H), wkc, wvc, pad_heads(wks, G), dup_heads(wvs, G),
                            pad_heads(wkw, G), dup_heads(wvw, G), wgl_p], axis=1).astype(BF16)
    q, kc, vc, ks, vs, kw, vw, gl = _linear(
        x, wcat, [H * LANE, G * d, G * d, G * LANE, G * LANE, G * LANE, G * LANE, LANE],
        [BF16, F32, F32, BF16, BF16, BF16, BF16, F32], name="nsa_proj")

    nrow = S // st
    eye = jnp.eye(G, dtype=F32)

    def phi_w(w_phi, half, dup):
        w = w_phi.reshape(L, d, d)[half * st:(half + 1) * st]
        wd = jnp.concatenate([w, w if dup else jnp.zeros_like(w)], axis=-1)
        return jnp.einsum('ldc,gh->lgdhc', wd, eye).reshape(st * G * d, G * LANE).astype(BF16)

    pos = lambda half: jnp.broadcast_to(cmp_pos[half * st:(half + 1) * st, None, :], (st, G, d)).reshape(1, st * G * d)
    wide = st * G * d
    cspec = pl.BlockSpec((1, nrow, wide), lambda b: (b, 0, 0))
    wspec = pl.BlockSpec((wide, G * LANE), lambda b: (0, 0))
    pspec = pl.BlockSpec((1, wide), lambda b: (0, 0))
    ospec = pl.BlockSpec((1, nrow, G * LANE), lambda b: (b, 0, 0))
    kcmp, vcmp = pl.pallas_call(
        _nsa_compress_kernel,
        out_shape=[jax.ShapeDtypeStruct((B, nrow, G * LANE), BF16)] * 2,
        grid=(B,),
        in_specs=[cspec, cspec, pspec, pspec, wspec, wspec, wspec, wspec],
        out_specs=[ospec, ospec],
        compiler_params=_cparams("parallel"),
        name="nsa_compress",
    )(kc.reshape(B, nrow, wide), vc.reshape(B, nrow, wide), pos(0), pos(1),
      phi_w(w_phi_k, 0, False), phi_w(w_phi_k, 1, False), phi_w(w_phi_v, 0, True), phi_w(w_phi_v, 1, True))

    def pos_lanes(tab, pos):
        tab[:, ALIBI_LANE0:ALIBI_LANE0 + 3] = (pos // POS_SPLIT)[:, None]
        tab[:, ALIBI_LANE0 + 3:ALIBI_LANE0 + 6] = (pos % POS_SPLIT)[:, None]
        return tab

    ktab = np.zeros((S, LANE), np.float32)
    ktab[np.arange(S), SEL_LANE0 + np.arange(S) // NSA_SLC_LEN] = 1.0
    ktab = pos_lanes(ktab, np.arange(S))
    ctab = pos_lanes(np.zeros((nrow, LANE), np.float32), np.arange(nrow) * st + (L - 1))
    s2 = jnp.asarray(_alibi(H) * LOG2E, F32)
    parts = []
    for _ in range(3):
        part = s2.astype(BF16).astype(F32)
        parts.append(part)
        s2 = s2 - part
    qcoef = jnp.zeros((H, LANE), F32).at[:, ALIBI_LANE0:ALIBI_LANE0 + 6].set(
        jnp.stack([POS_SPLIT * p for p in parts] + parts, axis=1)).reshape(G, hpg, LANE)
    cmp_start = np.arange(nrow) * st
    slc_start = np.arange(n_slc) * NSA_SLC_LEN
    ov = (cmp_start[:, None] < slc_start[None, :] + NSA_SLC_LEN) & (cmp_start[:, None] + L > slc_start[None, :])
    ov[(S - L) // st + 1:] = False
    c2s = np.zeros((nrow, LANE), np.float32)
    c2s[:, SEL_LANE0:SEL_LANE0 + n_slc] = ov
    rep = np.zeros((G, LANE, 3 * hpg * d), np.float32)
    for g in range(G):
        for j in range(hpg):
            for br in range(3):
                rep[g, (g * hpg + j) * 3 + br, br * hpg * d + j * d:br * hpg * d + (j + 1) * d] = 1.0

    tq, tk = 256, 256
    tqs, tks = ATTN_TQ, ATTN_TK
    assert NSA_WINDOW == 2 * tk and tk % tq == 0
    kvspec = pl.BlockSpec((1, S, LANE), lambda b, g: (b, 0, g))
    cmpspec = pl.BlockSpec((1, nrow, LANE), lambda b, g: (b, 0, g))
    o = pl.pallas_call(
        functools.partial(_nsa_attn_kernel, tq=tq, tk=tk, tqs=tqs, tks=tks, n_slc=n_slc,
                          items=_causal_item_list(S // tqs, tqs, tks)),
        out_shape=jax.ShapeDtypeStruct((B, S, H * d), BF16),
        grid=(B, G),
        in_specs=[pl.BlockSpec((1, S, hpg * LANE), lambda b, g: (b, 0, g)),
                  pl.BlockSpec((1, hpg, LANE), lambda b, g: (g, 0, 0)),
                  pl.BlockSpec((1, S, LANE), lambda b, g: (b, 0, 0)),
                  cmpspec, cmpspec, kvspec, kvspec, kvspec, kvspec,
                  pl.BlockSpec((S, LANE), lambda b, g: (0, 0)),
                  pl.BlockSpec((nrow, LANE), lambda b, g: (0, 0)),
                  pl.BlockSpec((nrow, LANE), lambda b, g: (0, 0)),
                  pl.BlockSpec((1, LANE, 3 * hpg * d), lambda b, g: (g, 0, 0))],
        out_specs=pl.BlockSpec((1, S, hpg * d), lambda b, g: (b, 0, g)),
        scratch_shapes=[pltpu.VMEM((S, LANE), BF16), pltpu.VMEM((S, hpg * d), F32), pltpu.VMEM((S, hpg * d), F32)]
        + _flat_scratch(hpg // 2, 2 * tqs, tks),
        compiler_params=_cparams("parallel", "parallel"),
        name="nsa_attn",
    )(q.reshape(B, S, H * LANE), qcoef, gl.reshape(B, S, LANE), kcmp, vcmp,
      ks.reshape(B, S, G * LANE), vs.reshape(B, S, G * LANE), kw.reshape(B, S, G * LANE), vw.reshape(B, S, G * LANE),
      jnp.asarray(ktab, BF16), jnp.asarray(ctab, BF16), jnp.asarray(c2s, BF16), jnp.asarray(rep, BF16))
    return _outproj_ln(o.reshape(N, H * d), w_o, x, ln_g, ln_b, name="nsa_out_ln")


def _dil_attn_kernel(slope_ref, q_ref, k_ref, v_ref, o_ref, lse_ref, *, t, dil):
    i = pl.program_id(2)
    d = DIL_HD
    rel = _rel_pos(t, t)
    relf = rel.astype(F32)
    rel2 = jnp.concatenate([rel, rel], axis=0)
    lane = lax.broadcasted_iota(jnp.int32, (t, LANE), 1)
    scale = d ** -0.5 * LOG2E

    def chunk(c):
        return pl.ds(pl.multiple_of(c * t, t), t)

    cs = [jnp.maximum(i - 1, 0), i]
    viss = [(rel2 >= 0) & (i >= 1), rel2 <= 0]
    lse_t = jnp.zeros((t, LANE), F32)
    for hp in range(DIL_HEADS // 2):
        sl = slice(hp * LANE, (hp + 1) * LANE)
        qf = q_ref[0, :, sl].astype(F32) * scale
        q2 = jnp.concatenate([jnp.where(lane < d, qf, 0.0), jnp.where(lane < d, 0.0, qf)], axis=0).astype(BF16)
        sa, sb = slope_ref[2 * hp] * (dil * LOG2E), slope_ref[2 * hp + 1] * (dil * LOG2E)
        bias = jnp.concatenate([sa * relf, sb * relf], axis=0)
        shift = jnp.concatenate([jnp.full((t, 1), sa * t, F32), jnp.full((t, 1), sb * t, F32)], axis=0)
        m, l, acc = _window_attention(q2, [k_ref[0, chunk(c), sl] for c in cs], [v_ref[0, chunk(c), sl] for c in cs],
                                      viss, [bias - shift, bias])
        o = acc / l
        o_ref[0, :, hp * LANE:(hp + 1) * LANE] = jnp.where(lane < d, o[:t], o[t:]).astype(o_ref.dtype)
        lse = m + jnp.log2(l)
        lse_t = jnp.where(lane == 2 * hp, lse[:t], lse_t)
        lse_t = jnp.where(lane == 2 * hp + 1, lse[t:], lse_t)
    lse_ref[0] = lse_t


def _dil_proj_kernel(x_ref, w_ref, *refs, tm):
    outs, scr = refs[:-1], refs[-1]
    xb = x_ref[...].astype(BF16)
    width = scr.shape[0] * LANE
    for p, (o_ref, (_, dil)) in enumerate(zip(outs, DIL_PATTERNS)):
        for c0 in range(0, width, 512):
            y = jnp.dot(xb, w_ref[:, p * width + c0:p * width + c0 + 512], preferred_element_type=F32)
            if dil == 1:
                o_ref[:, c0:c0 + 512] = y.astype(o_ref.dtype)
            else:
                for j in range(512 // LANE):
                    scr[c0 // LANE + j] = y[:, j * LANE:(j + 1) * LANE]
        for c in range(dil if dil > 1 else 0):
            for j in range(width // LANE):
                o_ref[:, c * width + j * LANE:c * width + (j + 1) * LANE] = (
                    scr[j, pl.ds(c, tm // dil, stride=dil), :].astype(o_ref.dtype))


def _dil_merge_ln_kernel(o0_ref, o1_ref, o2_ref, l0_ref, l1_ref, l2_ref, rep_ref, w_ref, x_ref, g_ref, b_ref, out_ref,
                         scr_o, scr_l):
    tm = x_ref.shape[0]

    def token_order(ref, dil, scr):
        if dil == 1:
            return ref[...].astype(F32)
        w = ref.shape[1] // dil
        for c in range(dil):
            for j in range(w // LANE):
                scr[j, pl.ds(c, tm // dil, stride=dil), :] = (
                    ref[:, c * w + j * LANE:c * w + (j + 1) * LANE].astype(F32))
        return jnp.concatenate([scr[j] for j in range(w // LANE)], axis=1)

    ls = [token_order(l_ref, dil, scr_l) for l_ref, (_, dil) in zip((l0_ref, l1_ref, l2_ref), DIL_PATTERNS)]
    mx = jnp.maximum(jnp.maximum(ls[0], ls[1]), ls[2])
    es = [jnp.exp2(v - mx) for v in ls]
    tot = es[0] + es[1] + es[2]
    o = None
    for e, o_ref, (_, dil) in zip(es, (o0_ref, o1_ref, o2_ref), DIL_PATTERNS):
        wh, wl = _split_bf16(e / tot)
        wrep = jnp.dot(wh, rep_ref[...], preferred_element_type=F32) + jnp.dot(wl, rep_ref[...], preferred_element_type=F32)
        term = wrep * token_order(o_ref, dil, scr_o)
        o = term if o is None else o + term
    y = jnp.dot(o.astype(BF16), w_ref[...], preferred_element_type=F32)
    out_ref[...] = _layer_norm_rows(DN_ALPHA * x_ref[...] + y, g_ref[...], b_ref[...])


def _dil_mixer(x, w_in, w_o, ln_g, ln_b, B, S):
    N, D = x.shape
    H, d = DIL_HEADS, DIL_HD
    n_pat = len(DIL_PATTERNS)
    hd = H * d
    width = 3 * hd
    ncol = n_pat * width
    tm = 512
    qkvs = pl.pallas_call(
        functools.partial(_dil_proj_kernel, tm=tm),
        out_shape=[jax.ShapeDtypeStruct((N // dil, dil * width), BF16) for _, dil in DIL_PATTERNS],
        grid=(N // tm,),
        in_specs=[pl.BlockSpec((tm, D), lambda i: (i, 0)), pl.BlockSpec((D, ncol), lambda i: (0, 0))],
        out_specs=[pl.BlockSpec((tm // dil, dil * width), lambda i: (i, 0)) for _, dil in DIL_PATTERNS],
        scratch_shapes=[pltpu.VMEM((width // LANE, tm, LANE), F32)],
        compiler_params=_cparams("parallel"),
        name="dil_proj",
    )(x, w_in.astype(BF16))
    tq = 128
    slopes = jnp.asarray(_alibi(H))
    outs, lses = [], []
    for p, (win, dil) in enumerate(DIL_PATTERNS):
        ls = S // dil
        assert ls % tq == 0 and win == dil * tq
        view = qkvs[p].reshape(B, ls, dil * width)
        o, lse = pl.pallas_call(
            functools.partial(_dil_attn_kernel, t=tq, dil=dil),
            out_shape=[jax.ShapeDtypeStruct((B, ls, dil * hd), BF16),
                       jax.ShapeDtypeStruct((B, ls, dil * LANE), F32)],
            grid=(B, dil, ls // tq),
            in_specs=[pl.BlockSpec(memory_space=pltpu.SMEM),
                      pl.BlockSpec((1, tq, hd), lambda b, r, i: (b, i, 3 * r)),
                      pl.BlockSpec((1, ls, hd), lambda b, r, i: (b, 0, 3 * r + 1)),
                      pl.BlockSpec((1, ls, hd), lambda b, r, i: (b, 0, 3 * r + 2))],
            out_specs=[pl.BlockSpec((1, tq, hd), lambda b, r, i: (b, i, r)),
                       pl.BlockSpec((1, tq, LANE), lambda b, r, i: (b, i, r))],
            compiler_params=_cparams("parallel", "parallel", "arbitrary"),
            name=f"dil_attn_{p}",
        )(slopes, view, view, view)
        outs.append(o.reshape(N // dil, dil * hd))
        lses.append(lse.reshape(N // dil, dil * LANE))

    rep = np.zeros((LANE, hd), np.float32)
    for h in range(H):
        rep[h, h * d:(h + 1) * d] = 1.0
    row = lambda n: pl.BlockSpec((tm, n), lambda i: (i, 0))
    rowp = lambda n: [pl.BlockSpec((tm // dil, dil * n), lambda i: (i, 0)) for _, dil in DIL_PATTERNS]
    full = lambda a, b: pl.BlockSpec((a, b), lambda i: (0, 0))
    return pl.pallas_call(
        _dil_merge_ln_kernel,
        out_shape=jax.ShapeDtypeStruct((N, D), F32),
        grid=(N // tm,),
        in_specs=rowp(hd) + rowp(LANE) + [full(LANE, hd), full(hd, D), row(D), full(1, D), full(1, D)],
        out_specs=row(D),
        scratch_shapes=[pltpu.VMEM((hd // LANE, tm, LANE), F32), pltpu.VMEM((1, tm, LANE), F32)],
        compiler_params=_cparams("parallel"),
        name="dil_merge_out_ln",
    )(*outs, *lses, jnp.asarray(rep, BF16), w_o.astype(BF16), x, ln_g.reshape(1, D), ln_b.reshape(1, D))


def _split_bf16(a):
    hi = a.astype(BF16)
    return hi, (a - hi.astype(F32)).astype(BF16)


ROUTER_ROWS = 32
MOE_PAIRS = MOE_EPG * (MOE_EPG - 1) // 2
MOE_BUCKETS = MOE_GROUPS * MOE_PAIRS
PAIR_LO = [a for a in range(MOE_EPG) for b in range(a + 1, MOE_EPG)]
PAIR_HI = [b for a in range(MOE_EPG) for b in range(a + 1, MOE_EPG)]


def _router_kernel(x_ref, wh_ref, wl_ref, b_ref, tri_ref, meta_ref, bkt_ref, rank_ref, cnt_ref, run_ref):
    @pl.when(pl.program_id(0) == 0)
    def _():
        run_ref[...] = jnp.zeros_like(run_ref)

    xh, xl = _split_bf16(x_ref[...])
    wh, wl = wh_ref[...], wl_ref[...]
    nt = (((1,), (1,)), ((), ()))
    logits = (lax.dot_general(wh, xh, nt, preferred_element_type=F32)
              + lax.dot_general(wh, xl, nt, preferred_element_type=F32)
              + lax.dot_general(wl, xh, nt, preferred_element_type=F32)) + b_ref[...]
    nr, tm = logits.shape
    row = lax.broadcasted_iota(jnp.int32, (nr, tm), 0)
    big = jnp.int32(nr)
    lg = jnp.where(row < MOE_GROUPS, logits, -jnp.inf)
    mg = jnp.max(lg, axis=0, keepdims=True)
    sg = jnp.sum(jnp.exp(lg - mg), axis=0, keepdims=True)
    pg_top = 1.0 / sg
    g_top = jnp.min(jnp.where(lg == mg, row, big), axis=0, keepdims=True)
    e_lo = MOE_GROUPS + MOE_EPG * g_top
    in_grp = (row >= e_lo) & (row < e_lo + MOE_EPG)
    le = jnp.where(in_grp, logits, -jnp.inf)
    me = jnp.max(le, axis=0, keepdims=True)
    ee = jnp.exp(le - me)
    se = jnp.sum(ee, axis=0, keepdims=True)
    pe = jnp.where(in_grp, ee / se, -1.0)
    v1 = jnp.max(pe, axis=0, keepdims=True)
    i1 = jnp.min(jnp.where(pe == v1, row, big), axis=0, keepdims=True)
    pe2 = jnp.where(row == i1, -1.0, pe)
    v2 = jnp.max(pe2, axis=0, keepdims=True)
    i2 = jnp.min(jnp.where(pe2 == v2, row, big), axis=0, keepdims=True)
    tot = v1 + v2
    a1, a2 = i1 - e_lo, i2 - e_lo
    lo, hi = jnp.minimum(a1, a2), jnp.maximum(a1, a2)
    pair = lo * (MOE_EPG - 1) - (lo * (lo - 1)) // 2 + hi - lo - 1
    bucket = g_top * MOE_PAIRS + pair
    g_first, g_second = (v1 / tot) * pg_top, (v2 / tot) * pg_top
    onehot = row == bucket
    prefix = jnp.dot(jnp.where(onehot, 1.0, 0.0).astype(BF16), tri_ref[...], preferred_element_type=F32)
    rank = jnp.sum(jnp.where(onehot, prefix + run_ref[:, 0:1] - 1.0, 0.0), axis=0, keepdims=True)
    gates_t = (jnp.where(row == 0, jnp.where(a1 < a2, g_first, g_second), 0.0)
               + jnp.where(row == 1, jnp.where(a1 < a2, g_second, g_first), 0.0))
    meta_ref[...] = jnp.concatenate([gates_t, jnp.zeros((LANE - nr, tm), F32)], axis=0).T
    bkt_ref[0] = bucket
    rank_ref[0] = rank.astype(jnp.int32)
    run_ref[...] += jnp.broadcast_to(prefix[:, tm - 1:tm], run_ref.shape)
    cnt_ref[...] = run_ref[...]


def _moe_dispatch_kernel(starts_ref, ends_ref, bkt_ref, rank_ref, x_ref, meta_ref, xs_hbm, buf, sem):
    tm, d = x_ref.shape

    @pl.when(pl.program_id(0) == 0)
    def _():
        buf[...] = jnp.zeros_like(buf)

        def fill_tile(t, carry):
            fill = pltpu.make_async_copy(buf, xs_hbm.at[pl.ds(pl.multiple_of(t * tm, tm), tm)], sem)
            fill.start()
            fill.wait()
            return carry

        for b in range(MOE_BUCKETS):
            @pl.when(ends_ref[b] > starts_ref[b])
            def _(b=b):
                fill_tile(ends_ref[b] // tm - 1, 0)
        lax.fori_loop(ends_ref[MOE_BUCKETS - 1] // tm, xs_hbm.shape[0] // tm, fill_tile, 0)

    buf[:, :d] = x_ref[...]
    buf[:, d:] = meta_ref[...]

    def issue(r, carry):
        slot = starts_ref[bkt_ref[0, 0, r]] + rank_ref[0, 0, r]
        pltpu.make_async_copy(buf.at[pl.ds(r, 1)], xs_hbm.at[pl.ds(slot, 1)], sem).start()
        return carry

    lax.fori_loop(0, tm, issue, 0, unroll=8)
    pltpu.make_async_copy(buf, xs_hbm.at[pl.ds(0, tm)], sem).wait()


def _moe_pair_kernel(ea_ref, eb_ref, used_ref, xs_ref, w1a_ref, w3a_ref, w2a_ref, w1b_ref, w3b_ref, w2b_ref, g_ref,
                     b_ref, ys_ref):
    del ea_ref, eb_ref
    d = ys_ref.shape[1]

    @pl.when(pl.program_id(0) < used_ref[0])
    def _():
        x = xs_ref[:, :d]
        gates = xs_ref[:, d:]
        lane = lax.broadcasted_iota(jnp.int32, gates.shape, 1)
        xb = x.astype(BF16)
        y = None
        for e, (w1_ref, w3_ref, w2_ref) in enumerate(((w1a_ref, w3a_ref, w2a_ref), (w1b_ref, w3b_ref, w2b_ref))):
            h1 = jnp.dot(xb, w1_ref[0], preferred_element_type=F32)
            h3 = jnp.dot(xb, w3_ref[0], preferred_element_type=F32)
            hid = (h1 * jax.nn.sigmoid(h1) * h3).astype(BF16)
            ge = jnp.sum(jnp.where(lane == e, gates, 0.0), axis=-1, keepdims=True)
            term = ge * jnp.dot(hid, w2_ref[0], preferred_element_type=F32)
            y = term if y is None else y + term
        ys_ref[...] = _layer_norm_rows(DN_ALPHA * x + y, g_ref[...], b_ref[...])

    @pl.when(pl.program_id(0) >= used_ref[0])
    def _():
        ys_ref[...] = jnp.zeros_like(ys_ref)


def _moe_collect_kernel(starts_ref, bkt_ref, rank_ref, ys_hbm, out_ref, sem):
    tm = out_ref.shape[0]

    def issue(r, carry):
        slot = starts_ref[bkt_ref[0, 0, r]] + rank_ref[0, 0, r]
        pltpu.make_async_copy(ys_hbm.at[pl.ds(slot, 1)], out_ref.at[pl.ds(r, 1)], sem).start()
        return carry

    lax.fori_loop(0, tm, issue, 0, unroll=8)
    pltpu.make_async_copy(ys_hbm.at[pl.ds(0, tm)], out_ref, sem).wait()


def _hier_moe_ln(x, wg, bg, we, be, w1, w3, w2, ln_g, ln_b):
    N, D = x.shape
    G, E, FF, NB = MOE_GROUPS, MOE_EPG, MOE_FF, MOE_BUCKETS
    NR = ROUTER_ROWS
    assert max(G + MOE_EXPERTS, NB) <= NR
    wr = jnp.concatenate([wg, jnp.moveaxis(we, 0, 1).reshape(D, MOE_EXPERTS),
                          jnp.zeros((D, NR - G - MOE_EXPERTS), F32)], axis=1).T
    br = jnp.concatenate([bg, be.reshape(-1), jnp.zeros((NR - G - MOE_EXPERTS,), F32)])
    wrh, wrl = _split_bf16(wr)
    tm = MOE_TM
    tri = jnp.asarray(np.triu(np.ones((tm, tm), np.float32)), BF16)
    meta, bkt, rank, cnt = pl.pallas_call(
        _router_kernel,
        out_shape=[jax.ShapeDtypeStruct((N, LANE), F32), jax.ShapeDtypeStruct((N // tm, 1, tm), jnp.int32),
                   jax.ShapeDtypeStruct((N // tm, 1, tm), jnp.int32), jax.ShapeDtypeStruct((NR, LANE), F32)],
        grid=(N // tm,),
        in_specs=[pl.BlockSpec((tm, D), lambda i: (i, 0)),
                  pl.BlockSpec((NR, D), lambda i: (0, 0)),
                  pl.BlockSpec((NR, D), lambda i: (0, 0)),
                  pl.BlockSpec((NR, tm), lambda i: (0, 0)),
                  pl.BlockSpec((tm, tm), lambda i: (0, 0))],
        out_specs=[pl.BlockSpec((tm, LANE), lambda i: (i, 0)),
                   pl.BlockSpec((1, 1, tm), lambda i: (i, 0, 0)),
                   pl.BlockSpec((1, 1, tm), lambda i: (i, 0, 0)),
                   pl.BlockSpec((NR, LANE), lambda i: (0, 0))],
        scratch_shapes=[pltpu.VMEM((NR, LANE), F32)],
        compiler_params=_cparams("arbitrary"),
        name="moe_router",
    )(x, wrh, wrl, jnp.broadcast_to(br[:, None], (NR, tm)), tri)

    counts = cnt[:NB, 0].astype(jnp.int32)
    padded = (counts + tm - 1) // tm * tm
    ends = jnp.cumsum(padded)
    starts = ends - padded
    n_pad = N + NB * tm
    n_tiles = n_pad // tm
    tile_row0 = jnp.arange(n_tiles, dtype=jnp.int32) * tm
    tile_bkt = jnp.minimum(jnp.sum((ends[None, :] <= tile_row0[:, None]).astype(jnp.int32), axis=1), NB - 1)
    tile_ea = (tile_bkt // MOE_PAIRS) * E + jnp.asarray(PAIR_LO, jnp.int32)[tile_bkt % MOE_PAIRS]
    tile_eb = (tile_bkt // MOE_PAIRS) * E + jnp.asarray(PAIR_HI, jnp.int32)[tile_bkt % MOE_PAIRS]

    xs = pl.pallas_call(
        _moe_dispatch_kernel,
        out_shape=jax.ShapeDtypeStruct((n_pad, D + LANE), F32),
        grid_spec=pltpu.PrefetchScalarGridSpec(
            num_scalar_prefetch=2,
            grid=(N // tm,),
            in_specs=[pl.BlockSpec((1, 1, tm), lambda i, st, en: (i, 0, 0), memory_space=pltpu.SMEM),
                      pl.BlockSpec((1, 1, tm), lambda i, st, en: (i, 0, 0), memory_space=pltpu.SMEM),
                      pl.BlockSpec((tm, D), lambda i, st, en: (i, 0)),
                      pl.BlockSpec((tm, LANE), lambda i, st, en: (i, 0))],
            out_specs=pl.BlockSpec(memory_space=pl.ANY),
            scratch_shapes=[pltpu.VMEM((tm, D + LANE), F32), pltpu.SemaphoreType.DMA]),
        compiler_params=_cparams("arbitrary"),
        name="moe_dispatch",
    )(starts, ends, bkt, rank, x, meta)

    wa = lambda a, b: pl.BlockSpec((1, a, b), lambda t, ea, eb, nu: (ea[t], 0, 0))
    wb = lambda a, b: pl.BlockSpec((1, a, b), lambda t, ea, eb, nu: (eb[t], 0, 0))
    w1b, w3b, w2b = w1.astype(BF16), w3.astype(BF16), w2.astype(BF16)
    ys = pl.pallas_call(
        _moe_pair_kernel,
        out_shape=jax.ShapeDtypeStruct((n_pad, D), F32),
        grid_spec=pltpu.PrefetchScalarGridSpec(
            num_scalar_prefetch=3,
            grid=(n_tiles,),
            in_specs=[pl.BlockSpec((tm, D + LANE), lambda t, ea, eb, nu: (t, 0)),
                      wa(D, FF), wa(D, FF), wa(FF, D), wb(D, FF), wb(D, FF), wb(FF, D),
                      pl.BlockSpec((1, D), lambda t, ea, eb, nu: (0, 0)),
                      pl.BlockSpec((1, D), lambda t, ea, eb, nu: (0, 0))],
            out_specs=pl.BlockSpec((tm, D), lambda t, ea, eb, nu: (t, 0))),
        compiler_params=_cparams("arbitrary"),
        name="moe_experts_ln",
    )(tile_ea, tile_eb, (ends[-1:] // tm).astype(jnp.int32), xs, w1b, w3b, w2b, w1b, w3b, w2b,
      ln_g.reshape(1, D), ln_b.reshape(1, D))

    return pl.pallas_call(
        _moe_collect_kernel,
        out_shape=jax.ShapeDtypeStruct((N, D), F32),
        grid_spec=pltpu.PrefetchScalarGridSpec(
            num_scalar_prefetch=1,
            grid=(N // tm,),
            in_specs=[pl.BlockSpec((1, 1, tm), lambda i, st: (i, 0, 0), memory_space=pltpu.SMEM),
                      pl.BlockSpec((1, 1, tm), lambda i, st: (i, 0, 0), memory_space=pltpu.SMEM),
                      pl.BlockSpec(memory_space=pl.ANY)],
            out_specs=pl.BlockSpec((tm, D), lambda i, st: (i, 0)),
            scratch_shapes=[pltpu.SemaphoreType.DMA]),
        compiler_params=_cparams("arbitrary"),
        name="moe_collect",
    )(starts, bkt, rank, ys)


def kernel(x, mla_w_in, mla_q_norm, mla_kv_norm, mla_w_qb, mla_w_kvb, mla_w_o, nsa_w_in, nsa_w_phi_k, nsa_w_phi_v, nsa_cmp_pos, nsa_w_o, diff_w_in, diff_lam_q1, diff_lam_k1, diff_lam_q2, diff_lam_k2, diff_subln, diff_w_o, dil_w_in, dil_w_o, ln1_g, ln1_b, ln2_g, ln2_b, moe_wg, moe_bg, moe_we, moe_be, moe_w1, moe_w3, moe_w2):
    B, S, D = x.shape
    h = x.reshape(B * S, D)
    for i in range(DEPTH):
        m, j = i % 4, i // 4
        if m == 0:
            h = _mla_mixer(h, mla_w_in[j], mla_q_norm[j], mla_kv_norm[j], mla_w_qb[j], mla_w_kvb[j], mla_w_o[j],
                           ln1_g[i], ln1_b[i], B, S)
        elif m == 1:
            h = _nsa_mixer(h, nsa_w_in[j], nsa_w_phi_k[j], nsa_w_phi_v[j], nsa_cmp_pos[j], nsa_w_o[j],
                           ln1_g[i], ln1_b[i], B, S)
        elif m == 2:
            h = _diff_mixer(h, diff_w_in[j], diff_lam_q1[j], diff_lam_k1[j], diff_lam_q2[j], diff_lam_k2[j],
                            diff_subln[j], diff_w_o[j], i, ln1_g[i], ln1_b[i], B, S)
        else:
            h = _dil_mixer(h, dil_w_in[j], dil_w_o[j], ln1_g[i], ln1_b[i], B, S)
        h = _hier_moe_ln(h, moe_wg[i], moe_bg[i], moe_we[i], moe_be[i], moe_w1[i], moe_w3[i], moe_w2[i],
                         ln2_g[i], ln2_b[i])
    return h.reshape(B, S, D)
```

```python
import functools
import math

import numpy as np
import jax
import jax.numpy as jnp
from jax import lax
from jax.experimental import pallas as pl
from jax.experimental.pallas import tpu as pltpu

F32 = jnp.float32
BF16 = jnp.bfloat16

DEPTH = 4
DN_ALPHA = (2.0 * DEPTH) ** 0.25
LN_EPS = 1e-5
NEG_INF = -1e30
LOG2E = math.log2(math.e)
LANE = 128
VMEM_LIMIT = 56 * 1024 * 1024
ATTN_TQ, ATTN_TK = 256, 512
MOE_TM = 512
DMA_ISSUE_UNROLL = 16

MLA_HEADS, MLA_Q_RANK, MLA_KV_RANK, MLA_NOPE, MLA_ROPE, MLA_V = 16, 384, 256, 64, 32, 64
ROPE_THETA = 10000.0
NSA_HEADS, NSA_GROUPS, NSA_HD = 16, 4, 64
NSA_CMP_LEN, NSA_CMP_STRIDE, NSA_SLC_LEN, NSA_TOP_N, NSA_WINDOW, NSA_FORCE = 32, 16, 64, 8, 512, 1e4
DIFF_HEADS, DIFF_HD = 8, 64
DIL_PATTERNS = ((128, 1), (512, 4), (2048, 16))
DIL_HEADS, DIL_HD = 8, 64
MOE_GROUPS, MOE_EPG, MOE_EXPERTS, MOE_FF = 4, 4, 16, 512


def _cparams(*sem):
    return pltpu.CompilerParams(dimension_semantics=sem, vmem_limit_bytes=VMEM_LIMIT)


def _alibi(n):
    return np.asarray(2.0 ** (-8.0 * np.arange(1, n + 1) / n), np.float32)


def _linear_kernel(x_ref, w_ref, *out_refs, splits, chunk):
    xb = x_ref[...].astype(BF16)
    col = 0
    for o_ref, n in zip(out_refs, splits):
        for c0 in range(0, n, chunk):
            cw = min(chunk, n - c0)
            o_ref[:, c0:c0 + cw] = jnp.dot(
                xb, w_ref[:, col + c0:col + c0 + cw], preferred_element_type=F32).astype(o_ref.dtype)
        col += n


def _linear(x, w, splits, dtypes, tm=512, name="linear"):
    M, K = x.shape
    ntot = sum(splits)
    assert w.shape == (K, ntot) and M % tm == 0 and all(n % LANE == 0 for n in splits)
    outs = pl.pallas_call(
        functools.partial(_linear_kernel, splits=tuple(splits), chunk=512),
        out_shape=[jax.ShapeDtypeStruct((M, n), d) for n, d in zip(splits, dtypes)],
        grid=(M // tm,),
        in_specs=[pl.BlockSpec((tm, K), lambda i: (i, 0)),
                  pl.BlockSpec((K, ntot), lambda i: (0, 0))],
        out_specs=[pl.BlockSpec((tm, n), lambda i: (i, 0)) for n in splits],
        compiler_params=_cparams("parallel"),
        name=name,
    )(x, w)
    return outs


def _layer_norm_rows(z, g, b):
    mu = jnp.mean(z, axis=-1, keepdims=True)
    zc = z - mu
    var = jnp.mean(zc * zc, axis=-1, keepdims=True)
    return zc * lax.rsqrt(var + LN_EPS) * g + b


def _outproj_ln_kernel(o_ref, w_ref, x_ref, g_ref, b_ref, out_ref):
    y = jnp.dot(o_ref[...].astype(BF16), w_ref[...], preferred_element_type=F32)
    out_ref[...] = _layer_norm_rows(DN_ALPHA * x_ref[...] + y, g_ref[...], b_ref[...])


def _outproj_ln(o, w_o, x, g, b, tm=512, name="outproj_ln"):
    M, K = o.shape
    D = x.shape[1]
    return pl.pallas_call(
        _outproj_ln_kernel,
        out_shape=jax.ShapeDtypeStruct((M, D), F32),
        grid=(M // tm,),
        in_specs=[pl.BlockSpec((tm, K), lambda i: (i, 0)),
                  pl.BlockSpec((K, D), lambda i: (0, 0)),
                  pl.BlockSpec((tm, D), lambda i: (i, 0)),
                  pl.BlockSpec((1, D), lambda i: (0, 0)),
                  pl.BlockSpec((1, D), lambda i: (0, 0))],
        out_specs=pl.BlockSpec((tm, D), lambda i: (i, 0)),
        compiler_params=_cparams("parallel"),
        name=name,
    )(o, w_o.astype(BF16), x, g.reshape(1, D), b.reshape(1, D))


def _scores(q, k, bias):
    s = lax.dot_general(q, k, (((1,), (1,)), ((), ())), preferred_element_type=F32)
    return s if bias is None else s + bias


def _causal_item_list(n_q, tq, tk):
    items = []
    for i in range(n_q):
        c_d = (i * tq) // tk
        items += [(i, c, c == 0, c == c_d) for c in range(c_d + 1)]
    return tuple(items)


def _flat_scratch(n_streams, M, t):
    return [pltpu.VMEM((n_streams, 2, M, t), F32), pltpu.VMEM((n_streams, 2, M, t), BF16),
            pltpu.VMEM((n_streams, M, LANE), F32), pltpu.VMEM((n_streams, 2, 3, M, LANE), F32)]


def _flash_static(items, streams, finalize, scratch, tq, tk):
    s_ref, p_ref, acc_ref, st_ref = scratch
    ns = len(streams)
    reps = tk // LANE
    n_rep = s_ref.shape[2] // tq

    def scores(n, item):
        q_at, k_at, _, bias_at = streams[n]
        return _scores(q_at(item[0]), k_at(item[1]), None if bias_at is None else bias_at(item[1]))

    def lane_partial_sum(p):
        return functools.reduce(jnp.add, [p[:, r * LANE:(r + 1) * LANE] for r in range(reps)])

    def value_update(n, prev, slot_prev):
        pv = jnp.dot(p_ref[n, slot_prev], streams[n][2](prev[1]), preferred_element_type=F32)
        acc_ref[n] = pv if prev[2] else st_ref[n, slot_prev, 2] * acc_ref[n] + pv

    def finish(prev, slot_prev):
        finalize(prev[0], [(jnp.sum(st_ref[n, slot_prev, 1], axis=-1, keepdims=True), acc_ref[n]) for n in range(ns)])

    for n in range(ns):
        s_ref[n, 0] = scores(n, items[0])
    for j, item in enumerate(items):
        slot = j % 2
        i, c, first, last = item
        if j + 1 < len(items):
            for n in range(ns):
                s_ref[n, 1 - slot] = scores(n, items[j + 1])
        if j > 0:
            for n in range(ns):
                value_update(n, items[j - 1], 1 - slot)
        vis = None
        if (c + 1) * tk > i * tq + 1:
            vis = _rel_pos(tq, tk) + (c * tk - i * tq) <= 0
            if n_rep > 1:
                vis = jnp.concatenate([vis] * n_rep, axis=0)
        for n in range(ns):
            s = s_ref[n, slot]
            if vis is not None:
                s = jnp.where(vis, s, NEG_INF)
            m_cur = jnp.max(s, axis=-1, keepdims=True)
            if first:
                m_new = jnp.broadcast_to(m_cur, (s.shape[0], LANE))
            else:
                m_old = st_ref[n, 1 - slot, 0]
                m_new = jnp.maximum(m_old, m_cur)
                alpha = jnp.exp2(m_old - m_new)
                st_ref[n, slot, 2] = alpha
            p = jnp.exp2(s - jnp.concatenate([m_new] * reps, axis=1))
            st_ref[n, slot, 0] = m_new
            st_ref[n, slot, 1] = lane_partial_sum(p) if first else alpha * st_ref[n, 1 - slot, 1] + lane_partial_sum(p)
            p_ref[n, slot] = p.astype(BF16)
        if j > 0 and items[j - 1][3]:
            finish(items[j - 1], 1 - slot)
    last_slot = (len(items) - 1) % 2
    for n in range(ns):
        value_update(n, items[-1], last_slot)
    finish(items[-1], last_slot)


def _window_attention(q, ks, vs, viss, biases=None):
    ss = []
    for j, (k, vis) in enumerate(zip(ks, viss)):
        s = _scores(q, k, None if biases is None else biases[j])
        ss.append(s if vis is None else jnp.where(vis, s, NEG_INF))
    m = jnp.max(functools.reduce(jnp.maximum, ss), axis=-1, keepdims=True)
    ps = [jnp.exp2(s - m) for s in ss]
    l = jnp.sum(functools.reduce(jnp.add, ps), axis=-1, keepdims=True)
    acc = functools.reduce(jnp.add, [jnp.dot(p.astype(BF16), v, preferred_element_type=F32) for p, v in zip(ps, vs)])
    return m, l, acc


def _rel_pos(rows, cols):
    return (lax.broadcasted_iota(jnp.int32, (rows, cols), 1)
            - lax.broadcasted_iota(jnp.int32, (rows, cols), 0))


def _rms_rows(c, g, eps):
    return c * lax.rsqrt(jnp.mean(c * c, axis=-1, keepdims=True) + eps) * g


def _mla_proj_kernel(x_ref, win_ref, qn_ref, kvn_ref, wq_ref, wqs_ref, wk_ref, wv_ref, cos_ref, sin_ref,
                     q_out, k_out, v_out, *, scale):
    xb = x_ref[...].astype(BF16)
    c = jnp.dot(xb, win_ref[...], preferred_element_type=F32)
    r0, r1 = MLA_Q_RANK, MLA_Q_RANK + MLA_KV_RANK
    cq = _rms_rows(c[:, :r0], qn_ref[...], 1e-6).astype(BF16)
    ckv = _rms_rows(c[:, r0:r1], kvn_ref[...], 1e-6).astype(BF16)
    cos, sin = cos_ref[...], sin_ref[...]
    kr = c[:, r1:r1 + LANE] * cos + c[:, r1 + LANE:r1 + 2 * LANE] * sin
    v_out[...] = jnp.dot(ckv, wv_ref[...], preferred_element_type=F32).astype(v_out.dtype)
    cos2, sin2, kr2 = (jnp.concatenate([t, t], axis=1) for t in (cos, sin, kr))
    for hp in range(MLA_HEADS // 2):
        sl = slice(2 * hp * LANE, 2 * (hp + 1) * LANE)
        qh = jnp.dot(cq, wq_ref[:, sl], preferred_element_type=F32)
        qhs = jnp.dot(cq, wqs_ref[:, sl], preferred_element_type=F32)
        q_out[:, sl] = ((qh * cos2 + qhs * sin2) * scale).astype(q_out.dtype)
        kh = jnp.dot(ckv, wk_ref[:, sl], preferred_element_type=F32)
        k_out[:, sl] = (kh + kr2).astype(k_out.dtype)


def _mla_attn_kernel(q_ref, k_ref, v_ref, o_ref, *scratch, tq, tk, items):
    def rows(i, t):
        return pl.ds(i * t, t)

    def v_at(c):
        return v_ref[0, rows(c, tk), :]

    streams = []
    for hh in range(2):
        sl = slice(hh * LANE, (hh + 1) * LANE)
        streams.append((lambda i, sl=sl: q_ref[0, rows(i, tq), sl], lambda c, sl=sl: k_ref[0, rows(c, tk), sl],
                        v_at, None))
    lane = lax.broadcasted_iota(jnp.int32, (tq, LANE), 1)

    def finalize(i, res):
        (la, acca), (lb, accb) = res
        o_ref[0, rows(i, tq), :] = jnp.where(lane < MLA_V, acca / la, accb / lb).astype(o_ref.dtype)

    _flash_static(items, streams, finalize, scratch, tq, tk)


def _mla_mixer(x, w_in, q_norm, kv_norm, w_qb, w_kvb, w_o, ln_g, ln_b, B, S):
    N, D = x.shape
    H, dq = MLA_HEADS, MLA_NOPE + MLA_ROPE
    half = MLA_ROPE // 2
    r0, r1 = MLA_Q_RANK, MLA_Q_RANK + MLA_KV_RANK
    z = lambda *s: jnp.zeros(s, F32)
    swap = lambda a: jnp.concatenate([a[..., half:], a[..., :half]], axis=-1)
    kr_w = w_in[:, r1:]
    win = jnp.concatenate([w_in[:, :r1],
                           z(D, MLA_NOPE), kr_w, z(D, LANE - dq),
                           z(D, MLA_NOPE), swap(kr_w), z(D, LANE - dq)], axis=1).astype(BF16)
    wq3 = w_qb.reshape(r0, H, dq)
    wq = jnp.concatenate([wq3, z(r0, H, LANE - dq)], axis=-1).reshape(r0, H * LANE).astype(BF16)
    wqs = jnp.concatenate([z(r0, H, MLA_NOPE), swap(wq3[..., MLA_NOPE:]), z(r0, H, LANE - dq)],
                          axis=-1).reshape(r0, H * LANE).astype(BF16)
    wkv3 = w_kvb.reshape(MLA_KV_RANK, H, MLA_NOPE + MLA_V)
    wk = jnp.concatenate([wkv3[..., :MLA_NOPE], z(MLA_KV_RANK, H, LANE - MLA_NOPE)],
                         axis=-1).reshape(MLA_KV_RANK, H * LANE).astype(BF16)
    wv = wkv3[..., MLA_NOPE:].reshape(MLA_KV_RANK, H * MLA_V).astype(BF16)
    freq = ROPE_THETA ** (-jnp.arange(half, dtype=F32) / half)
    ang = jnp.arange(S, dtype=F32)[:, None] * freq
    cos, sin = jnp.cos(ang), jnp.sin(ang)
    ones, zer = jnp.ones((S, MLA_NOPE), F32), jnp.zeros((S, LANE - dq), F32)
    cos_t = jnp.concatenate([ones, cos, cos, zer], axis=1)
    sin_t = jnp.concatenate([0 * ones, -sin, sin, zer], axis=1)

    tm = 256
    nwin = win.shape[1]
    q, k, v = pl.pallas_call(
        functools.partial(_mla_proj_kernel, scale=dq ** -0.5 * LOG2E),
        out_shape=[jax.ShapeDtypeStruct((N, H * LANE), BF16),
                   jax.ShapeDtypeStruct((N, H * LANE), BF16),
                   jax.ShapeDtypeStruct((N, H * MLA_V), BF16)],
        grid=(N // tm,),
        in_specs=[pl.BlockSpec((tm, D), lambda i: (i, 0)),
                  pl.BlockSpec((D, nwin), lambda i: (0, 0)),
                  pl.BlockSpec((1, r0), lambda i: (0, 0)),
                  pl.BlockSpec((1, MLA_KV_RANK), lambda i: (0, 0)),
                  pl.BlockSpec((r0, H * LANE), lambda i: (0, 0)),
                  pl.BlockSpec((r0, H * LANE), lambda i: (0, 0)),
                  pl.BlockSpec((MLA_KV_RANK, H * LANE), lambda i: (0, 0)),
                  pl.BlockSpec((MLA_KV_RANK, H * MLA_V), lambda i: (0, 0)),
                  pl.BlockSpec((tm, LANE), lambda i: (i % (S // tm), 0)),
                  pl.BlockSpec((tm, LANE), lambda i: (i % (S // tm), 0))],
        out_specs=[pl.BlockSpec((tm, H * LANE), lambda i: (i, 0)),
                   pl.BlockSpec((tm, H * LANE), lambda i: (i, 0)),
                   pl.BlockSpec((tm, H * MLA_V), lambda i: (i, 0))],
        compiler_params=_cparams("parallel"),
        name="mla_proj",
    )(x, win, q_norm.reshape(1, r0), kv_norm.reshape(1, MLA_KV_RANK), wq, wqs, wk, wv, cos_t, sin_t)

    tq, tk = ATTN_TQ, ATTN_TK
    o = pl.pallas_call(
        functools.partial(_mla_attn_kernel, tq=tq, tk=tk, items=_causal_item_list(S // tq, tq, tk)),
        out_shape=jax.ShapeDtypeStruct((B, S, H * MLA_V), BF16),
        grid=(B, H // 2),
        in_specs=[pl.BlockSpec((1, S, 2 * LANE), lambda b, h: (b, 0, h)),
                  pl.BlockSpec((1, S, 2 * LANE), lambda b, h: (b, 0, h)),
                  pl.BlockSpec((1, S, LANE), lambda b, h: (b, 0, h))],
        out_specs=pl.BlockSpec((1, S, LANE), lambda b, h: (b, 0, h)),
        scratch_shapes=_flat_scratch(2, tq, tk),
        compiler_params=_cparams("parallel", "parallel"),
        name="mla_attn",
    )(q.reshape(B, S, H * LANE), k.reshape(B, S, H * LANE), v.reshape(B, S, H * MLA_V))
    return _outproj_ln(o.reshape(N, H * MLA_V), w_o, x, ln_g, ln_b, name="mla_out_ln")


def _diff_attn_kernel(slope_ref, lam_ref, sub_ref, q_ref, k_ref, v_ref, o_ref, *scratch, tq, tk, items, lam_init):
    slope = slope_ref[pl.program_id(1)] * LOG2E
    lv = lam_ref[...]
    lam = (jnp.exp(jnp.sum(lv[0:1] * lv[1:2], axis=-1, keepdims=True))
           - jnp.exp(jnp.sum(lv[2:3] * lv[3:4], axis=-1, keepdims=True)) + lam_init)
    lane = lax.broadcasted_iota(jnp.int32, (tq, LANE), 1)
    colf = lax.broadcasted_iota(jnp.int32, (1, tk), 1).astype(F32)

    def rows(i, t):
        return pl.ds(i * t, t)

    def q_at(i, first_map):
        qf = q_ref[0, rows(i, tq), :].astype(F32) * (DIFF_HD ** -0.5 * LOG2E)
        return jnp.where((lane < DIFF_HD) == first_map, qf, 0.0).astype(BF16)

    def k_at(c):
        return k_ref[0, rows(c, tk), :]

    def v_at(c):
        return v_ref[0, rows(c, tk), :]

    def bias_at(c):
        return slope * (colf + float(c * tk))

    def finalize(i, res):
        (l0, acc0), (l1, acc1) = res
        a = _rms_rows(acc0 / l0 - lam * (acc1 / l1), sub_ref[...], 1e-5) * (1.0 - lam_init)
        o_ref[0, rows(i, tq), :] = a.astype(o_ref.dtype)

    streams = [(functools.partial(q_at, first_map=fm), k_at, v_at, bias_at) for fm in (True, False)]
    _flash_static(items, streams, finalize, scratch, tq, tk)


def _diff_mixer(x, w_in, lam_q1, lam_k1, lam_q2, lam_k2, subln, w_o, layer_idx, ln_g, ln_b, B, S):
    N, D = x.shape
    H, d = DIFF_HEADS, DIFF_HD
    nq = H * 2 * d
    (qkv,) = _linear(x, w_in.astype(BF16), [3 * nq], [BF16], name="diff_proj")
    qkv = qkv.reshape(B, S, 3 * nq)
    lam_init = 0.8 - 0.6 * math.exp(-0.3 * layer_idx)
    lamv = jnp.zeros((8, LANE), F32).at[:4, :d].set(jnp.stack([lam_q1, lam_k1, lam_q2, lam_k2]).astype(F32))
    tq, tk = ATTN_TQ, ATTN_TK
    o = pl.pallas_call(
        functools.partial(_diff_attn_kernel, tq=tq, tk=tk, items=_causal_item_list(S // tq, tq, tk), lam_init=lam_init),
        out_shape=jax.ShapeDtypeStruct((B, S, nq), BF16),
        grid=(B, H),
        in_specs=[pl.BlockSpec(memory_space=pltpu.SMEM),
                  pl.BlockSpec((8, LANE), lambda b, h: (0, 0)),
                  pl.BlockSpec((1, 2 * d), lambda b, h: (0, 0)),
                  pl.BlockSpec((1, S, LANE), lambda b, h: (b, 0, h)),
                  pl.BlockSpec((1, S, LANE), lambda b, h: (b, 0, H + h)),
                  pl.BlockSpec((1, S, LANE), lambda b, h: (b, 0, 2 * H + h))],
        out_specs=pl.BlockSpec((1, S, LANE), lambda b, h: (b, 0, h)),
        scratch_shapes=_flat_scratch(2, tq, tk),
        compiler_params=_cparams("parallel", "parallel"),
        name="diff_attn",
    )(jnp.asarray(_alibi(H)), lamv, subln.reshape(1, 2 * d).astype(F32), qkv, qkv, qkv)
    return _outproj_ln(o.reshape(N, nq), w_o, x, ln_g, ln_b, name="diff_out_ln")


SEL_LANE0 = NSA_HD
SEL_OFF = 1e30
ALIBI_LANE0 = 96
POS_SPLIT = 64


def _nsa_compress_kernel(ak_ref, av_ref, plo_ref, phi_ref, wklo_ref, wkhi_ref, wvlo_ref, wvhi_ref, kc_out, vc_out):
    def one(a_ref, wlo_ref, whi_ref, out):
        a = a_ref[0]
        lo = jnp.dot((a + plo_ref[...]).astype(BF16), wlo_ref[...], preferred_element_type=F32)
        hi = jnp.dot((a + phi_ref[...]).astype(BF16), whi_ref[...], preferred_element_type=F32)
        out[0] = (lo + pltpu.roll(hi, hi.shape[0] - 1, 0)).astype(out.dtype)

    one(ak_ref, wklo_ref, wkhi_ref, kc_out)
    one(av_ref, wvlo_ref, wvhi_ref, vc_out)


def _pack_heads(o, tq):
    lane = lax.broadcasted_iota(jnp.int32, (tq, LANE), 1)
    p01 = jnp.where(lane < NSA_HD, o[0:tq], o[tq:2 * tq])
    p23 = jnp.where(lane < NSA_HD, o[2 * tq:3 * tq], o[3 * tq:4 * tq])
    return jnp.concatenate([p01, p23], axis=1)


def _nsa_attn_kernel(q_ref, qc_ref, gl_ref, kc_ref, vc_ref, ks_ref, vs_ref, kw_ref, vw_ref, ktab_ref, ctab_ref,
                     c2s_ref, rep_ref, o_ref, selb_ref, ocw_ref, gs_ref, *scratch, tq, tk, tqs, tks, n_slc, items):
    hpg = NSA_HEADS // NSA_GROUPS

    def head_q(rows, j):
        return q_ref[0, rows, j * LANE:(j + 1) * LANE] + jnp.broadcast_to(qc_ref[0, j:j + 1, :],
                                                                         (rows.size, LANE)).astype(BF16)

    def tile(i, carry):
        _nsa_tile(i, head_q, gl_ref, kc_ref, vc_ref, kw_ref, vw_ref, ktab_ref, ctab_ref, c2s_ref, rep_ref,
                  selb_ref, ocw_ref, gs_ref, tq=tq, tk=tk, n_slc=n_slc)
        return carry

    lax.fori_loop(0, q_ref.shape[1] // tq, tile, 0, unroll=2)

    def rows_s(i, t):
        return pl.ds(i * t, t)

    def q_at(i, pair):
        r = rows_s(i, tqs)
        return jnp.concatenate([head_q(r, j) + selb_ref[r, :] for j in (2 * pair, 2 * pair + 1)], axis=0)

    def ks_at(c):
        return ks_ref[0, rows_s(c, tks), :] + ktab_ref[rows_s(c, tks), :]

    def vs_at(c):
        return vs_ref[0, rows_s(c, tks), :]

    def finalize(i, res):
        r = rows_s(i, tqs)
        o_s = jnp.concatenate([acc / l for l, acc in res], axis=0)
        o_ref[0, r, :] = (ocw_ref[r, :] + gs_ref[r, :] * _pack_heads(o_s, tqs)).astype(o_ref.dtype)

    streams = [(functools.partial(q_at, pair=pr), ks_at, vs_at, None) for pr in range(hpg // 2)]
    _flash_static(items, streams, finalize, scratch, tqs, tks)


def _nsa_tile(i, head_q, gl_ref, kc_ref, vc_ref, kw_ref, vw_ref, ktab_ref, ctab_ref, c2s_ref, rep_ref,
              selb_ref, ocw_ref, gs_ref, *, tq, tk, n_slc):
    t0 = i * tq
    rows = pl.ds(pl.multiple_of(t0, tq), tq)
    hpg = NSA_HEADS // NSA_GROUPS
    M = hpg * tq
    q = jnp.concatenate([head_q(rows, j) for j in range(hpg)], axis=0)
    trow1 = t0 + lax.broadcasted_iota(jnp.int32, (tq, 1), 0)
    trow = jnp.concatenate([trow1] * hpg, axis=0)
    lane_m = lax.broadcasted_iota(jnp.int32, (M, LANE), 1)

    sc = lax.dot_general(q, kc_ref[0] + ctab_ref[...], (((1,), (1,)), ((), ())), preferred_element_type=F32)
    sc = jnp.where(trow >= lane_m * NSA_CMP_STRIDE + (NSA_CMP_LEN - 1), sc, NEG_INF)
    e = jnp.exp2(sc - jnp.max(sc, axis=-1, keepdims=True))
    p_c = jnp.where(trow >= NSA_CMP_LEN - 1, e / jnp.sum(e, axis=-1, keepdims=True), 0.0)
    o_c = jnp.dot(p_c.astype(BF16), vc_ref[0], preferred_element_type=F32)
    psum = p_c[0:tq]
    for j in range(1, hpg):
        psum = psum + p_c[j * tq:(j + 1) * tq]
    ph, plw = _split_bf16(psum)
    imp = (jnp.dot(ph, c2s_ref[...], preferred_element_type=F32)
           + jnp.dot(plw, c2s_ref[...], preferred_element_type=F32))

    lane = lax.broadcasted_iota(jnp.int32, (tq, LANE), 1)
    sidx = lane - SEL_LANE0
    valid = (sidx >= 0) & (sidx < n_slc)
    forced = (sidx == 0) | (sidx == trow1 // NSA_SLC_LEN)
    future = sidx * NSA_SLC_LEN > trow1
    score = imp + jnp.where(forced, NSA_FORCE, 0.0) - jnp.where(future, 2.0 * NSA_FORCE, 0.0)
    score = jnp.where(valid, score, -jnp.inf)
    n_rows = ALIBI_LANE0 - SEL_LANE0
    st = score.T[SEL_LANE0:ALIBI_LANE0]
    blk = lax.broadcasted_iota(jnp.int32, (n_rows, tq), 0)
    beaten_by = jnp.zeros((n_rows, tq), jnp.int32)
    for other in range(n_slc):
        row = st[other:other + 1, :]
        beaten_by = beaten_by + jnp.where((row > st) | ((row == st) & (blk > other)), 1, 0)
    sel = beaten_by < min(NSA_TOP_N, n_slc)
    off = jnp.where(sel | (blk >= n_slc), 0.0, -SEL_OFF)
    selbias = jnp.concatenate([jnp.zeros((SEL_LANE0, tq), F32), off, jnp.zeros((LANE - ALIBI_LANE0, tq), F32)],
                              axis=0).T.astype(BF16)

    selb_ref[rows, :] = selbias

    c_d = t0 // tk
    d_diag = jnp.concatenate([_rel_pos(tq, tk)] * hpg, axis=0) + (c_d * tk - t0)

    def chunk(c):
        return pl.ds(pl.multiple_of(c * tk, tk), tk)

    cw = [jnp.maximum(c_d - 2, 0), jnp.maximum(c_d - 1, 0), c_d]
    vis_w = [(d_diag - 2 * tk > -NSA_WINDOW) & (c_d >= 2), jnp.broadcast_to(c_d >= 1, d_diag.shape), d_diag <= 0]
    _, l_w, acc_w = _window_attention(q, [kw_ref[0, chunk(c), :] + ktab_ref[chunk(c), :] for c in cw],
                                      [vw_ref[0, chunk(c), :] for c in cw], vis_w)
    o_w = acc_w / l_w

    gh, glw = _split_bf16(jax.nn.sigmoid(gl_ref[0, rows, :]))
    gr = jnp.dot(gh, rep_ref[0], preferred_element_type=F32) + jnp.dot(glw, rep_ref[0], preferred_element_type=F32)
    w = hpg * NSA_HD
    ocw_ref[rows, :] = gr[:, 0:w] * _pack_heads(o_c, tq) + gr[:, 2 * w:3 * w] * _pack_heads(o_w, tq)
    gs_ref[rows, :] = gr[:, w:2 * w]


def _nsa_mixer(x, w_in, w_phi_k, w_phi_v, cmp_pos, w_o, ln_g, ln_b, B, S):
    N, D = x.shape
    H, G, d = NSA_HEADS, NSA_GROUPS, NSA_HD
    hpg = H // G
    L, st = NSA_CMP_LEN, NSA_CMP_STRIDE
    n_slc = S // NSA_SLC_LEN
    assert S % 256 == 0 and n_slc <= ALIBI_LANE0 - SEL_LANE0 and L == 2 * st
    cuts = [H * d + i * G * d for i in range(7)]
    wq, wkc, wvc, wks, wvs, wkw, wvw, wgl = jnp.split(w_in, cuts, axis=1)
    z = lambda *s: jnp.zeros(s, F32)
    pad_heads = lambda w, n: jnp.concatenate([w.reshape(D, n, d), z(D, n, LANE - d)], axis=-1).reshape(D, n * LANE)
    dup_heads = lambda w, n: jnp.concatenate([w.reshape(D, n, d)] * 2, axis=-1).reshape(D, n * LANE)
    wgl_p = jnp.concatenate([wgl, z(D, LANE - wgl.shape[1])], axis=1)
    wcat = jnp.concatenate([pad_heads(wq * (d ** -0.5 * LOG2E), H), wkc, wvc, pad_heads(wks, G), dup_heads(wvs, G),
                            pad_heads(wkw, G), dup_heads(wvw, G), wgl_p], axis=1).astype(BF16)
    q, kc, vc, ks, vs, kw, vw, gl = _linear(
        x, wcat, [H * LANE, G * d, G * d, G * LANE, G * LANE, G * LANE, G * LANE, LANE],
        [BF16, F32, F32, BF16, BF16, BF16, BF16, F32], name="nsa_proj")

    nrow = S // st
    eye = jnp.eye(G, dtype=F32)

    def phi_w(w_phi, half, dup):
        w = w_phi.reshape(L, d, d)[half * st:(half + 1) * st]
        wd = jnp.concatenate([w, w if dup else jnp.zeros_like(w)], axis=-1)
        return jnp.einsum('ldc,gh->lgdhc', wd, eye).reshape(st * G * d, G * LANE).astype(BF16)

    pos = lambda half: jnp.broadcast_to(cmp_pos[half * st:(half + 1) * st, None, :], (st, G, d)).reshape(1, st * G * d)
    wide = st * G * d
    cspec = pl.BlockSpec((1, nrow, wide), lambda b: (b, 0, 0))
    wspec = pl.BlockSpec((wide, G * LANE), lambda b: (0, 0))
    pspec = pl.BlockSpec((1, wide), lambda b: (0, 0))
    ospec = pl.BlockSpec((1, nrow, G * LANE), lambda b: (b, 0, 0))
    kcmp, vcmp = pl.pallas_call(
        _nsa_compress_kernel,
        out_shape=[jax.ShapeDtypeStruct((B, nrow, G * LANE), BF16)] * 2,
        grid=(B,),
        in_specs=[cspec, cspec, pspec, pspec, wspec, wspec, wspec, wspec],
        out_specs=[ospec, ospec],
        compiler_params=_cparams("parallel"),
        name="nsa_compress",
    )(kc.reshape(B, nrow, wide), vc.reshape(B, nrow, wide), pos(0), pos(1),
      phi_w(w_phi_k, 0, False), phi_w(w_phi_k, 1, False), phi_w(w_phi_v, 0, True), phi_w(w_phi_v, 1, True))

    def pos_lanes(tab, pos):
        tab[:, ALIBI_LANE0:ALIBI_LANE0 + 3] = (pos // POS_SPLIT)[:, None]
        tab[:, ALIBI_LANE0 + 3:ALIBI_LANE0 + 6] = (pos % POS_SPLIT)[:, None]
        return tab

    ktab = np.zeros((S, LANE), np.float32)
    ktab[np.arange(S), SEL_LANE0 + np.arange(S) // NSA_SLC_LEN] = 1.0
    ktab = pos_lanes(ktab, np.arange(S))
    ctab = pos_lanes(np.zeros((nrow, LANE), np.float32), np.arange(nrow) * st + (L - 1))
    s2 = jnp.asarray(_alibi(H) * LOG2E, F32)
    parts = []
    for _ in range(3):
        part = s2.astype(BF16).astype(F32)
        parts.append(part)
        s2 = s2 - part
    qcoef = jnp.zeros((H, LANE), F32).at[:, ALIBI_LANE0:ALIBI_LANE0 + 6].set(
        jnp.stack([POS_SPLIT * p for p in parts] + parts, axis=1)).reshape(G, hpg, LANE)
    cmp_start = np.arange(nrow) * st
    slc_start = np.arange(n_slc) * NSA_SLC_LEN
    ov = (cmp_start[:, None] < slc_start[None, :] + NSA_SLC_LEN) & (cmp_start[:, None] + L > slc_start[None, :])
    ov[(S - L) // st + 1:] = False
    c2s = np.zeros((nrow, LANE), np.float32)
    c2s[:, SEL_LANE0:SEL_LANE0 + n_slc] = ov
    rep = np.zeros((G, LANE, 3 * hpg * d), np.float32)
    for g in range(G):
        for j in range(hpg):
            for br in range(3):
                rep[g, (g * hpg + j) * 3 + br, br * hpg * d + j * d:br * hpg * d + (j + 1) * d] = 1.0

    tq, tk = 256, 256
    tqs, tks = ATTN_TQ, ATTN_TK
    assert NSA_WINDOW == 2 * tk and tk % tq == 0
    kvspec = pl.BlockSpec((1, S, LANE), lambda b, g: (b, 0, g))
    cmpspec = pl.BlockSpec((1, nrow, LANE), lambda b, g: (b, 0, g))
    o = pl.pallas_call(
        functools.partial(_nsa_attn_kernel, tq=tq, tk=tk, tqs=tqs, tks=tks, n_slc=n_slc,
                          items=_causal_item_list(S // tqs, tqs, tks)),
        out_shape=jax.ShapeDtypeStruct((B, S, H * d), BF16),
        grid=(B, G),
        in_specs=[pl.BlockSpec((1, S, hpg * LANE), lambda b, g: (b, 0, g)),
                  pl.BlockSpec((1, hpg, LANE), lambda b, g: (g, 0, 0)),
                  pl.BlockSpec((1, S, LANE), lambda b, g: (b, 0, 0)),
                  cmpspec, cmpspec, kvspec, kvspec, kvspec, kvspec,
                  pl.BlockSpec((S, LANE), lambda b, g: (0, 0)),
                  pl.BlockSpec((nrow, LANE), lambda b, g: (0, 0)),
                  pl.BlockSpec((nrow, LANE), lambda b, g: (0, 0)),
                  pl.BlockSpec((1, LANE, 3 * hpg * d), lambda b, g: (g, 0, 0))],
        out_specs=pl.BlockSpec((1, S, hpg * d), lambda b, g: (b, 0, g)),
        scratch_shapes=[pltpu.VMEM((S, LANE), BF16), pltpu.VMEM((S, hpg * d), F32), pltpu.VMEM((S, hpg * d), F32)]
        + _flat_scratch(hpg // 2, 2 * tqs, tks),
        compiler_params=_cparams("parallel", "parallel"),
        name="nsa_attn",
    )(q.reshape(B, S, H * LANE), qcoef, gl.reshape(B, S, LANE), kcmp, vcmp,
      ks.reshape(B, S, G * LANE), vs.reshape(B, S, G * LANE), kw.reshape(B, S, G * LANE), vw.reshape(B, S, G * LANE),
      jnp.asarray(ktab, BF16), jnp.asarray(ctab, BF16), jnp.asarray(c2s, BF16), jnp.asarray(rep, BF16))
    return _outproj_ln(o.reshape(N, H * d), w_o, x, ln_g, ln_b, name="nsa_out_ln")


def _dil_attn_kernel(slope_ref, q_ref, k_ref, v_ref, o_ref, lse_ref, *, t, dil):
    i = pl.program_id(2)
    d = DIL_HD
    rel = _rel_pos(t, t)
    relf = rel.astype(F32)
    rel2 = jnp.concatenate([rel, rel], axis=0)
    lane = lax.broadcasted_iota(jnp.int32, (t, LANE), 1)
    scale = d ** -0.5 * LOG2E

    def chunk(c):
        return pl.ds(pl.multiple_of(c * t, t), t)

    cs = [jnp.maximum(i - 1, 0), i]
    viss = [(rel2 >= 0) & (i >= 1), rel2 <= 0]
    lse_t = jnp.zeros((t, LANE), F32)
    for hp in range(DIL_HEADS // 2):
        sl = slice(hp * LANE, (hp + 1) * LANE)
        qf = q_ref[0, :, sl].astype(F32) * scale
        q2 = jnp.concatenate([jnp.where(lane < d, qf, 0.0), jnp.where(lane < d, 0.0, qf)], axis=0).astype(BF16)
        sa, sb = slope_ref[2 * hp] * (dil * LOG2E), slope_ref[2 * hp + 1] * (dil * LOG2E)
        bias = jnp.concatenate([sa * relf, sb * relf], axis=0)
        shift = jnp.concatenate([jnp.full((t, 1), sa * t, F32), jnp.full((t, 1), sb * t, F32)], axis=0)
        m, l, acc = _window_attention(q2, [k_ref[0, chunk(c), sl] for c in cs], [v_ref[0, chunk(c), sl] for c in cs],
                                      viss, [bias - shift, bias])
        o = acc / l
        o_ref[0, :, hp * LANE:(hp + 1) * LANE] = jnp.where(lane < d, o[:t], o[t:]).astype(o_ref.dtype)
        lse = m + jnp.log2(l)
        lse_t = jnp.where(lane == 2 * hp, lse[:t], lse_t)
        lse_t = jnp.where(lane == 2 * hp + 1, lse[t:], lse_t)
    lse_ref[0] = lse_t


def _dil_proj_kernel(x_ref, w_ref, *refs, tm):
    outs, scr = refs[:-1], refs[-1]
    xb = x_ref[...].astype(BF16)
    width = scr.shape[0] * LANE
    for p, (o_ref, (_, dil)) in enumerate(zip(outs, DIL_PATTERNS)):
        for c0 in range(0, width, 512):
            y = jnp.dot(xb, w_ref[:, p * width + c0:p * width + c0 + 512], preferred_element_type=F32)
            if dil == 1:
                o_ref[:, c0:c0 + 512] = y.astype(o_ref.dtype)
            else:
                for j in range(512 // LANE):
                    scr[c0 // LANE + j] = y[:, j * LANE:(j + 1) * LANE]
        for c in range(dil if dil > 1 else 0):
            for j in range(width // LANE):
                o_ref[:, c * width + j * LANE:c * width + (j + 1) * LANE] = (
                    scr[j, pl.ds(c, tm // dil, stride=dil), :].astype(o_ref.dtype))


def _dil_merge_ln_kernel(o0_ref, o1_ref, o2_ref, l0_ref, l1_ref, l2_ref, rep_ref, w_ref, x_ref, g_ref, b_ref, out_ref,
                         scr_o, scr_l):
    tm = x_ref.shape[0]

    def token_order(ref, dil, scr):
        if dil == 1:
            return ref[...].astype(F32)
        w = ref.shape[1] // dil
        for c in range(dil):
            for j in range(w // LANE):
                scr[j, pl.ds(c, tm // dil, stride=dil), :] = (
                    ref[:, c * w + j * LANE:c * w + (j + 1) * LANE].astype(F32))
        return jnp.concatenate([scr[j] for j in range(w // LANE)], axis=1)

    ls = [token_order(l_ref, dil, scr_l) for l_ref, (_, dil) in zip((l0_ref, l1_ref, l2_ref), DIL_PATTERNS)]
    mx = jnp.maximum(jnp.maximum(ls[0], ls[1]), ls[2])
    es = [jnp.exp2(v - mx) for v in ls]
    tot = es[0] + es[1] + es[2]
    o = None
    for e, o_ref, (_, dil) in zip(es, (o0_ref, o1_ref, o2_ref), DIL_PATTERNS):
        wh, wl = _split_bf16(e / tot)
        wrep = jnp.dot(wh, rep_ref[...], preferred_element_type=F32) + jnp.dot(wl, rep_ref[...], preferred_element_type=F32)
        term = wrep * token_order(o_ref, dil, scr_o)
        o = term if o is None else o + term
    y = jnp.dot(o.astype(BF16), w_ref[...], preferred_element_type=F32)
    out_ref[...] = _layer_norm_rows(DN_ALPHA * x_ref[...] + y, g_ref[...], b_ref[...])


def _dil_mixer(x, w_in, w_o, ln_g, ln_b, B, S):
    N, D = x.shape
    H, d = DIL_HEADS, DIL_HD
    n_pat = len(DIL_PATTERNS)
    hd = H * d
    width = 3 * hd
    ncol = n_pat * width
    tm = 512
    qkvs = pl.pallas_call(
        functools.partial(_dil_proj_kernel, tm=tm),
        out_shape=[jax.ShapeDtypeStruct((N // dil, dil * width), BF16) for _, dil in DIL_PATTERNS],
        grid=(N // tm,),
        in_specs=[pl.BlockSpec((tm, D), lambda i: (i, 0)), pl.BlockSpec((D, ncol), lambda i: (0, 0))],
        out_specs=[pl.BlockSpec((tm // dil, dil * width), lambda i: (i, 0)) for _, dil in DIL_PATTERNS],
        scratch_shapes=[pltpu.VMEM((width // LANE, tm, LANE), F32)],
        compiler_params=_cparams("parallel"),
        name="dil_proj",
    )(x, w_in.astype(BF16))
    tq = 128
    slopes = jnp.asarray(_alibi(H))
    outs, lses = [], []
    for p, (win, dil) in enumerate(DIL_PATTERNS):
        ls = S // dil
        assert ls % tq == 0 and win == dil * tq
        view = qkvs[p].reshape(B, ls, dil * width)
        o, lse = pl.pallas_call(
            functools.partial(_dil_attn_kernel, t=tq, dil=dil),
            out_shape=[jax.ShapeDtypeStruct((B, ls, dil * hd), BF16),
                       jax.ShapeDtypeStruct((B, ls, dil * LANE), F32)],
            grid=(B, dil, ls // tq),
            in_specs=[pl.BlockSpec(memory_space=pltpu.SMEM),
                      pl.BlockSpec((1, tq, hd), lambda b, r, i: (b, i, 3 * r)),
                      pl.BlockSpec((1, ls, hd), lambda b, r, i: (b, 0, 3 * r + 1)),
                      pl.BlockSpec((1, ls, hd), lambda b, r, i: (b, 0, 3 * r + 2))],
            out_specs=[pl.BlockSpec((1, tq, hd), lambda b, r, i: (b, i, r)),
                       pl.BlockSpec((1, tq, LANE), lambda b, r, i: (b, i, r))],
            compiler_params=_cparams("parallel", "parallel", "arbitrary"),
            name=f"dil_attn_{p}",
        )(slopes, view, view, view)
        outs.append(o.reshape(N // dil, dil * hd))
        lses.append(lse.reshape(N // dil, dil * LANE))

    rep = np.zeros((LANE, hd), np.float32)
    for h in range(H):
        rep[h, h * d:(h + 1) * d] = 1.0
    row = lambda n: pl.BlockSpec((tm, n), lambda i: (i, 0))
    rowp = lambda n: [pl.BlockSpec((tm // dil, dil * n), lambda i: (i, 0)) for _, dil in DIL_PATTERNS]
    full = lambda a, b: pl.BlockSpec((a, b), lambda i: (0, 0))
    return pl.pallas_call(
        _dil_merge_ln_kernel,
        out_shape=jax.ShapeDtypeStruct((N, D), F32),
        grid=(N // tm,),
        in_specs=rowp(hd) + rowp(LANE) + [full(LANE, hd), full(hd, D), row(D), full(1, D), full(1, D)],
        out_specs=row(D),
        scratch_shapes=[pltpu.VMEM((hd // LANE, tm, LANE), F32), pltpu.VMEM((1, tm, LANE), F32)],
        compiler_params=_cparams("parallel"),
        name="dil_merge_out_ln",
    )(*outs, *lses, jnp.asarray(rep, BF16), w_o.astype(BF16), x, ln_g.reshape(1, D), ln_b.reshape(1, D))


def _split_bf16(a):
    hi = a.astype(BF16)
    return hi, (a - hi.astype(F32)).astype(BF16)


ROUTER_ROWS = 32
MOE_PAIRS = MOE_EPG * (MOE_EPG - 1) // 2
MOE_BUCKETS = MOE_GROUPS * MOE_PAIRS
PAIR_LO = [a for a in range(MOE_EPG) for b in range(a + 1, MOE_EPG)]
PAIR_HI = [b for a in range(MOE_EPG) for b in range(a + 1, MOE_EPG)]


def _router_kernel(x_ref, wh_ref, wl_ref, b_ref, tri_ref, meta_ref, bkt_ref, rank_ref, cnt_ref, run_ref):
    @pl.when(pl.program_id(0) == 0)
    def _():
        run_ref[...] = jnp.zeros_like(run_ref)

    xh, xl = _split_bf16(x_ref[...])
    wh, wl = wh_ref[...], wl_ref[...]
    nt = (((1,), (1,)), ((), ()))
    logits = (lax.dot_general(wh, xh, nt, preferred_element_type=F32)
              + lax.dot_general(wh, xl, nt, preferred_element_type=F32)
              + lax.dot_general(wl, xh, nt, preferred_element_type=F32)) + b_ref[...]
    nr, tm = logits.shape
    row = lax.broadcasted_iota(jnp.int32, (nr, tm), 0)
    big = jnp.int32(nr)
    lg = jnp.where(row < MOE_GROUPS, logits, -jnp.inf)
    mg = jnp.max(lg, axis=0, keepdims=True)
    sg = jnp.sum(jnp.exp(lg - mg), axis=0, keepdims=True)
    pg_top = 1.0 / sg
    g_top = jnp.min(jnp.where(lg == mg, row, big), axis=0, keepdims=True)
    e_lo = MOE_GROUPS + MOE_EPG * g_top
    in_grp = (row >= e_lo) & (row < e_lo + MOE_EPG)
    le = jnp.where(in_grp, logits, -jnp.inf)
    me = jnp.max(le, axis=0, keepdims=True)
    ee = jnp.exp(le - me)
    se = jnp.sum(ee, axis=0, keepdims=True)
    pe = jnp.where(in_grp, ee / se, -1.0)
    v1 = jnp.max(pe, axis=0, keepdims=True)
    i1 = jnp.min(jnp.where(pe == v1, row, big), axis=0, keepdims=True)
    pe2 = jnp.where(row == i1, -1.0, pe)
    v2 = jnp.max(pe2, axis=0, keepdims=True)
    i2 = jnp.min(jnp.where(pe2 == v2, row, big), axis=0, keepdims=True)
    tot = v1 + v2
    a1, a2 = i1 - e_lo, i2 - e_lo
    lo, hi = jnp.minimum(a1, a2), jnp.maximum(a1, a2)
    pair = lo * (MOE_EPG - 1) - (lo * (lo - 1)) // 2 + hi - lo - 1
    bucket = g_top * MOE_PAIRS + pair
    g_first, g_second = (v1 / tot) * pg_top, (v2 / tot) * pg_top
    onehot = row == bucket
    prefix = jnp.dot(jnp.where(onehot, 1.0, 0.0).astype(BF16), tri_ref[...], preferred_element_type=F32)
    rank = jnp.sum(jnp.where(onehot, prefix + run_ref[:, 0:1] - 1.0, 0.0), axis=0, keepdims=True)
    gates_t = (jnp.where(row == 0, jnp.where(a1 < a2, g_first, g_second), 0.0)
               + jnp.where(row == 1, jnp.where(a1 < a2, g_second, g_first), 0.0))
    meta_ref[...] = jnp.concatenate([gates_t, jnp.zeros((LANE - nr, tm), F32)], axis=0).T
    bkt_ref[0] = bucket
    rank_ref[0] = rank.astype(jnp.int32)
    run_ref[...] += jnp.broadcast_to(prefix[:, tm - 1:tm], run_ref.shape)
    cnt_ref[...] = run_ref[...]


def _moe_dispatch_kernel(starts_ref, ends_ref, bkt_ref, rank_ref, x_ref, meta_ref, xs_hbm, buf, sem):
    tm, d = x_ref.shape

    @pl.when(pl.program_id(0) == 0)
    def _():
        buf[...] = jnp.zeros_like(buf)

        def fill_tile(t, carry):
            fill = pltpu.make_async_copy(buf, xs_hbm.at[pl.ds(pl.multiple_of(t * tm, tm), tm)], sem)
            fill.start()
            fill.wait()
            return carry

        for b in range(MOE_BUCKETS):
            @pl.when(ends_ref[b] > starts_ref[b])
            def _(b=b):
                fill_tile(ends_ref[b] // tm - 1, 0)
        lax.fori_loop(ends_ref[MOE_BUCKETS - 1] // tm, xs_hbm.shape[0] // tm, fill_tile, 0)

    buf[:, :d] = x_ref[...]
    buf[:, d:] = meta_ref[...]

    def issue(r, carry):
        slot = starts_ref[bkt_ref[0, 0, r]] + rank_ref[0, 0, r]
        pltpu.make_async_copy(buf.at[pl.ds(r, 1)], xs_hbm.at[pl.ds(slot, 1)], sem).start()
        return carry

    lax.fori_loop(0, tm, issue, 0, unroll=DMA_ISSUE_UNROLL)
    pltpu.make_async_copy(buf, xs_hbm.at[pl.ds(0, tm)], sem).wait()


def _moe_pair_kernel(ea_ref, eb_ref, used_ref, xs_ref, w1a_ref, w3a_ref, w2a_ref, w1b_ref, w3b_ref, w2b_ref, g_ref,
                     b_ref, ys_ref):
    del ea_ref, eb_ref
    d = ys_ref.shape[1]

    @pl.when(pl.program_id(0) < used_ref[0])
    def _():
        x = xs_ref[:, :d]
        gates = xs_ref[:, d:]
        lane = lax.broadcasted_iota(jnp.int32, gates.shape, 1)
        xb = x.astype(BF16)
        y = None
        for e, (w1_ref, w3_ref, w2_ref) in enumerate(((w1a_ref, w3a_ref, w2a_ref), (w1b_ref, w3b_ref, w2b_ref))):
            h1 = jnp.dot(xb, w1_ref[0], preferred_element_type=F32)
            h3 = jnp.dot(xb, w3_ref[0], preferred_element_type=F32)
            hid = (h1 * jax.nn.sigmoid(h1) * h3).astype(BF16)
            ge = jnp.sum(jnp.where(lane == e, gates, 0.0), axis=-1, keepdims=True)
            term = ge * jnp.dot(hid, w2_ref[0], preferred_element_type=F32)
            y = term if y is None else y + term
        ys_ref[...] = _layer_norm_rows(DN_ALPHA * x + y, g_ref[...], b_ref[...])

    @pl.when(pl.program_id(0) >= used_ref[0])
    def _():
        ys_ref[...] = jnp.zeros_like(ys_ref)


def _moe_collect_kernel(starts_ref, bkt_ref, rank_ref, ys_hbm, out_ref, sem):
    tm = out_ref.shape[0]

    def issue(r, carry):
        slot = starts_ref[bkt_ref[0, 0, r]] + rank_ref[0, 0, r]
        pltpu.make_async_copy(ys_hbm.at[pl.ds(slot, 1)], out_ref.at[pl.ds(r, 1)], sem).start()
        return carry

    lax.fori_loop(0, tm, issue, 0, unroll=DMA_ISSUE_UNROLL)
    pltpu.make_async_copy(ys_hbm.at[pl.ds(0, tm)], out_ref, sem).wait()


def _hier_moe_ln(x, wg, bg, we, be, w1, w3, w2, ln_g, ln_b):
    N, D = x.shape
    G, E, FF, NB = MOE_GROUPS, MOE_EPG, MOE_FF, MOE_BUCKETS
    NR = ROUTER_ROWS
    assert max(G + MOE_EXPERTS, NB) <= NR
    wr = jnp.concatenate([wg, jnp.moveaxis(we, 0, 1).reshape(D, MOE_EXPERTS),
                          jnp.zeros((D, NR - G - MOE_EXPERTS), F32)], axis=1).T
    br = jnp.concatenate([bg, be.reshape(-1), jnp.zeros((NR - G - MOE_EXPERTS,), F32)])
    wrh, wrl = _split_bf16(wr)
    tm = MOE_TM
    tri = jnp.asarray(np.triu(np.ones((tm, tm), np.float32)), BF16)
    meta, bkt, rank, cnt = pl.pallas_call(
        _router_kernel,
        out_shape=[jax.ShapeDtypeStruct((N, LANE), F32), jax.ShapeDtypeStruct((N // tm, 1, tm), jnp.int32),
                   jax.ShapeDtypeStruct((N // tm, 1, tm), jnp.int32), jax.ShapeDtypeStruct((NR, LANE), F32)],
        grid=(N // tm,),
        in_specs=[pl.BlockSpec((tm, D), lambda i: (i, 0)),
                  pl.BlockSpec((NR, D), lambda i: (0, 0)),
                  pl.BlockSpec((NR, D), lambda i: (0, 0)),
                  pl.BlockSpec((NR, tm), lambda i: (0, 0)),
                  pl.BlockSpec((tm, tm), lambda i: (0, 0))],
        out_specs=[pl.BlockSpec((tm, LANE), lambda i: (i, 0)),
                   pl.BlockSpec((1, 1, tm), lambda i: (i, 0, 0)),
                   pl.BlockSpec((1, 1, tm), lambda i: (i, 0, 0)),
                   pl.BlockSpec((NR, LANE), lambda i: (0, 0))],
        scratch_shapes=[pltpu.VMEM((NR, LANE), F32)],
        compiler_params=_cparams("arbitrary"),
        name="moe_router",
    )(x, wrh, wrl, jnp.broadcast_to(br[:, None], (NR, tm)), tri)

    counts = cnt[:NB, 0].astype(jnp.int32)
    padded = (counts + tm - 1) // tm * tm
    ends = jnp.cumsum(padded)
    starts = ends - padded
    n_pad = N + NB * tm
    n_tiles = n_pad // tm
    tile_row0 = jnp.arange(n_tiles, dtype=jnp.int32) * tm
    tile_bkt = jnp.minimum(jnp.sum((ends[None, :] <= tile_row0[:, None]).astype(jnp.int32), axis=1), NB - 1)
    tile_ea = (tile_bkt // MOE_PAIRS) * E + jnp.asarray(PAIR_LO, jnp.int32)[tile_bkt % MOE_PAIRS]
    tile_eb = (tile_bkt // MOE_PAIRS) * E + jnp.asarray(PAIR_HI, jnp.int32)[tile_bkt % MOE_PAIRS]

    xs = pl.pallas_call(
        _moe_dispatch_kernel,
        out_shape=jax.ShapeDtypeStruct((n_pad, D + LANE), F32),
        grid_spec=pltpu.PrefetchScalarGridSpec(
            num_scalar_prefetch=2,
            grid=(N // tm,),
            in_specs=[pl.BlockSpec((1, 1, tm), lambda i, st, en: (i, 0, 0), memory_space=pltpu.SMEM),
                      pl.BlockSpec((1, 1, tm), lambda i, st, en: (i, 0, 0), memory_space=pltpu.SMEM),
                      pl.BlockSpec((tm, D), lambda i, st, en: (i, 0)),
                      pl.BlockSpec((tm, LANE), lambda i, st, en: (i, 0))],
            out_specs=pl.BlockSpec(memory_space=pl.ANY),
            scratch_shapes=[pltpu.VMEM((tm, D + LANE), F32), pltpu.SemaphoreType.DMA]),
        compiler_params=_cparams("arbitrary"),
        name="moe_dispatch",
    )(starts, ends, bkt, rank, x, meta)

    wa = lambda a, b: pl.BlockSpec((1, a, b), lambda t, ea, eb, nu: (ea[t], 0, 0))
    wb = lambda a, b: pl.BlockSpec((1, a, b), lambda t, ea, eb, nu: (eb[t], 0, 0))
    w1b, w3b, w2b = w1.astype(BF16), w3.astype(BF16), w2.astype(BF16)
    ys = pl.pallas_call(
        _moe_pair_kernel,
        out_shape=jax.ShapeDtypeStruct((n_pad, D), F32),
        grid_spec=pltpu.PrefetchScalarGridSpec(
            num_scalar_prefetch=3,
            grid=(n_tiles,),
            in_specs=[pl.BlockSpec((tm, D + LANE), lambda t, ea, eb, nu: (t, 0)),
                      wa(D, FF), wa(D, FF), wa(FF, D), wb(D, FF), wb(D, FF), wb(FF, D),
                      pl.BlockSpec((1, D), lambda t, ea, eb, nu: (0, 0)),
                      pl.BlockSpec((1, D), lambda t, ea, eb, nu: (0, 0))],
            out_specs=pl.BlockSpec((tm, D), lambda t, ea, eb, nu: (t, 0))),
        compiler_params=_cparams("arbitrary"),
        name="moe_experts_ln",
    )(tile_ea, tile_eb, (ends[-1:] // tm).astype(jnp.int32), xs, w1b, w3b, w2b, w1b, w3b, w2b,
      ln_g.reshape(1, D), ln_b.reshape(1, D))

    return pl.pallas_call(
        _moe_collect_kernel,
        out_shape=jax.ShapeDtypeStruct((N, D), F32),
        grid_spec=pltpu.PrefetchScalarGridSpec(
            num_scalar_prefetch=1,
            grid=(N // tm,),
            in_specs=[pl.BlockSpec((1, 1, tm), lambda i, st: (i, 0, 0), memory_space=pltpu.SMEM),
                      pl.BlockSpec((1, 1, tm), lambda i, st: (i, 0, 0), memory_space=pltpu.SMEM),
                      pl.BlockSpec(memory_space=pl.ANY)],
            out_specs=pl.BlockSpec((tm, D), lambda i, st: (i, 0)),
            scratch_shapes=[pltpu.SemaphoreType.DMA]),
        compiler_params=_cparams("arbitrary"),
        name="moe_collect",
    )(starts, bkt, rank, ys)


def kernel(x, mla_w_in, mla_q_norm, mla_kv_norm, mla_w_qb, mla_w_kvb, mla_w_o, nsa_w_in, nsa_w_phi_k, nsa_w_phi_v, nsa_cmp_pos, nsa_w_o, diff_w_in, diff_lam_q1, diff_lam_k1, diff_lam_q2, diff_lam_k2, diff_subln, diff_w_o, dil_w_in, dil_w_o, ln1_g, ln1_b, ln2_g, ln2_b, moe_wg, moe_bg, moe_we, moe_be, moe_w1, moe_w3, moe_w2):
    B, S, D = x.shape
    h = x.reshape(B * S, D)
    for i in range(DEPTH):
        m, j = i % 4, i // 4
        if m == 0:
            h = _mla_mixer(h, mla_w_in[j], mla_q_norm[j], mla_kv_norm[j], mla_w_qb[j], mla_w_kvb[j], mla_w_o[j],
                           ln1_g[i], ln1_b[i], B, S)
        elif m == 1:
            h = _nsa_mixer(h, nsa_w_in[j], nsa_w_phi_k[j], nsa_w_phi_v[j], nsa_cmp_pos[j], nsa_w_o[j],
                           ln1_g[i], ln1_b[i], B, S)
        elif m == 2:
            h = _diff_mixer(h, diff_w_in[j], diff_lam_q1[j], diff_lam_k1[j], diff_lam_q2[j], diff_lam_k2[j],
                            diff_subln[j], diff_w_o[j], i, ln1_g[i], ln1_b[i], B, S)
        else:
            h = _dil_mixer(h, dil_w_in[j], dil_w_o[j], ln1_g[i], ln1_b[i], B, S)
        h = _hier_moe_ln(h, moe_wg[i], moe_bg[i], moe_we[i], moe_be[i], moe_w1[i], moe_w3[i], moe_w2[i],
                         ln2_g[i], ln2_b[i])
    return h.reshape(B, S, D)
```
